```python
import math
import jax
import jax.numpy as jnp
from jax import lax
import numpy as np

D_MODEL = 2048
BATCH = 8
SEQ = 8192
DEPTH = 4

N_BRANCHES = 4
BRANCH_WIDTH = D_MODEL // 4
D_FF = 4 * D_MODEL
NORM_EPS = 1e-6

S5_GROUP = 16
S5_GROUPS = BRANCH_WIDTH // S5_GROUP
S5_STATE = 64
S5_DT_MIN = 1e-3
S5_DT_MAX = 1e-1

GDN_HEAD_DIM = 128
GDN_HEADS = BRANCH_WIDTH // GDN_HEAD_DIM
GDN_CONV = 4
GDN_CHUNK = 64

SWA_HEAD_DIM = 64
SWA_HEADS = BRANCH_WIDTH // SWA_HEAD_DIM
SWA_KV_HEADS = SWA_HEADS // 4
WINDOW = 128
SWA_BLOCK = 128
T5_BUCKETS = 32
T5_MAX_DISTANCE = 128

MLA_HEADS = BRANCH_WIDTH // 128
MLA_Q_RANK = 384
MLA_KV_RANK = 512
MLA_NOPE = 128
MLA_ROPE = 64
MLA_V = 128
ROPE_THETA = 10000.0
ATTN_BLOCK = 128

IN_SPLITS = (
    BRANCH_WIDTH,
    3 * BRANCH_WIDTH,
    BRANCH_WIDTH,
    2 * GDN_HEADS,
    2 * GDN_HEADS,
    SWA_HEADS * SWA_HEAD_DIM,
    2 * SWA_KV_HEADS * SWA_HEAD_DIM,
    MLA_Q_RANK,
    MLA_KV_RANK,
    MLA_ROPE,
    N_BRANCHES * D_MODEL,
)
D_IN = sum(IN_SPLITS)

kernel_name = 'hybrid_parallel_gated_encoder'


def rms_norm(x, gain):
    xf = x.astype(jnp.float32)
    y = xf * lax.rsqrt(jnp.mean(xf * xf, axis=-1, keepdims=True) + NORM_EPS)
    return (y * gain.astype(jnp.float32)).astype(x.dtype)


def _cmul(ar, ai, br, bi):
    return ar * br - ai * bi, ar * bi + ai * br


def _linear_recurrence_op(left, right):
    a1r, a1i, b1r, b1i = left
    a2r, a2i, b2r, b2i = right
    ar, ai = _cmul(a2r, a2i, a1r, a1i)
    br, bi = _cmul(a2r, a2i, b1r, b1i)
    return ar, ai, br + b2r, bi + b2i


def s5_direction(u, lam_re, lam_im, log_step, b_re, b_im, c_re, c_im, reverse):
    f = lambda t: t.astype(jnp.float32)
    lam_re = jnp.minimum(f(lam_re), -1e-4)
    lam_im = f(lam_im)
    dt = jnp.exp(f(log_step))[:, None]
    mag = jnp.exp(lam_re * dt)
    abar_r = mag * jnp.cos(lam_im * dt)
    abar_i = mag * jnp.sin(lam_im * dt)
    den = lam_re * lam_re + lam_im * lam_im
    xr = abar_r - 1.0
    xi = abar_i
    coef_r = (xr * lam_re + xi * lam_im) / den
    coef_i = (xi * lam_re - xr * lam_im) / den
    b_re, b_im = f(b_re), f(b_im)
    bbar_r = coef_r[..., None] * b_re - coef_i[..., None] * b_im
    bbar_i = coef_r[..., None] * b_im + coef_i[..., None] * b_re
    bu_r = jnp.einsum('blgh,gph->blgp', u, bbar_r)
    bu_i = jnp.einsum('blgh,gph->blgp', u, bbar_i)
    a_r = jnp.broadcast_to(abar_r, bu_r.shape)
    a_i = jnp.broadcast_to(abar_i, bu_i.shape)
    _, _, s_r, s_i = lax.associative_scan(
        _linear_recurrence_op, (a_r, a_i, bu_r, bu_i), reverse=reverse, axis=1)
    return (jnp.einsum('blgp,ghp->blgh', s_r, f(c_re))
            - jnp.einsum('blgp,ghp->blgh', s_i, f(c_im)))


def s5_mixer(u, lam_re, lam_im, log_step, b_re, b_im, c_re, c_im, d_skip, w_glu, b_glu):
    B_, L, W = u.shape
    uf = u.astype(jnp.float32)
    ug = uf.reshape(B_, L, S5_GROUPS, S5_GROUP)
    y = s5_direction(ug, lam_re[0], lam_im[0], log_step[0], b_re[0], b_im[0],
                     c_re[0], c_im[0], reverse=False)
    y = y + s5_direction(ug, lam_re[1], lam_im[1], log_step[1], b_re[1], b_im[1],
                         c_re[1], c_im[1], reverse=True)
    y = y.reshape(B_, L, W) + d_skip.astype(jnp.float32) * uf
    y = jax.nn.gelu(y)
    y = y * jax.nn.sigmoid(y @ w_glu.astype(jnp.float32) + b_glu.astype(jnp.float32))
    return y.astype(u.dtype)


def l2_normalize(x):
    xf = x.astype(jnp.float32)
    return xf * lax.rsqrt(jnp.sum(xf * xf, axis=-1, keepdims=True) + 1e-6)


def depthwise_conv_centred(x, w):
    K = w.shape[0]
    return lax.conv_general_dilated(
        x, w.astype(x.dtype)[:, None, :], window_strides=(1,),
        padding=[(K // 2, K - 1 - K // 2)],
        dimension_numbers=('NWC', 'WIO', 'NWC'),
        feature_group_count=x.shape[-1])


def gated_delta_rule_chunked(q, k, v, g, beta):
    B_, L, H, dk = q.shape
    dv = v.shape[-1]
    C = GDN_CHUNK
    N = L // C

    def chunks(t):
        return t.reshape(B_, N, C, H, -1).transpose(0, 3, 1, 2, 4)

    q, k, v = chunks(q), chunks(k), chunks(v)
    g = jnp.cumsum(g.reshape(B_, N, C, H).transpose(0, 3, 1, 2), axis=-1)
    beta = beta.reshape(B_, N, C, H).transpose(0, 3, 1, 2)[..., None]
    k_beta = k * beta
    lower = jnp.tril(jnp.ones((C, C), dtype=bool))
    strict = jnp.tril(jnp.ones((C, C), dtype=bool), -1)
    decay = jnp.exp(jnp.where(lower, g[..., :, None] - g[..., None, :], -jnp.inf))
    a_mat = jnp.where(strict, jnp.einsum('bhnik,bhnjk->bhnij', k_beta, k) * decay, 0.0)
    a_mat = a_mat + jnp.eye(C, dtype=q.dtype)
    rhs = jnp.concatenate([v * beta, k_beta * jnp.exp(g)[..., None]], axis=-1)
    sol = lax.linalg.triangular_solve(a_mat, rhs, left_side=True, lower=True,
                                      unit_diagonal=True)
    u, w = sol[..., :dv], sol[..., dv:]
    attn = jnp.einsum('bhnik,bhnjk->bhnij', q, k) * decay

    def step(S, inp):
        q_c, k_c, u_c, w_c, g_c, a_c = inp
        v_new = u_c - jnp.einsum('bhck,bhkv->bhcv', w_c, S)
        o = (jnp.einsum('bhck,bhkv->bhcv', q_c * jnp.exp(g_c)[..., None], S)
             + jnp.einsum('bhcj,bhjv->bhcv', a_c, v_new))
        g_last = g_c[..., -1]
        S = (S * jnp.exp(g_last)[..., None, None]
             + jnp.einsum('bhck,bhcv->bhkv', k_c * jnp.exp(g_last[..., None] - g_c)[..., None], v_new))
        return S, o

    s0 = jnp.zeros((B_, H, dk, dv), q.dtype)
    xs = tuple(jnp.moveaxis(t, 2, 0) for t in (q, k, u, w, g, attn))
    _, o = lax.scan(step, s0, xs)
    return jnp.moveaxis(o, 0, 2).transpose(0, 2, 3, 1, 4).reshape(B_, L, H, dv)


def gdn_mixer(qkv, z, beta_logits, decay_logits, conv_w, a_log, dt_bias, o_gain):
    B_, L, _ = qkv.shape
    H, Dh = GDN_HEADS, GDN_HEAD_DIM
    qkv = jax.nn.silu(depthwise_conv_centred(qkv, conv_w))
    q, k, v = jnp.split(qkv, 3, axis=-1)
    q = l2_normalize(q.reshape(B_, L, H, Dh)) * (Dh ** -0.5)
    k = l2_normalize(k.reshape(B_, L, H, Dh))
    v = v.reshape(B_, L, H, Dh).astype(jnp.float32)
    beta = jax.nn.sigmoid(beta_logits.astype(jnp.float32)).reshape(B_, L, 2, H)
    g = -jnp.exp(a_log.astype(jnp.float32)) * jax.nn.softplus(
        decay_logits.astype(jnp.float32).reshape(B_, L, 2, H) + dt_bias.astype(jnp.float32))
    o_fwd = gated_delta_rule_chunked(q, k, v, g[:, :, 0], beta[:, :, 0])
    flip = lambda t: jnp.flip(t, axis=1)
    o_bwd = flip(gated_delta_rule_chunked(flip(q), flip(k), flip(v),
                                          flip(g[:, :, 1]), flip(beta[:, :, 1])))
    o = rms_norm(o_fwd + o_bwd, o_gain)
    o = o * jax.nn.silu(z.astype(jnp.float32).reshape(B_, L, H, Dh))
    return o.reshape(B_, L, H * Dh).astype(z.dtype)


def t5_bucket(rel):
    nb = T5_BUCKETS // 2
    max_exact = nb // 2
    ret = jnp.where(rel > 0, nb, 0)
    n = jnp.abs(rel)
    nf = jnp.maximum(n, 1).astype(jnp.float32)
    large = max_exact + (jnp.log(nf / max_exact) / math.log(T5_MAX_DISTANCE / max_exact)
                         * (nb - max_exact)).astype(jnp.int32)
    large = jnp.minimum(large, nb - 1)
    return ret + jnp.where(n < max_exact, n, large)


def swa_mixer(q, kv, sink, t5_bias):
    B_, L, _ = q.shape
    NB = L // SWA_BLOCK
    G = SWA_HEADS // SWA_KV_HEADS
    q = q.reshape(B_, NB, SWA_BLOCK, SWA_KV_HEADS, G, SWA_HEAD_DIM)
    k, v = jnp.split(kv, 2, axis=-1)

    def band(t):
        t = jnp.pad(t, ((0, 0), (SWA_BLOCK, SWA_BLOCK), (0, 0)))
        t = t.reshape(B_, NB + 2, SWA_BLOCK, SWA_KV_HEADS, SWA_HEAD_DIM)
        return jnp.concatenate([t[:, :-2], t[:, 1:-1], t[:, 2:]], axis=2)

    kb, vb = band(k), band(v)
    qi = jnp.arange(SWA_BLOCK)[:, None]
    sj = jnp.arange(3 * SWA_BLOCK)[None, :]
    rel = sj - SWA_BLOCK - qi
    bias = t5_bias[t5_bucket(rel)].astype(jnp.float32)
    bias = bias.transpose(2, 0, 1).reshape(SWA_KV_HEADS, G, SWA_BLOCK, 3 * SWA_BLOCK)
    kpos = jnp.arange(NB)[:, None] * SWA_BLOCK + sj - SWA_BLOCK
    valid = (jnp.abs(rel) <= WINDOW)[None] & ((kpos >= 0) & (kpos < L))[:, None, :]
    logits = (jnp.einsum('bnqkgd,bnskd->bnkgqs', q, kb).astype(jnp.float32)
              * (SWA_HEAD_DIM ** -0.5) + bias)
    logits = jnp.where(valid[None, :, None, None], logits, -1e30)
    sink_col = jnp.broadcast_to(
        sink.astype(jnp.float32).reshape(SWA_KV_HEADS, G, 1, 1), logits.shape[:-1] + (1,))
    probs = jax.nn.softmax(jnp.concatenate([logits, sink_col], axis=-1), axis=-1)[..., :-1]
    out = jnp.einsum('bnkgqs,bnskd->bnqkgd', probs.astype(vb.dtype), vb)
    return out.reshape(B_, L, SWA_HEADS * SWA_HEAD_DIM)


def apply_rope(x, cos, sin):
    x1, x2 = jnp.split(x, 2, axis=-1)
    return jnp.concatenate([x1 * cos - x2 * sin, x2 * cos + x1 * sin], axis=-1)


def mla_mixer(c_q, c_kv, k_rope, q_gain, kv_gain, w_uq, w_ukv):
    B_, L, _ = c_q.shape
    H = MLA_HEADS
    q = (rms_norm(c_q, q_gain) @ w_uq).reshape(B_, L, H, MLA_NOPE + MLA_ROPE)
    kv = (rms_norm(c_kv, kv_gain) @ w_ukv).reshape(B_, L, H, MLA_NOPE + MLA_V)
    q_nope, q_pe = q[..., :MLA_NOPE], q[..., MLA_NOPE:]
    k_nope, v = kv[..., :MLA_NOPE], kv[..., MLA_NOPE:]
    pos = jnp.arange(L, dtype=jnp.float32)
    inv_freq = ROPE_THETA ** (-jnp.arange(0, MLA_ROPE, 2, dtype=jnp.float32) / MLA_ROPE)
    ang = pos[:, None] * inv_freq[None, :]
    cos, sin = jnp.cos(ang)[:, None, :], jnp.sin(ang)[:, None, :]
    q_pe = apply_rope(q_pe.astype(jnp.float32), cos, sin)
    k_pe = apply_rope(k_rope.astype(jnp.float32)[:, :, None, :], cos, sin)
    qf = jnp.concatenate([q_nope.astype(jnp.float32), q_pe], axis=-1).astype(c_q.dtype)
    kf = jnp.concatenate([k_nope.astype(jnp.float32),
                          jnp.broadcast_to(k_pe, (B_, L, H, MLA_ROPE))], axis=-1).astype(c_q.dtype)
    scale = (MLA_NOPE + MLA_ROPE) ** -0.5
    NB = L // ATTN_BLOCK
    qb = qf.reshape(B_, NB, ATTN_BLOCK, H, MLA_NOPE + MLA_ROPE).transpose(1, 0, 2, 3, 4)

    def attend(qblk):
        s = jnp.einsum('bqhd,bshd->bhqs', qblk, kf).astype(jnp.float32) * scale
        p = jax.nn.softmax(s, axis=-1)
        return jnp.einsum('bhqs,bshd->bqhd', p.astype(v.dtype), v)

    out = lax.map(attend, qb)
    return out.transpose(1, 0, 2, 3, 4).reshape(B_, L, H * MLA_V)


def _fwd_setup_inputs(seed: int = 0) -> dict:
    key = jax.random.key(seed)
    ks = jax.random.split(key, 28)
    f32 = jnp.float32

    def normal(k, shape, scale):
        return jax.random.normal(k, shape, f32) * scale

    G, P, Hg = S5_GROUPS, S5_STATE, S5_GROUP
    W = BRANCH_WIDTH
    gk = jax.random.split(ks[24], 4)
    gdn_dt = jnp.exp(jax.random.uniform(ks[14], (DEPTH, 2, GDN_HEADS), f32,
                                        math.log(1e-3), math.log(1e-1)))
    return {
        'x': jax.random.normal(ks[0], (BATCH, SEQ, D_MODEL), f32),
        'w_in': normal(ks[1], (DEPTH, D_MODEL, D_IN), D_MODEL ** -0.5),
        's5_lam_re': -0.5 + normal(ks[2], (DEPTH, 2, G, P), 0.01),
        's5_lam_im': math.pi * jnp.arange(P, dtype=f32) + normal(ks[3], (DEPTH, 2, G, P), 0.01),
        's5_log_step': jax.random.uniform(ks[4], (DEPTH, 2, G), f32,
                                          math.log(S5_DT_MIN), math.log(S5_DT_MAX)),
        's5_b_re': normal(ks[5], (DEPTH, 2, G, P, Hg), (2 * Hg) ** -0.5),
        's5_b_im': normal(ks[6], (DEPTH, 2, G, P, Hg), (2 * Hg) ** -0.5),
        's5_c_re': normal(ks[7], (DEPTH, 2, G, Hg, P), P ** -0.5),
        's5_c_im': normal(ks[8], (DEPTH, 2, G, Hg, P), P ** -0.5),
        's5_d': normal(ks[9], (DEPTH, W), 1.0),
        's5_w_glu': normal(ks[10], (DEPTH, W, W), W ** -0.5),
        's5_b_glu': normal(ks[11], (DEPTH, W), 0.01),
        'gdn_conv': normal(ks[12], (DEPTH, GDN_CONV, 3 * W), GDN_CONV ** -0.5),
        'gdn_a_log': jnp.log(jax.random.uniform(ks[13], (DEPTH, 2, GDN_HEADS), f32, 1.0, 16.0)),
        'gdn_dt_bias': gdn_dt + jnp.log(-jnp.expm1(-gdn_dt)),
        'gdn_o_gain': 1.0 + normal(ks[15], (DEPTH, GDN_HEAD_DIM), 0.02),
        'swa_sink': normal(ks[16], (DEPTH, SWA_HEADS), 0.5),
        't5_bias': normal(ks[17], (T5_BUCKETS, SWA_HEADS), 0.5),
        'mla_q_gain': 1.0 + normal(ks[18], (DEPTH, MLA_Q_RANK), 0.02),
        'mla_kv_gain': 1.0 + normal(ks[19], (DEPTH, MLA_KV_RANK), 0.02),
        'mla_w_uq': normal(ks[20], (DEPTH, MLA_Q_RANK, MLA_HEADS * (MLA_NOPE + MLA_ROPE)),
                           MLA_Q_RANK ** -0.5),
        'mla_w_ukv': normal(ks[21], (DEPTH, MLA_KV_RANK, MLA_HEADS * (MLA_NOPE + MLA_V)),
                            MLA_KV_RANK ** -0.5),
        'w_branch': normal(ks[22], (DEPTH, N_BRANCHES, W, D_MODEL), W ** -0.5),
        'w_out': normal(ks[23], (DEPTH, D_MODEL, D_MODEL), D_MODEL ** -0.5),
        'mix_pre_gain': 1.0 + normal(gk[0], (DEPTH, D_MODEL), 0.02),
        'mix_post_gain': 1.0 + normal(gk[1], (DEPTH, D_MODEL), 0.02),
        'mlp_pre_gain': 1.0 + normal(gk[2], (DEPTH, D_MODEL), 0.02),
        'mlp_post_gain': 1.0 + normal(gk[3], (DEPTH, D_MODEL), 0.02),
        'w_mlp_in': normal(ks[25], (DEPTH, D_MODEL, D_FF), D_MODEL ** -0.5),
        'w_mlp_out': normal(ks[26], (DEPTH, D_FF, D_MODEL), D_FF ** -0.5),
    }


def _fwd_reference(x, w_in, s5_lam_re, s5_lam_im, s5_log_step, s5_b_re, s5_b_im, s5_c_re,
              s5_c_im, s5_d, s5_w_glu, s5_b_glu, gdn_conv, gdn_a_log, gdn_dt_bias,
              gdn_o_gain, swa_sink, t5_bias, mla_q_gain, mla_kv_gain, mla_w_uq, mla_w_ukv,
              w_branch, w_out, mix_pre_gain, mix_post_gain, mlp_pre_gain, mlp_post_gain,
              w_mlp_in, w_mlp_out):
    B_, L, _ = x.shape
    split_points = np.cumsum(IN_SPLITS)[:-1].tolist()
    for l in range(DEPTH):
        h = rms_norm(x, mix_pre_gain[l])
        (s5_u, gdn_qkv, gdn_z, gdn_beta, gdn_decay, swa_q, swa_kv,
         mla_cq, mla_ckv, mla_kr, gate_logits) = jnp.split(h @ w_in[l], split_points, axis=-1)
        y_a = s5_mixer(s5_u, s5_lam_re[l], s5_lam_im[l], s5_log_step[l], s5_b_re[l],
                       s5_b_im[l], s5_c_re[l], s5_c_im[l], s5_d[l], s5_w_glu[l], s5_b_glu[l])
        y_b = gdn_mixer(gdn_qkv, gdn_z, gdn_beta, gdn_decay, gdn_conv[l], gdn_a_log[l],
                        gdn_dt_bias[l], gdn_o_gain[l])
        y_c = swa_mixer(swa_q, swa_kv, swa_sink[l], t5_bias)
        y_d = mla_mixer(mla_cq, mla_ckv, mla_kr, mla_q_gain[l], mla_kv_gain[l],
                        mla_w_uq[l], mla_w_ukv[l])
        ys = jnp.stack([y_a, y_b, y_c, y_d], axis=2)
        branch = jnp.einsum('blnw,nwd->blnd', ys, w_branch[l])
        gates = jax.nn.sigmoid(gate_logits.reshape(B_, L, N_BRANCHES, D_MODEL))
        merged = jnp.sum(gates * branch, axis=2)
        x = x + rms_norm(merged @ w_out[l], mix_post_gain[l])
        h = rms_norm(x, mlp_pre_gain[l])
        f = jnp.square(jax.nn.relu(h @ w_mlp_in[l])) @ w_mlp_out[l]
        x = x + rms_norm(f, mlp_post_gain[l])
    return x


import jax as _jax
import jax.numpy as _jnp

TWIN_FORMAT = 'train_step'
FWD_PARAMS = ['x', 'w_in', 's5_lam_re', 's5_lam_im', 's5_log_step', 's5_b_re', 's5_b_im', 's5_c_re', 's5_c_im', 's5_d', 's5_w_glu', 's5_b_glu', 'gdn_conv', 'gdn_a_log', 'gdn_dt_bias', 'gdn_o_gain', 'swa_sink', 't5_bias', 'mla_q_gain', 'mla_kv_gain', 'mla_w_uq', 'mla_w_ukv', 'w_branch', 'w_out', 'mix_pre_gain', 'mix_post_gain', 'mlp_pre_gain', 'mlp_post_gain', 'w_mlp_in', 'w_mlp_out']
TWIN_WEIGHTS = ['w_in', 's5_lam_re', 's5_lam_im', 's5_log_step', 's5_b_re', 's5_b_im', 's5_c_re', 's5_c_im', 's5_d', 's5_w_glu', 's5_b_glu', 'gdn_conv', 'gdn_a_log', 'gdn_dt_bias', 'gdn_o_gain', 'swa_sink', 't5_bias', 'mla_q_gain', 'mla_kv_gain', 'mla_w_uq', 'mla_w_ukv', 'w_branch', 'w_out', 'mix_pre_gain', 'mix_post_gain', 'mlp_pre_gain', 'mlp_post_gain', 'w_mlp_in', 'w_mlp_out']
TWIN_DIFF_INPUT = 'x'
TWIN_INPUTS = ['x', 'w_in', 's5_lam_re', 's5_lam_im', 's5_log_step', 's5_b_re', 's5_b_im', 's5_c_re', 's5_c_im', 's5_d', 's5_w_glu', 's5_b_glu', 'gdn_conv', 'gdn_a_log', 'gdn_dt_bias', 'gdn_o_gain', 'swa_sink', 't5_bias', 'mla_q_gain', 'mla_kv_gain', 'mla_w_uq', 'mla_w_ukv', 'w_branch', 'w_out', 'mix_pre_gain', 'mix_post_gain', 'mlp_pre_gain', 'mlp_post_gain', 'w_mlp_in', 'w_mlp_out', 'loss_target', 'm_w_in', 'm_s5_lam_re', 'm_s5_lam_im', 'm_s5_log_step', 'm_s5_b_re', 'm_s5_b_im', 'm_s5_c_re', 'm_s5_c_im', 'm_s5_d', 'm_s5_w_glu', 'm_s5_b_glu', 'm_gdn_conv', 'm_gdn_a_log', 'm_gdn_dt_bias', 'm_gdn_o_gain', 'm_swa_sink', 'm_t5_bias', 'm_mla_q_gain', 'm_mla_kv_gain', 'm_mla_w_uq', 'm_mla_w_ukv', 'm_w_branch', 'm_w_out', 'm_mix_pre_gain', 'm_mix_post_gain', 'm_mlp_pre_gain', 'm_mlp_post_gain', 'm_w_mlp_in', 'm_w_mlp_out', 'v_w_in', 'v_s5_lam_re', 'v_s5_lam_im', 'v_s5_log_step', 'v_s5_b_re', 'v_s5_b_im', 'v_s5_c_re', 'v_s5_c_im', 'v_s5_d', 'v_s5_w_glu', 'v_s5_b_glu', 'v_gdn_conv', 'v_gdn_a_log', 'v_gdn_dt_bias', 'v_gdn_o_gain', 'v_swa_sink', 'v_t5_bias', 'v_mla_q_gain', 'v_mla_kv_gain', 'v_mla_w_uq', 'v_mla_w_ukv', 'v_w_branch', 'v_w_out', 'v_mix_pre_gain', 'v_mix_post_gain', 'v_mlp_pre_gain', 'v_mlp_post_gain', 'v_w_mlp_in', 'v_w_mlp_out']
TWIN_OUTPUTS = ['loss', 'grad_x', 'grad_w_in', 'grad_s5_lam_re', 'grad_s5_lam_im', 'grad_s5_log_step', 'grad_s5_b_re', 'grad_s5_b_im', 'grad_s5_c_re', 'grad_s5_c_im', 'grad_s5_d', 'grad_s5_w_glu', 'grad_s5_b_glu', 'grad_gdn_conv', 'grad_gdn_a_log', 'grad_gdn_dt_bias', 'grad_gdn_o_gain', 'grad_swa_sink', 'grad_t5_bias', 'grad_mla_q_gain', 'grad_mla_kv_gain', 'grad_mla_w_uq', 'grad_mla_w_ukv', 'grad_w_branch', 'grad_w_out', 'grad_mix_pre_gain', 'grad_mix_post_gain', 'grad_mlp_pre_gain', 'grad_mlp_post_gain', 'grad_w_mlp_in', 'grad_w_mlp_out', 'delta_w_in', 'delta_s5_lam_re', 'delta_s5_lam_im', 'delta_s5_log_step', 'delta_s5_b_re', 'delta_s5_b_im', 'delta_s5_c_re', 'delta_s5_c_im', 'delta_s5_d', 'delta_s5_w_glu', 'delta_s5_b_glu', 'delta_gdn_conv', 'delta_gdn_a_log', 'delta_gdn_dt_bias', 'delta_gdn_o_gain', 'delta_swa_sink', 'delta_t5_bias', 'delta_mla_q_gain', 'delta_mla_kv_gain', 'delta_mla_w_uq', 'delta_mla_w_ukv', 'delta_w_branch', 'delta_w_out', 'delta_mix_pre_gain', 'delta_mix_post_gain', 'delta_mlp_pre_gain', 'delta_mlp_post_gain', 'delta_w_mlp_in', 'delta_w_mlp_out', 'new_m_w_in', 'new_m_s5_lam_re', 'new_m_s5_lam_im', 'new_m_s5_log_step', 'new_m_s5_b_re', 'new_m_s5_b_im', 'new_m_s5_c_re', 'new_m_s5_c_im', 'new_m_s5_d', 'new_m_s5_w_glu', 'new_m_s5_b_glu', 'new_m_gdn_conv', 'new_m_gdn_a_log', 'new_m_gdn_dt_bias', 'new_m_gdn_o_gain', 'new_m_swa_sink', 'new_m_t5_bias', 'new_m_mla_q_gain', 'new_m_mla_kv_gain', 'new_m_mla_w_uq', 'new_m_mla_w_ukv', 'new_m_w_branch', 'new_m_w_out', 'new_m_mix_pre_gain', 'new_m_mix_post_gain', 'new_m_mlp_pre_gain', 'new_m_mlp_post_gain', 'new_m_w_mlp_in', 'new_m_w_mlp_out', 'new_v_w_in', 'new_v_s5_lam_re', 'new_v_s5_lam_im', 'new_v_s5_log_step', 'new_v_s5_b_re', 'new_v_s5_b_im', 'new_v_s5_c_re', 'new_v_s5_c_im', 'new_v_s5_d', 'new_v_s5_w_glu', 'new_v_s5_b_glu', 'new_v_gdn_conv', 'new_v_gdn_a_log', 'new_v_gdn_dt_bias', 'new_v_gdn_o_gain', 'new_v_swa_sink', 'new_v_t5_bias', 'new_v_mla_q_gain', 'new_v_mla_kv_gain', 'new_v_mla_w_uq', 'new_v_mla_w_ukv', 'new_v_w_branch', 'new_v_w_out', 'new_v_mix_pre_gain', 'new_v_mix_post_gain', 'new_v_mlp_pre_gain', 'new_v_mlp_post_gain', 'new_v_w_mlp_in', 'new_v_w_mlp_out']
TWIN_LEAF_KINDS = {'loss': 'loss', 'grad_x': 'grad_x', 'grad_w_in': 'grad_w', 'grad_s5_lam_re': 'grad_w', 'grad_s5_lam_im': 'grad_w', 'grad_s5_log_step': 'grad_w', 'grad_s5_b_re': 'grad_w', 'grad_s5_b_im': 'grad_w', 'grad_s5_c_re': 'grad_w', 'grad_s5_c_im': 'grad_w', 'grad_s5_d': 'grad_w', 'grad_s5_w_glu': 'grad_w', 'grad_s5_b_glu': 'grad_w', 'grad_gdn_conv': 'grad_w', 'grad_gdn_a_log': 'grad_w', 'grad_gdn_dt_bias': 'grad_w', 'grad_gdn_o_gain': 'grad_w', 'grad_swa_sink': 'grad_w', 'grad_t5_bias': 'grad_w', 'grad_mla_q_gain': 'grad_w', 'grad_mla_kv_gain': 'grad_w', 'grad_mla_w_uq': 'grad_w', 'grad_mla_w_ukv': 'grad_w', 'grad_w_branch': 'grad_w', 'grad_w_out': 'grad_w', 'grad_mix_pre_gain': 'grad_w', 'grad_mix_post_gain': 'grad_w', 'grad_mlp_pre_gain': 'grad_w', 'grad_mlp_post_gain': 'grad_w', 'grad_w_mlp_in': 'grad_w', 'grad_w_mlp_out': 'grad_w', 'delta_w_in': 'delta_w', 'delta_s5_lam_re': 'delta_w', 'delta_s5_lam_im': 'delta_w', 'delta_s5_log_step': 'delta_w', 'delta_s5_b_re': 'delta_w', 'delta_s5_b_im': 'delta_w', 'delta_s5_c_re': 'delta_w', 'delta_s5_c_im': 'delta_w', 'delta_s5_d': 'delta_w', 'delta_s5_w_glu': 'delta_w', 'delta_s5_b_glu': 'delta_w', 'delta_gdn_conv': 'delta_w', 'delta_gdn_a_log': 'delta_w', 'delta_gdn_dt_bias': 'delta_w', 'delta_gdn_o_gain': 'delta_w', 'delta_swa_sink': 'delta_w', 'delta_t5_bias': 'delta_w', 'delta_mla_q_gain': 'delta_w', 'delta_mla_kv_gain': 'delta_w', 'delta_mla_w_uq': 'delta_w', 'delta_mla_w_ukv': 'delta_w', 'delta_w_branch': 'delta_w', 'delta_w_out': 'delta_w', 'delta_mix_pre_gain': 'delta_w', 'delta_mix_post_gain': 'delta_w', 'delta_mlp_pre_gain': 'delta_w', 'delta_mlp_post_gain': 'delta_w', 'delta_w_mlp_in': 'delta_w', 'delta_w_mlp_out': 'delta_w', 'new_m_w_in': 'new_m', 'new_m_s5_lam_re': 'new_m', 'new_m_s5_lam_im': 'new_m', 'new_m_s5_log_step': 'new_m', 'new_m_s5_b_re': 'new_m', 'new_m_s5_b_im': 'new_m', 'new_m_s5_c_re': 'new_m', 'new_m_s5_c_im': 'new_m', 'new_m_s5_d': 'new_m', 'new_m_s5_w_glu': 'new_m', 'new_m_s5_b_glu': 'new_m', 'new_m_gdn_conv': 'new_m', 'new_m_gdn_a_log': 'new_m', 'new_m_gdn_dt_bias': 'new_m', 'new_m_gdn_o_gain': 'new_m', 'new_m_swa_sink': 'new_m', 'new_m_t5_bias': 'new_m', 'new_m_mla_q_gain': 'new_m', 'new_m_mla_kv_gain': 'new_m', 'new_m_mla_w_uq': 'new_m', 'new_m_mla_w_ukv': 'new_m', 'new_m_w_branch': 'new_m', 'new_m_w_out': 'new_m', 'new_m_mix_pre_gain': 'new_m', 'new_m_mix_post_gain': 'new_m', 'new_m_mlp_pre_gain': 'new_m', 'new_m_mlp_post_gain': 'new_m', 'new_m_w_mlp_in': 'new_m', 'new_m_w_mlp_out': 'new_m', 'new_v_w_in': 'new_v', 'new_v_s5_lam_re': 'new_v', 'new_v_s5_lam_im': 'new_v', 'new_v_s5_log_step': 'new_v', 'new_v_s5_b_re': 'new_v', 'new_v_s5_b_im': 'new_v', 'new_v_s5_c_re': 'new_v', 'new_v_s5_c_im': 'new_v', 'new_v_s5_d': 'new_v', 'new_v_s5_w_glu': 'new_v', 'new_v_s5_b_glu': 'new_v', 'new_v_gdn_conv': 'new_v', 'new_v_gdn_a_log': 'new_v', 'new_v_gdn_dt_bias': 'new_v', 'new_v_gdn_o_gain': 'new_v', 'new_v_swa_sink': 'new_v', 'new_v_t5_bias': 'new_v', 'new_v_mla_q_gain': 'new_v', 'new_v_mla_kv_gain': 'new_v', 'new_v_mla_w_uq': 'new_v', 'new_v_mla_w_ukv': 'new_v', 'new_v_w_branch': 'new_v', 'new_v_w_out': 'new_v', 'new_v_mix_pre_gain': 'new_v', 'new_v_mix_post_gain': 'new_v', 'new_v_mlp_pre_gain': 'new_v', 'new_v_mlp_post_gain': 'new_v', 'new_v_w_mlp_in': 'new_v', 'new_v_w_mlp_out': 'new_v'}


def _forward(args):
    return _fwd_reference(*[args[k] for k in FWD_PARAMS])


def _output_shape():
    def fwd():
        inp = _fwd_setup_inputs(0)
        return _fwd_reference(*[inp[k] for k in FWD_PARAMS])
    out = _jax.eval_shape(fwd)
    return out.shape, out.dtype

N_MICROBATCH = 1
ADAM_LR = 0.001
ADAM_B1 = 0.9
ADAM_B2 = 0.999
ADAM_EPS = 1e-08
ADAM_WD = 0.01
ADAM_STEP = 10
PER_EXAMPLE_BATCH_AXIS = {'x': 0, 'loss_target': 0}
SHARED_INPUTS = []
_WEIGHT_DTYPES = {'w_in': _jnp.float32, 's5_lam_re': _jnp.float32, 's5_lam_im': _jnp.float32, 's5_log_step': _jnp.float32, 's5_b_re': _jnp.float32, 's5_b_im': _jnp.float32, 's5_c_re': _jnp.float32, 's5_c_im': _jnp.float32, 's5_d': _jnp.float32, 's5_w_glu': _jnp.float32, 's5_b_glu': _jnp.float32, 'gdn_conv': _jnp.float32, 'gdn_a_log': _jnp.float32, 'gdn_dt_bias': _jnp.float32, 'gdn_o_gain': _jnp.float32, 'swa_sink': _jnp.float32, 't5_bias': _jnp.float32, 'mla_q_gain': _jnp.float32, 'mla_kv_gain': _jnp.float32, 'mla_w_uq': _jnp.float32, 'mla_w_ukv': _jnp.float32, 'w_branch': _jnp.float32, 'w_out': _jnp.float32, 'mix_pre_gain': _jnp.float32, 'mix_post_gain': _jnp.float32, 'mlp_pre_gain': _jnp.float32, 'mlp_post_gain': _jnp.float32, 'w_mlp_in': _jnp.float32, 'w_mlp_out': _jnp.float32}
MOMENT_SCALE = {'w_in': 1.262378e+01, 's5_lam_re': 1.015139e+00, 's5_lam_im': 1.226481e+00, 's5_log_step': 4.110924e+01, 's5_b_re': 7.710626e-01, 's5_b_im': 7.794053e-01, 's5_c_re': 1.260737e+00, 's5_c_im': 1.075305e+00, 's5_d': 3.468461e+01, 's5_w_glu': 6.077224e+00, 's5_b_glu': 1.589800e+01, 'gdn_conv': 1.489704e+01, 'gdn_a_log': 1.194913e+01, 'gdn_dt_bias': 1.175958e+01, 'gdn_o_gain': 5.839263e+01, 'swa_sink': 8.762253e-01, 't5_bias': 9.258547e-01, 'mla_q_gain': 1.325535e+00, 'mla_kv_gain': 3.891648e+01, 'mla_w_uq': 9.413941e-01, 'mla_w_ukv': 2.809642e+01, 'w_branch': 1.834523e+01, 'w_out': 3.618143e+01, 'mix_pre_gain': 3.086895e+01, 'mix_post_gain': 5.167063e+01, 'mlp_pre_gain': 1.872784e+01, 'mlp_post_gain': 4.812157e+01, 'w_mlp_in': 9.225600e+00, 'w_mlp_out': 3.419514e+01}


def _to_microbatches(a, axis):
    t = _jnp.moveaxis(a, axis, 0)
    t = t.reshape((N_MICROBATCH, t.shape[0] // N_MICROBATCH) + t.shape[1:])
    return _jnp.moveaxis(t, 1, axis + 1)


def setup_inputs(seed: int = 0) -> dict:
    inp = _fwd_setup_inputs(seed)
    key = _jax.random.fold_in(_jax.random.key(seed), 7919)
    shape, _ = _output_shape()
    out = dict(inp)
    out["loss_target"] = _jax.random.normal(_jax.random.fold_in(key, 0), shape, _jnp.float32)
    for i, name in enumerate(TWIN_WEIGHTS):
        w = inp[name].astype(_jnp.float32)
        if MOMENT_SCALE is None:
            s = _jnp.sqrt(_jnp.mean(_jnp.square(w)) + 1e-30)
        else:
            s = MOMENT_SCALE[name]
        km, kv = _jax.random.split(_jax.random.fold_in(key, i + 1))
        out[name] = w
        out["m_" + name] = s * _jax.random.normal(km, w.shape, _jnp.float32)
        out["v_" + name] = (s * s) * _jax.random.uniform(kv, w.shape, _jnp.float32, 0.5, 1.5)
    if N_MICROBATCH > 1:
        for name, axis in PER_EXAMPLE_BATCH_AXIS.items():
            out[name] = _to_microbatches(out[name], axis)
    return {'x': out['x'], 'w_in': out['w_in'], 's5_lam_re': out['s5_lam_re'], 's5_lam_im': out['s5_lam_im'], 's5_log_step': out['s5_log_step'], 's5_b_re': out['s5_b_re'], 's5_b_im': out['s5_b_im'], 's5_c_re': out['s5_c_re'], 's5_c_im': out['s5_c_im'], 's5_d': out['s5_d'], 's5_w_glu': out['s5_w_glu'], 's5_b_glu': out['s5_b_glu'], 'gdn_conv': out['gdn_conv'], 'gdn_a_log': out['gdn_a_log'], 'gdn_dt_bias': out['gdn_dt_bias'], 'gdn_o_gain': out['gdn_o_gain'], 'swa_sink': out['swa_sink'], 't5_bias': out['t5_bias'], 'mla_q_gain': out['mla_q_gain'], 'mla_kv_gain': out['mla_kv_gain'], 'mla_w_uq': out['mla_w_uq'], 'mla_w_ukv': out['mla_w_ukv'], 'w_branch': out['w_branch'], 'w_out': out['w_out'], 'mix_pre_gain': out['mix_pre_gain'], 'mix_post_gain': out['mix_post_gain'], 'mlp_pre_gain': out['mlp_pre_gain'], 'mlp_post_gain': out['mlp_post_gain'], 'w_mlp_in': out['w_mlp_in'], 'w_mlp_out': out['w_mlp_out'], 'loss_target': out['loss_target'], 'm_w_in': out['m_w_in'], 'm_s5_lam_re': out['m_s5_lam_re'], 'm_s5_lam_im': out['m_s5_lam_im'], 'm_s5_log_step': out['m_s5_log_step'], 'm_s5_b_re': out['m_s5_b_re'], 'm_s5_b_im': out['m_s5_b_im'], 'm_s5_c_re': out['m_s5_c_re'], 'm_s5_c_im': out['m_s5_c_im'], 'm_s5_d': out['m_s5_d'], 'm_s5_w_glu': out['m_s5_w_glu'], 'm_s5_b_glu': out['m_s5_b_glu'], 'm_gdn_conv': out['m_gdn_conv'], 'm_gdn_a_log': out['m_gdn_a_log'], 'm_gdn_dt_bias': out['m_gdn_dt_bias'], 'm_gdn_o_gain': out['m_gdn_o_gain'], 'm_swa_sink': out['m_swa_sink'], 'm_t5_bias': out['m_t5_bias'], 'm_mla_q_gain': out['m_mla_q_gain'], 'm_mla_kv_gain': out['m_mla_kv_gain'], 'm_mla_w_uq': out['m_mla_w_uq'], 'm_mla_w_ukv': out['m_mla_w_ukv'], 'm_w_branch': out['m_w_branch'], 'm_w_out': out['m_w_out'], 'm_mix_pre_gain': out['m_mix_pre_gain'], 'm_mix_post_gain': out['m_mix_post_gain'], 'm_mlp_pre_gain': out['m_mlp_pre_gain'], 'm_mlp_post_gain': out['m_mlp_post_gain'], 'm_w_mlp_in': out['m_w_mlp_in'], 'm_w_mlp_out': out['m_w_mlp_out'], 'v_w_in': out['v_w_in'], 'v_s5_lam_re': out['v_s5_lam_re'], 'v_s5_lam_im': out['v_s5_lam_im'], 'v_s5_log_step': out['v_s5_log_step'], 'v_s5_b_re': out['v_s5_b_re'], 'v_s5_b_im': out['v_s5_b_im'], 'v_s5_c_re': out['v_s5_c_re'], 'v_s5_c_im': out['v_s5_c_im'], 'v_s5_d': out['v_s5_d'], 'v_s5_w_glu': out['v_s5_w_glu'], 'v_s5_b_glu': out['v_s5_b_glu'], 'v_gdn_conv': out['v_gdn_conv'], 'v_gdn_a_log': out['v_gdn_a_log'], 'v_gdn_dt_bias': out['v_gdn_dt_bias'], 'v_gdn_o_gain': out['v_gdn_o_gain'], 'v_swa_sink': out['v_swa_sink'], 'v_t5_bias': out['v_t5_bias'], 'v_mla_q_gain': out['v_mla_q_gain'], 'v_mla_kv_gain': out['v_mla_kv_gain'], 'v_mla_w_uq': out['v_mla_w_uq'], 'v_mla_w_ukv': out['v_mla_w_ukv'], 'v_w_branch': out['v_w_branch'], 'v_w_out': out['v_w_out'], 'v_mix_pre_gain': out['v_mix_pre_gain'], 'v_mix_post_gain': out['v_mix_post_gain'], 'v_mlp_pre_gain': out['v_mlp_pre_gain'], 'v_mlp_post_gain': out['v_mlp_post_gain'], 'v_w_mlp_in': out['v_w_mlp_in'], 'v_w_mlp_out': out['v_w_mlp_out']}


def _loss(weights, diff, rest, loss_target):
    with _jax.named_scope("forward"):
        args = {**rest, TWIN_DIFF_INPUT: diff, **{k: w.astype(_WEIGHT_DTYPES[k]) for k, w in weights.items()}}
        y = _forward(args)
    with _jax.named_scope("loss_head"):
        err = _jnp.square(y.astype(_jnp.float32) - loss_target)
        return 0.5 * _jnp.sum(_jnp.mean(err, axis=-1)) if err.ndim else 0.5 * err


def _adamw(w, g, m, v):
    m = ADAM_B1 * m + (1.0 - ADAM_B1) * g
    v = ADAM_B2 * v + (1.0 - ADAM_B2) * _jnp.square(g)
    m_hat = m / (1.0 - ADAM_B1 ** ADAM_STEP)
    v_hat = v / (1.0 - ADAM_B2 ** ADAM_STEP)
    delta = -ADAM_LR * (m_hat / (_jnp.sqrt(v_hat) + ADAM_EPS) + ADAM_WD * w)
    return delta, m, v


def reference(x, w_in, s5_lam_re, s5_lam_im, s5_log_step, s5_b_re, s5_b_im, s5_c_re, s5_c_im, s5_d, s5_w_glu, s5_b_glu, gdn_conv, gdn_a_log, gdn_dt_bias, gdn_o_gain, swa_sink, t5_bias, mla_q_gain, mla_kv_gain, mla_w_uq, mla_w_ukv, w_branch, w_out, mix_pre_gain, mix_post_gain, mlp_pre_gain, mlp_post_gain, w_mlp_in, w_mlp_out, loss_target, m_w_in, m_s5_lam_re, m_s5_lam_im, m_s5_log_step, m_s5_b_re, m_s5_b_im, m_s5_c_re, m_s5_c_im, m_s5_d, m_s5_w_glu, m_s5_b_glu, m_gdn_conv, m_gdn_a_log, m_gdn_dt_bias, m_gdn_o_gain, m_swa_sink, m_t5_bias, m_mla_q_gain, m_mla_kv_gain, m_mla_w_uq, m_mla_w_ukv, m_w_branch, m_w_out, m_mix_pre_gain, m_mix_post_gain, m_mlp_pre_gain, m_mlp_post_gain, m_w_mlp_in, m_w_mlp_out, v_w_in, v_s5_lam_re, v_s5_lam_im, v_s5_log_step, v_s5_b_re, v_s5_b_im, v_s5_c_re, v_s5_c_im, v_s5_d, v_s5_w_glu, v_s5_b_glu, v_gdn_conv, v_gdn_a_log, v_gdn_dt_bias, v_gdn_o_gain, v_swa_sink, v_t5_bias, v_mla_q_gain, v_mla_kv_gain, v_mla_w_uq, v_mla_w_ukv, v_w_branch, v_w_out, v_mix_pre_gain, v_mix_post_gain, v_mlp_pre_gain, v_mlp_post_gain, v_w_mlp_in, v_w_mlp_out):
    given = dict(x=x, w_in=w_in, s5_lam_re=s5_lam_re, s5_lam_im=s5_lam_im, s5_log_step=s5_log_step, s5_b_re=s5_b_re, s5_b_im=s5_b_im, s5_c_re=s5_c_re, s5_c_im=s5_c_im, s5_d=s5_d, s5_w_glu=s5_w_glu, s5_b_glu=s5_b_glu, gdn_conv=gdn_conv, gdn_a_log=gdn_a_log, gdn_dt_bias=gdn_dt_bias, gdn_o_gain=gdn_o_gain, swa_sink=swa_sink, t5_bias=t5_bias, mla_q_gain=mla_q_gain, mla_kv_gain=mla_kv_gain, mla_w_uq=mla_w_uq, mla_w_ukv=mla_w_ukv, w_branch=w_branch, w_out=w_out, mix_pre_gain=mix_pre_gain, mix_post_gain=mix_post_gain, mlp_pre_gain=mlp_pre_gain, mlp_post_gain=mlp_post_gain, w_mlp_in=w_mlp_in, w_mlp_out=w_mlp_out, loss_target=loss_target, m_w_in=m_w_in, m_s5_lam_re=m_s5_lam_re, m_s5_lam_im=m_s5_lam_im, m_s5_log_step=m_s5_log_step, m_s5_b_re=m_s5_b_re, m_s5_b_im=m_s5_b_im, m_s5_c_re=m_s5_c_re, m_s5_c_im=m_s5_c_im, m_s5_d=m_s5_d, m_s5_w_glu=m_s5_w_glu, m_s5_b_glu=m_s5_b_glu, m_gdn_conv=m_gdn_conv, m_gdn_a_log=m_gdn_a_log, m_gdn_dt_bias=m_gdn_dt_bias, m_gdn_o_gain=m_gdn_o_gain, m_swa_sink=m_swa_sink, m_t5_bias=m_t5_bias, m_mla_q_gain=m_mla_q_gain, m_mla_kv_gain=m_mla_kv_gain, m_mla_w_uq=m_mla_w_uq, m_mla_w_ukv=m_mla_w_ukv, m_w_branch=m_w_branch, m_w_out=m_w_out, m_mix_pre_gain=m_mix_pre_gain, m_mix_post_gain=m_mix_post_gain, m_mlp_pre_gain=m_mlp_pre_gain, m_mlp_post_gain=m_mlp_post_gain, m_w_mlp_in=m_w_mlp_in, m_w_mlp_out=m_w_mlp_out, v_w_in=v_w_in, v_s5_lam_re=v_s5_lam_re, v_s5_lam_im=v_s5_lam_im, v_s5_log_step=v_s5_log_step, v_s5_b_re=v_s5_b_re, v_s5_b_im=v_s5_b_im, v_s5_c_re=v_s5_c_re, v_s5_c_im=v_s5_c_im, v_s5_d=v_s5_d, v_s5_w_glu=v_s5_w_glu, v_s5_b_glu=v_s5_b_glu, v_gdn_conv=v_gdn_conv, v_gdn_a_log=v_gdn_a_log, v_gdn_dt_bias=v_gdn_dt_bias, v_gdn_o_gain=v_gdn_o_gain, v_swa_sink=v_swa_sink, v_t5_bias=v_t5_bias, v_mla_q_gain=v_mla_q_gain, v_mla_kv_gain=v_mla_kv_gain, v_mla_w_uq=v_mla_w_uq, v_mla_w_ukv=v_mla_w_ukv, v_w_branch=v_w_branch, v_w_out=v_w_out, v_mix_pre_gain=v_mix_pre_gain, v_mix_post_gain=v_mix_post_gain, v_mlp_pre_gain=v_mlp_pre_gain, v_mlp_post_gain=v_mlp_post_gain, v_w_mlp_in=v_w_mlp_in, v_w_mlp_out=v_w_mlp_out)
    weights = {n: given[n] for n in TWIN_WEIGHTS}
    shared = {n: given[n] for n in SHARED_INPUTS}
    per_example = {n: given[n] for n in ['x']}
    grad_fn = _jax.value_and_grad(_loss, argnums=(0, 1))

    def one_microbatch(ex, loss_target):
        ex = dict(ex)
        diff = ex.pop(TWIN_DIFF_INPUT)
        return grad_fn(weights, diff, {**shared, **ex}, loss_target)

    if N_MICROBATCH == 1:
        loss, (grad_w, grad_x) = one_microbatch(per_example, given["loss_target"])
    else:
        def body(carry, xs):
            loss_sum, grad_sum = carry
            l_k, (gw_k, gx_k) = one_microbatch(xs[0], xs[1])
            with _jax.named_scope("update"):
                return (loss_sum + l_k, _jax.tree.map(_jnp.add, grad_sum, gw_k)), gx_k

        init = (_jnp.zeros((), _jnp.float32), _jax.tree.map(_jnp.zeros_like, weights))
        (loss, grad_w), grad_x = _jax.lax.scan(body, init, (per_example, given["loss_target"]))
    with _jax.named_scope("update"):
        delta_w, new_m, new_v = {}, {}, {}
        for n in TWIN_WEIGHTS:
            delta_w[n], new_m[n], new_v[n] = _adamw(weights[n], grad_w[n], given["m_" + n], given["v_" + n])
    return (loss, grad_x, *[grad_w[n] for n in TWIN_WEIGHTS], *[delta_w[n] for n in TWIN_WEIGHTS],
            *[new_m[n] for n in TWIN_WEIGHTS], *[new_v[n] for n in TWIN_WEIGHTS])
```

```python
import functools
import math

import numpy as np
import jax
import jax.numpy as jnp
from jax import lax
from jax.experimental import pallas as pl
from jax.experimental.pallas import tpu as pltpu

F32 = jnp.float32
BF16 = jnp.bfloat16

VMEM_LIMIT_BYTES = 56 * 1024 * 1024
SUBLANES = 8
LANES = 128

NORM_EPS = 1e-6
DEPTH = 4
N_BRANCHES = 4
S5_GROUP = 16
S5_STATE = 64
GDN_HEAD_DIM = 128
GDN_CONV = 4
GDN_CHUNK = 64
SWA_HEAD_DIM = 64
SWA_KV_GROUP = 4
WINDOW = 128
SWA_BLOCK = 128
T5_BUCKETS = 32
T5_MAX_DISTANCE = 128
MLA_NOPE = 128
MLA_ROPE = 64
MLA_V = 128
ROPE_THETA = 10000.0

ADAM_LR = 0.001
ADAM_B1 = 0.9
ADAM_B2 = 0.999
ADAM_EPS = 1e-08
ADAM_WD = 0.01
ADAM_STEP = 10

_HIGHEST = lax.Precision.HIGHEST


def _params(sem, vmem=VMEM_LIMIT_BYTES):
    return pltpu.CompilerParams(dimension_semantics=sem, vmem_limit_bytes=vmem)


def _tile(dim, target, align):
    if dim <= target:
        return dim
    t = (target // align) * align
    while t >= align:
        if dim % t == 0:
            return t
        t -= align
    return dim


_DOT_DIMS = {"nn": ((1,), (0,)), "nt": ((1,), (1,)), "tn": ((0,), (0,))}


def _mm(a, b, form="nn", out_dtype=F32, tm=1024, tn=1024, tk=512, name="mm"):
    if form == "nn":
        (M, K), (K2, N) = a.shape, b.shape
    elif form == "nt":
        (M, K), (N, K2) = a.shape, b.shape
    else:
        (K, M), (K2, N) = a.shape, b.shape
    assert K == K2, (a.shape, b.shape, form)
    tm = _tile(M, tm, LANES if form == "tn" else 16)
    tn = _tile(N, tn, LANES if form != "nt" else 16)
    tk = _tile(K, tk, LANES)
    if form == "nt":
        tn = _tile(N, tn, LANES)
    nk = K // tk
    dims = (_DOT_DIMS[form], ((), ()))

    def body(a_ref, b_ref, o_ref, acc_ref):
        k = pl.program_id(2)

        @pl.when(k == 0)
        def _():
            acc_ref[...] = jnp.zeros_like(acc_ref)

        acc_ref[...] += lax.dot_general(a_ref[...].astype(BF16), b_ref[...].astype(BF16), dims,
                                        preferred_element_type=F32)

        @pl.when(k == nk - 1)
        def _():
            o_ref[...] = acc_ref[...].astype(o_ref.dtype)

    if form == "nn":
        a_spec = pl.BlockSpec((tm, tk), lambda i, j, k: (i, k))
        b_spec = pl.BlockSpec((tk, tn), lambda i, j, k: (k, j))
    elif form == "nt":
        a_spec = pl.BlockSpec((tm, tk), lambda i, j, k: (i, k))
        b_spec = pl.BlockSpec((tn, tk), lambda i, j, k: (j, k))
    else:
        a_spec = pl.BlockSpec((tk, tm), lambda i, j, k: (k, i))
        b_spec = pl.BlockSpec((tk, tn), lambda i, j, k: (k, j))
    return pl.pallas_call(
        body,
        out_shape=jax.ShapeDtypeStruct((M, N), out_dtype),
        grid=(M // tm, N // tn, nk),
        in_specs=[a_spec, b_spec],
        out_specs=pl.BlockSpec((tm, tn), lambda i, j, k: (i, j)),
        scratch_shapes=[pltpu.VMEM((tm, tn), F32)],
        compiler_params=_params(("parallel", "parallel", "arbitrary")),
        name=name,
    )(a, b)


@functools.partial(jax.custom_vjp, nondiff_argnums=(2,))
def linear(x, w, name):
    return _mm(x, w, "nn", name=name + "_fwd")


def _linear_fwd(x, w, name):
    return _mm(x, w, "nn", name=name + "_fwd"), (x, w)


def _linear_bwd(name, res, dy):
    x, w = res
    dx = _mm(dy, w, "nt", out_dtype=x.dtype, name=name + "_dx")
    dw = _mm(x, dy, "tn", out_dtype=w.dtype, name=name + "_dw")
    return dx, dw


linear.defvjp(_linear_fwd, _linear_bwd)


def _rms_rows(L, D):
    return _tile(L, max(SUBLANES, (1 << 20) // D), SUBLANES)


def _rms_fwd_call(x, gain, name):
    L, D = x.shape
    tm = _rms_rows(L, D)

    def body(x_ref, g_ref, o_ref):
        xv = x_ref[...]
        r = lax.rsqrt(jnp.mean(xv * xv, axis=-1, keepdims=True) + NORM_EPS)
        o_ref[...] = xv * r * g_ref[...]

    return pl.pallas_call(
        body,
        out_shape=jax.ShapeDtypeStruct((L, D), F32),
        grid=(L // tm,),
        in_specs=[pl.BlockSpec((tm, D), lambda i: (i, 0)), pl.BlockSpec((1, D), lambda i: (0, 0))],
        out_specs=pl.BlockSpec((tm, D), lambda i: (i, 0)),
        compiler_params=_params(("parallel",)),
        name=name,
    )(x, gain.reshape(1, D))


def _rms_bwd_call(x, gain, dy, name):
    L, D = x.shape
    tm = _rms_rows(L, D)

    def body(x_ref, g_ref, dy_ref, dx_ref, dg_ref):
        xv = x_ref[...]
        dyv = dy_ref[...]
        r = lax.rsqrt(jnp.mean(xv * xv, axis=-1, keepdims=True) + NORM_EPS)
        xh = xv * r
        dyg = dyv * g_ref[...]
        dx_ref[...] = r * (dyg - xh * jnp.mean(dyg * xh, axis=-1, keepdims=True))

        @pl.when(pl.program_id(0) == 0)
        def _():
            dg_ref[...] = jnp.zeros_like(dg_ref)

        dg_ref[...] += jnp.sum(dyv * xh, axis=0, keepdims=True)

    return pl.pallas_call(
        body,
        out_shape=(jax.ShapeDtypeStruct((L, D), F32), jax.ShapeDtypeStruct((1, D), F32)),
        grid=(L // tm,),
        in_specs=[pl.BlockSpec((tm, D), lambda i: (i, 0)), pl.BlockSpec((1, D), lambda i: (0, 0)),
                  pl.BlockSpec((tm, D), lambda i: (i, 0))],
        out_specs=(pl.BlockSpec((tm, D), lambda i: (i, 0)), pl.BlockSpec((1, D), lambda i: (0, 0))),
        compiler_params=_params(("arbitrary",)),
        name=name,
    )(x, gain.reshape(1, D), dy)


@functools.partial(jax.custom_vjp, nondiff_argnums=(2,))
def rms_norm(x, gain, name):
    return _rms_fwd_call(x, gain, name + "_fwd")


def _rms_norm_fwd(x, gain, name):
    return _rms_fwd_call(x, gain, name + "_fwd"), (x, gain)


def _rms_norm_bwd(name, res, dy):
    x, gain = res
    dx, dg = _rms_bwd_call(x, gain, dy, name + "_bwd")
    return dx, dg.reshape(gain.shape)


rms_norm.defvjp(_rms_norm_fwd, _rms_norm_bwd)


SCAN_COLS = 512
SCAN_ROWS = 256
SCAN_UNROLL = 8


def _to_scan_cols(t, nd):
    P = t.shape[-1]
    pc = _tile(P, SCAN_COLS, LANES)
    lead = t.shape[:-3]
    t = t.reshape(lead + (nd, 2, P // pc, pc))
    t = jnp.swapaxes(t, -3, -2)
    return t.reshape(lead + (nd * 2 * P,))


def _scan_call(b, a, nd, reverse_dir0, name):
    L, NC = b.shape
    P = NC // (2 * nd)
    pc = _tile(P, SCAN_COLS, LANES)
    ncb = P // pc
    T = _tile(L, SCAN_ROWS, SUBLANES)
    nt = L // T
    rev0 = 1 if reverse_dir0 else 0

    def rev_of(d):
        return d + rev0 - 2 * d * rev0

    def body(b_ref, a_ref, o_ref, st_ref):
        rv = rev_of(pl.program_id(0))

        @pl.when(pl.program_id(2) == 0)
        def _():
            st_ref[...] = jnp.zeros_like(st_ref)

        ar = a_ref[:, :pc]
        ai = a_ref[:, pc:]

        def step(j, carry):
            sr, si = carry
            r = j + rv * (T - 1 - 2 * j)
            row = b_ref[pl.ds(r, 1), :]
            nr = ar * sr - ai * si + row[:, :pc]
            ni = ar * si + ai * sr + row[:, pc:]
            o_ref[pl.ds(r, 1), :] = jnp.concatenate([nr, ni], axis=1)
            return nr, ni

        sr, si = lax.fori_loop(0, T, step, (st_ref[:, :pc], st_ref[:, pc:]), unroll=SCAN_UNROLL)
        st_ref[...] = jnp.concatenate([sr, si], axis=1)

    def t_idx(d, i):
        return i + rev_of(d) * (nt - 1 - 2 * i)

    spec = pl.BlockSpec((T, 2 * pc), lambda d, c, i: (t_idx(d, i), d * ncb + c))
    return pl.pallas_call(
        body,
        out_shape=jax.ShapeDtypeStruct(b.shape, F32),
        grid=(nd, ncb, nt),
        in_specs=[spec, pl.BlockSpec((1, 2 * pc), lambda d, c, i: (0, d * ncb + c))],
        out_specs=spec,
        scratch_shapes=[pltpu.VMEM((1, 2 * pc), F32)],
        compiler_params=_params(("arbitrary", "arbitrary", "arbitrary")),
        name=name,
    )(b, a)


def _scan_da_call(g, sp, nd, name):
    L, NC = g.shape
    P = NC // (2 * nd)
    pc = _tile(P, SCAN_COLS, LANES)
    T = _tile(L, SCAN_ROWS * 2, SUBLANES)

    def body(g_ref, s_ref, o_ref):
        @pl.when(pl.program_id(1) == 0)
        def _():
            o_ref[...] = jnp.zeros_like(o_ref)

        gr, gi = g_ref[:, :pc], g_ref[:, pc:]
        sr, si = s_ref[:, :pc], s_ref[:, pc:]
        dar = jnp.sum(gr * sr + gi * si, axis=0, keepdims=True)
        dai = jnp.sum(gi * sr - gr * si, axis=0, keepdims=True)
        o_ref[...] += jnp.concatenate([dar, dai], axis=1)

    spec = pl.BlockSpec((T, 2 * pc), lambda c, i: (i, c))
    return pl.pallas_call(
        body,
        out_shape=jax.ShapeDtypeStruct((1, NC), F32),
        grid=(NC // (2 * pc), L // T),
        in_specs=[spec, spec],
        out_specs=pl.BlockSpec((1, 2 * pc), lambda c, i: (0, c)),
        compiler_params=_params(("arbitrary", "arbitrary")),
        name=name,
    )(g, sp)


def _conj_cols(a, nd):
    P = a.shape[-1] // (2 * nd)
    pc = _tile(P, SCAN_COLS, LANES)
    sign = jnp.tile(jnp.concatenate([jnp.ones((pc,), F32), -jnp.ones((pc,), F32)]), a.shape[-1] // (2 * pc))
    return a * sign


@functools.partial(jax.custom_vjp, nondiff_argnums=(2,))
def s5_scan(b, a, nd):
    return _scan_call(b, a, nd, False, "s5_scan_fwd")


def _s5_scan_fwd(b, a, nd):
    s = _scan_call(b, a, nd, False, "s5_scan_fwd")
    return s, (s, a)


def _s5_scan_bwd(nd, res, ds):
    s, a = res
    g = _scan_call(ds, _conj_cols(a, nd), nd, True, "s5_scan_adj")
    L, NC = s.shape
    half = NC // nd
    zero = jnp.zeros((1, half), F32)
    prev = [jnp.concatenate([zero, s[:-1, :half]], axis=0)]
    if nd == 2:
        prev.append(jnp.concatenate([s[1:, half:], zero], axis=0))
    sp = jnp.concatenate(prev, axis=1)
    da = _scan_da_call(g, sp, nd, "s5_scan_da")
    return g, da


s5_scan.defvjp(_s5_scan_fwd, _s5_scan_bwd)


ATTN_TQ = 1024
ATTN_TK = 1024
ATTN_BWD_TQ = 512
ATTN_BWD_TK = 512


def _attn_fwd_call(q, k, v, scale):
    H, L, DQ = q.shape
    DV = v.shape[-1]
    tq = _tile(L, ATTN_TQ, LANES)
    tk = _tile(L, ATTN_TK, LANES)
    nk = L // tk

    def body(q_ref, k_ref, v_ref, o_ref, lse_ref, m_s, l_s, acc_s):
        j = pl.program_id(2)

        @pl.when(j == 0)
        def _():
            m_s[...] = jnp.full_like(m_s, -jnp.inf)
            l_s[...] = jnp.zeros_like(l_s)
            acc_s[...] = jnp.zeros_like(acc_s)

        s = lax.dot_general(q_ref[0].astype(BF16), k_ref[0].astype(BF16), (((1,), (1,)), ((), ())),
                            preferred_element_type=F32) * scale
        m_old = m_s[...]
        m_new = jnp.maximum(m_old, jnp.max(s, axis=1, keepdims=True))
        alpha = jnp.exp(m_old - m_new)
        p = jnp.exp(s - m_new)
        l_s[...] = alpha * l_s[...] + jnp.sum(p, axis=1, keepdims=True)
        acc_s[...] = alpha * acc_s[...] + jnp.dot(p.astype(BF16), v_ref[0].astype(BF16), preferred_element_type=F32)
        m_s[...] = m_new

        @pl.when(j == nk - 1)
        def _():
            o_ref[0] = acc_s[...] / l_s[...]
            lse_ref[0] = m_s[...] + jnp.log(l_s[...])

    return pl.pallas_call(
        body,
        out_shape=(jax.ShapeDtypeStruct((H, L, DV), F32), jax.ShapeDtypeStruct((H, L, 1), F32)),
        grid=(H, L // tq, nk),
        in_specs=[pl.BlockSpec((1, tq, DQ), lambda h, i, j: (h, i, 0)),
                  pl.BlockSpec((1, tk, DQ), lambda h, i, j: (h, j, 0)),
                  pl.BlockSpec((1, tk, DV), lambda h, i, j: (h, j, 0))],
        out_specs=(pl.BlockSpec((1, tq, DV), lambda h, i, j: (h, i, 0)),
                   pl.BlockSpec((1, tq, 1), lambda h, i, j: (h, i, 0))),
        scratch_shapes=[pltpu.VMEM((tq, 1), F32), pltpu.VMEM((tq, 1), F32), pltpu.VMEM((tq, DV), F32)],
        compiler_params=_params(("parallel", "parallel", "arbitrary")),
        name="mla_attn_fwd",
    )(q, k, v)


def _attn_bwd_call(q, k, v, do, lse, delta, scale):
    H, L, DQ = q.shape
    DV = v.shape[-1]
    tq = _tile(L, ATTN_BWD_TQ, LANES)
    tk = _tile(L, ATTN_BWD_TK, LANES)
    nq = L // tq

    def body(q_ref, k_ref, v_ref, do_ref, lse_ref, dl_ref, dq_ref, dk_ref, dv_ref, dk_s, dv_s):
        j = pl.program_id(1)
        i = pl.program_id(2)

        @pl.when(jnp.logical_and(j == 0, i == 0))
        def _():
            dq_ref[...] = jnp.zeros_like(dq_ref)

        @pl.when(i == 0)
        def _():
            dk_s[...] = jnp.zeros_like(dk_s)
            dv_s[...] = jnp.zeros_like(dv_s)

        qb = q_ref[0].astype(BF16)
        kb = k_ref[0].astype(BF16)
        dob = do_ref[0].astype(BF16)
        s = lax.dot_general(qb, kb, (((1,), (1,)), ((), ())), preferred_element_type=F32) * scale
        p = jnp.exp(s - lse_ref[0])
        dv_s[...] += lax.dot_general(p.astype(BF16), dob, (((0,), (0,)), ((), ())), preferred_element_type=F32)
        dp = lax.dot_general(dob, v_ref[0].astype(BF16), (((1,), (1,)), ((), ())), preferred_element_type=F32)
        ds = (p * (dp - dl_ref[0]) * scale).astype(BF16)
        dk_s[...] += lax.dot_general(ds, qb, (((0,), (0,)), ((), ())), preferred_element_type=F32)
        rows = pl.ds(pl.multiple_of(i * tq, tq), tq)
        dq_ref[0, rows, :] += jnp.dot(ds, kb, preferred_element_type=F32)

        @pl.when(i == nq - 1)
        def _():
            dk_ref[0] = dk_s[...]
            dv_ref[0] = dv_s[...]

    return pl.pallas_call(
        body,
        out_shape=(jax.ShapeDtypeStruct((H, L, DQ), F32), jax.ShapeDtypeStruct((H, L, DQ), F32),
                   jax.ShapeDtypeStruct((H, L, DV), F32)),
        grid=(H, L // tk, nq),
        in_specs=[pl.BlockSpec((1, tq, DQ), lambda h, j, i: (h, i, 0)),
                  pl.BlockSpec((1, tk, DQ), lambda h, j, i: (h, j, 0)),
                  pl.BlockSpec((1, tk, DV), lambda h, j, i: (h, j, 0)),
                  pl.BlockSpec((1, tq, DV), lambda h, j, i: (h, i, 0)),
                  pl.BlockSpec((1, tq, 1), lambda h, j, i: (h, i, 0)),
                  pl.BlockSpec((1, tq, 1), lambda h, j, i: (h, i, 0))],
        out_specs=(pl.BlockSpec((1, L, DQ), lambda h, j, i: (h, 0, 0)),
                   pl.BlockSpec((1, tk, DQ), lambda h, j, i: (h, j, 0)),
                   pl.BlockSpec((1, tk, DV), lambda h, j, i: (h, j, 0))),
        scratch_shapes=[pltpu.VMEM((tk, DQ), F32), pltpu.VMEM((tk, DV), F32)],
        compiler_params=_params(("arbitrary", "arbitrary", "arbitrary")),
        name="mla_attn_bwd",
    )(q, k, v, do, lse, delta)


@functools.partial(jax.custom_vjp, nondiff_argnums=(3,))
def full_attention(q, k, v, scale):
    return _attn_fwd_call(q, k, v, scale)[0]


def _full_attention_fwd(q, k, v, scale):
    o, lse = _attn_fwd_call(q, k, v, scale)
    return o, (q, k, v, o, lse)


def _full_attention_bwd(scale, res, do):
    q, k, v, o, lse = res
    delta = jnp.sum(do * o, axis=-1, keepdims=True)
    return _attn_bwd_call(q, k, v, do, lse, delta, scale)


full_attention.defvjp(_full_attention_fwd, _full_attention_bwd)


def _bdot(a, b, dims):
    return lax.dot_general(a.astype(BF16), b.astype(BF16), (dims, ((), ())), preferred_element_type=F32)


def _fdot(a, b, dims):
    return lax.dot_general(a, b, (dims, ((), ())), precision=_HIGHEST, preferred_element_type=F32)


_NN = ((1,), (0,))
_NT = ((1,), (1,))
_TN = ((0,), (0,))


def _swa_head(q, k, v, bias, sink, valid):
    s = _bdot(q, k, _NT) * (SWA_HEAD_DIM ** -0.5) + bias
    s = jnp.where(valid, s, -1e30)
    m = lax.stop_gradient(jnp.maximum(jnp.max(s, axis=1, keepdims=True), sink))
    p = jnp.exp(s - m)
    den = jnp.sum(p, axis=1, keepdims=True) + jnp.exp(sink - m)
    return _bdot(p / den, v, _NN)


def _swa_valid(n, L):
    qi = lax.broadcasted_iota(jnp.int32, (SWA_BLOCK, 3 * SWA_BLOCK), 0)
    sj = lax.broadcasted_iota(jnp.int32, (SWA_BLOCK, 3 * SWA_BLOCK), 1)
    rel = sj - SWA_BLOCK - qi
    kpos = n * SWA_BLOCK + sj - SWA_BLOCK
    return (jnp.abs(rel) <= WINDOW) & (kpos >= 0) & (kpos < L)


def _swa_specs(HQ, HKV, L):
    B = SWA_BLOCK
    q_spec = pl.BlockSpec((HQ, B, SWA_HEAD_DIM), lambda n: (0, n, 0))
    kv_spec = pl.BlockSpec((HKV, L + 2 * B, SWA_HEAD_DIM), lambda n: (0, 0, 0))
    bias_spec = pl.BlockSpec((HQ, B, 3 * B), lambda n: (0, 0, 0))
    sink_spec = pl.BlockSpec((HQ, B, 1), lambda n: (0, 0, 0))
    return q_spec, kv_spec, bias_spec, sink_spec


def _swa_fwd_call(q, kpad, vpad, bias, sink):
    HQ, L, _ = q.shape
    HKV = kpad.shape[0]
    B = SWA_BLOCK

    def body(q_ref, k_ref, v_ref, b_ref, s_ref, o_ref):
        n = pl.program_id(0)
        valid = _swa_valid(n, L)
        rows = pl.ds(pl.multiple_of(n * B, B), 3 * B)
        for g in range(HKV):
            kc = k_ref[g, rows, :]
            vc = v_ref[g, rows, :]
            for h in range(g * SWA_KV_GROUP, (g + 1) * SWA_KV_GROUP):
                o_ref[h] = _swa_head(q_ref[h], kc, vc, b_ref[h], s_ref[h], valid)

    q_spec, kv_spec, bias_spec, sink_spec = _swa_specs(HQ, HKV, L)
    return pl.pallas_call(
        body,
        out_shape=jax.ShapeDtypeStruct(q.shape, F32),
        grid=(L // B,),
        in_specs=[q_spec, kv_spec, kv_spec, bias_spec, sink_spec],
        out_specs=q_spec,
        compiler_params=_params(("parallel",)),
        name="swa_fwd",
    )(q, kpad, vpad, bias, sink)


def _swa_bwd_call(q, kpad, vpad, bias, sink, do):
    HQ, L, _ = q.shape
    HKV = kpad.shape[0]
    B = SWA_BLOCK

    def body(q_ref, k_ref, v_ref, b_ref, s_ref, do_ref, dq_ref, dk_ref, dv_ref, db_ref, ds_ref):
        n = pl.program_id(0)

        @pl.when(n == 0)
        def _():
            dk_ref[...] = jnp.zeros_like(dk_ref)
            dv_ref[...] = jnp.zeros_like(dv_ref)
            db_ref[...] = jnp.zeros_like(db_ref)
            ds_ref[...] = jnp.zeros_like(ds_ref)

        valid = _swa_valid(n, L)
        rows = pl.ds(pl.multiple_of(n * B, B), 3 * B)
        for g in range(HKV):
            kc = k_ref[g, rows, :]
            vc = v_ref[g, rows, :]
            dk_acc = jnp.zeros_like(kc)
            dv_acc = jnp.zeros_like(vc)
            for h in range(g * SWA_KV_GROUP, (g + 1) * SWA_KV_GROUP):
                _, vjp = jax.vjp(functools.partial(_swa_head, valid=valid), q_ref[h], kc, vc, b_ref[h], s_ref[h])
                dq, dk, dv, db, dsk = vjp(do_ref[h])
                dq_ref[h] = dq
                dk_acc += dk
                dv_acc += dv
                db_ref[h] += db
                ds_ref[h] += dsk
            dk_ref[g, rows, :] += dk_acc
            dv_ref[g, rows, :] += dv_acc

    q_spec, kv_spec, bias_spec, sink_spec = _swa_specs(HQ, HKV, L)
    return pl.pallas_call(
        body,
        out_shape=(jax.ShapeDtypeStruct(q.shape, F32), jax.ShapeDtypeStruct(kpad.shape, F32),
                   jax.ShapeDtypeStruct(vpad.shape, F32), jax.ShapeDtypeStruct(bias.shape, F32),
                   jax.ShapeDtypeStruct(sink.shape, F32)),
        grid=(L // B,),
        in_specs=[q_spec, kv_spec, kv_spec, bias_spec, sink_spec, q_spec],
        out_specs=(q_spec, kv_spec, kv_spec, bias_spec, sink_spec),
        compiler_params=_params(("arbitrary",)),
        name="swa_bwd",
    )(q, kpad, vpad, bias, sink, do)


@jax.custom_vjp
def window_attention(q, kpad, vpad, bias, sink):
    return _swa_fwd_call(q, kpad, vpad, bias, sink)


def _window_attention_fwd(q, kpad, vpad, bias, sink):
    return _swa_fwd_call(q, kpad, vpad, bias, sink), (q, kpad, vpad, bias, sink)


def _window_attention_bwd(res, do):
    return _swa_bwd_call(*res, do)


window_attention.defvjp(_window_attention_fwd, _window_attention_bwd)


def _gdn_chunk(S, q, k, v, gc, gr, gl, beta):
    C = q.shape[0]
    ii = lax.broadcasted_iota(jnp.int32, (C, C), 0)
    jj = lax.broadcasted_iota(jnp.int32, (C, C), 1)
    lower = ii >= jj
    strict = ii > jj
    decay = jnp.where(lower, jnp.exp(jnp.where(lower, gc - gr, 0.0)), 0.0)
    kb = k * beta
    a = jnp.where(strict, _bdot(kb, k, _NT) * decay, 0.0)
    pw = -a
    eye = jnp.where(ii == jj, 1.0, 0.0)
    t = eye + pw
    for _ in range(int(math.log2(C)) - 1):
        pw = _fdot(pw, pw, _NN)
        t = t + _fdot(t, pw, _NN)
    u = _fdot(t, v * beta, _NN)
    w = _fdot(t, kb * jnp.exp(gc), _NN)
    attn = _bdot(q, k, _NT) * decay
    v_new = u - _bdot(w, S, _NN)
    o = _bdot(q * jnp.exp(gc), S, _NN) + _bdot(attn, v_new, _NN)
    s_new = S * jnp.exp(gl) + _bdot(k * jnp.exp(gl - gc), v_new, _TN)
    return s_new, o


def _gdn_specs(NS, L, C, Dh, rev):
    NC = L // C
    idx = (lambda c: NC - 1 - c) if rev else (lambda c: c)
    seq = pl.BlockSpec((NS, C, Dh), lambda c: (0, idx(c), 0))
    col = pl.BlockSpec((NS, 1, C, 1), lambda c: (0, idx(c), 0, 0))
    row = pl.BlockSpec((NS, 1, 1, C), lambda c: (0, idx(c), 0, 0))
    one = pl.BlockSpec((NS, 1, 1, 1), lambda c: (0, idx(c), 0, 0))
    st = pl.BlockSpec((NS, 1, Dh, Dh), lambda c: (0, idx(c), 0, 0))
    return seq, col, row, one, st


def _gdn_fwd_call(q, k, v, gc, gr, gl, beta):
    NS, L, Dh = q.shape
    C = gc.shape[2]
    NC = L // C

    def body(q_ref, k_ref, v_ref, gc_ref, gr_ref, gl_ref, b_ref, o_ref, st_ref, s_scr):
        @pl.when(pl.program_id(0) == 0)
        def _():
            s_scr[...] = jnp.zeros_like(s_scr)

        for s in range(NS):
            s0 = s_scr[s]
            st_ref[s, 0] = s0
            s1, o = _gdn_chunk(s0, q_ref[s], k_ref[s], v_ref[s], gc_ref[s, 0], gr_ref[s, 0], gl_ref[s, 0], b_ref[s, 0])
            s_scr[s] = s1
            o_ref[s] = o

    seq, col, row, one, st = _gdn_specs(NS, L, C, Dh, False)
    return pl.pallas_call(
        body,
        out_shape=(jax.ShapeDtypeStruct(q.shape, F32), jax.ShapeDtypeStruct((NS, NC, Dh, Dh), F32)),
        grid=(NC,),
        in_specs=[seq, seq, seq, col, row, one, col],
        out_specs=(seq, st),
        scratch_shapes=[pltpu.VMEM((NS, Dh, Dh), F32)],
        compiler_params=_params(("arbitrary",)),
        name="gdn_fwd",
    )(q, k, v, gc, gr, gl, beta)


def _gdn_bwd_call(q, k, v, gc, gr, gl, beta, states, do):
    NS, L, Dh = q.shape
    C = gc.shape[2]
    NC = L // C

    def body(q_ref, k_ref, v_ref, gc_ref, gr_ref, gl_ref, b_ref, st_ref, do_ref,
             dq_ref, dk_ref, dv_ref, dgc_ref, dgr_ref, dgl_ref, db_ref, ds_scr):
        @pl.when(pl.program_id(0) == 0)
        def _():
            ds_scr[...] = jnp.zeros_like(ds_scr)

        for s in range(NS):
            _, vjp = jax.vjp(_gdn_chunk, st_ref[s, 0], q_ref[s], k_ref[s], v_ref[s],
                             gc_ref[s, 0], gr_ref[s, 0], gl_ref[s, 0], b_ref[s, 0])
            ds0, dq, dk, dv, dgc, dgr, dgl, db = vjp((ds_scr[s], do_ref[s]))
            ds_scr[s] = ds0
            dq_ref[s] = dq
            dk_ref[s] = dk
            dv_ref[s] = dv
            dgc_ref[s, 0] = dgc
            dgr_ref[s, 0] = dgr
            dgl_ref[s, 0] = dgl
            db_ref[s, 0] = db

    seq, col, row, one, st = _gdn_specs(NS, L, C, Dh, True)
    sds = jax.ShapeDtypeStruct
    return pl.pallas_call(
        body,
        out_shape=(sds(q.shape, F32), sds(q.shape, F32), sds(q.shape, F32), sds(gc.shape, F32),
                   sds(gr.shape, F32), sds(gl.shape, F32), sds(beta.shape, F32)),
        grid=(NC,),
        in_specs=[seq, seq, seq, col, row, one, col, st, seq],
        out_specs=(seq, seq, seq, col, row, one, col),
        scratch_shapes=[pltpu.VMEM((NS, Dh, Dh), F32)],
        compiler_params=_params(("arbitrary",)),
        name="gdn_bwd",
    )(q, k, v, gc, gr, gl, beta, states, do)


@jax.custom_vjp
def gated_delta_rule(q, k, v, gc, gr, gl, beta):
    return _gdn_fwd_call(q, k, v, gc, gr, gl, beta)[0]


def _gated_delta_rule_fwd(q, k, v, gc, gr, gl, beta):
    o, states = _gdn_fwd_call(q, k, v, gc, gr, gl, beta)
    return o, (q, k, v, gc, gr, gl, beta, states)


def _gated_delta_rule_bwd(res, do):
    return _gdn_bwd_call(*res, do)


gated_delta_rule.defvjp(_gated_delta_rule_fwd, _gated_delta_rule_bwd)


def _loss_fwd_call(y, t):
    L, D = y.shape
    tm = _rms_rows(L, D)

    def body(y_ref, t_ref, o_ref):
        @pl.when(pl.program_id(0) == 0)
        def _():
            o_ref[...] = jnp.zeros_like(o_ref)

        e = y_ref[...] - t_ref[...]
        part = jnp.sum(jnp.sum(e * e, axis=1, keepdims=True), axis=0, keepdims=True)
        o_ref[...] += part * (0.5 / D)

    out = pl.pallas_call(
        body,
        out_shape=jax.ShapeDtypeStruct((SUBLANES, LANES), F32),
        grid=(L // tm,),
        in_specs=[pl.BlockSpec((tm, D), lambda i: (i, 0)), pl.BlockSpec((tm, D), lambda i: (i, 0))],
        out_specs=pl.BlockSpec((SUBLANES, LANES), lambda i: (0, 0)),
        compiler_params=_params(("arbitrary",)),
        name="loss_fwd",
    )(y, t)
    return out[0, 0]


def _loss_bwd_call(y, t, g):
    L, D = y.shape
    tm = _rms_rows(L, D)

    def body(y_ref, t_ref, g_ref, o_ref):
        o_ref[...] = (y_ref[...] - t_ref[...]) * (g_ref[...] * (1.0 / D))

    return pl.pallas_call(
        body,
        out_shape=jax.ShapeDtypeStruct((L, D), F32),
        grid=(L // tm,),
        in_specs=[pl.BlockSpec((tm, D), lambda i: (i, 0)), pl.BlockSpec((tm, D), lambda i: (i, 0)),
                  pl.BlockSpec((1, 1), lambda i: (0, 0))],
        out_specs=pl.BlockSpec((tm, D), lambda i: (i, 0)),
        compiler_params=_params(("parallel",)),
        name="loss_bwd",
    )(y, t, g.reshape(1, 1))


@jax.custom_vjp
def loss_head(y, t):
    return _loss_fwd_call(y, t)


def _loss_head_fwd(y, t):
    return _loss_fwd_call(y, t), (y, t)


def _loss_head_bwd(res, g):
    y, t = res
    return _loss_bwd_call(y, t, g), jnp.zeros_like(t)


loss_head.defvjp(_loss_head_fwd, _loss_head_bwd)


def _rows_for(C, nbuf):
    return max(16, ((24 << 20) // (nbuf * 4 * C)) // 16 * 16)


def _pair_add_call(a, b, name):
    R, C = a.shape
    tm = _tile(R, _rows_for(C, 6), 16)

    def body(a_ref, b_ref, o_ref):
        o_ref[...] = (a_ref[...].astype(F32) + b_ref[...].astype(F32)).astype(o_ref.dtype)

    spec = pl.BlockSpec((tm, C), lambda i: (i, 0))
    return pl.pallas_call(
        body, out_shape=jax.ShapeDtypeStruct(a.shape, a.dtype), grid=(R // tm,),
        in_specs=[spec, spec], out_specs=spec, compiler_params=_params(("parallel",)), name=name,
    )(a, b)


def _adamw_call(parts, w, m, v, name):
    P, R, C = parts.shape
    tm = _tile(R, _rows_for(C, 2 * (P + 7)), 16)

    def body(p_ref, w_ref, m_ref, v_ref, g_ref, d_ref, mo_ref, vo_ref):
        g = p_ref[0].astype(F32)
        for s in range(1, P):
            g = g + p_ref[s].astype(F32)
        m2 = ADAM_B1 * m_ref[...] + (1.0 - ADAM_B1) * g
        v2 = ADAM_B2 * v_ref[...] + (1.0 - ADAM_B2) * jnp.square(g)
        m_hat = m2 / (1.0 - ADAM_B1 ** ADAM_STEP)
        v_hat = v2 / (1.0 - ADAM_B2 ** ADAM_STEP)
        g_ref[...] = g
        d_ref[...] = -ADAM_LR * (m_hat / (jnp.sqrt(v_hat) + ADAM_EPS) + ADAM_WD * w_ref[...])
        mo_ref[...] = m2
        vo_ref[...] = v2

    spec = pl.BlockSpec((tm, C), lambda i: (i, 0))
    out = jax.ShapeDtypeStruct((R, C), F32)
    return pl.pallas_call(
        body, out_shape=(out, out, out, out), grid=(R // tm,),
        in_specs=[pl.BlockSpec((P, tm, C), lambda i: (0, i, 0)), spec, spec, spec],
        out_specs=(spec, spec, spec, spec), compiler_params=_params(("parallel",)), name=name,
    )(parts, w, m, v)


_MESH = pl.DeviceIdType.MESH
FLIPS_CHIPS = ((1, 0, 0), (0, 1, 0), (1, 1, 0))
FLIPS_ALL = ((0, 0, 1), (1, 0, 0), (0, 1, 0), (1, 1, 0), (1, 0, 1), (0, 1, 1), (1, 1, 1))
FLIPS_SIBLING = ((0, 0, 1),)


def _exchange(arrays, flips, mode, name):
    n = len(arrays)
    nf = len(flips)
    over_c = any(f[2] for f in flips)
    n_slots = 8 if over_c else 4

    def slot(x, y, c):
        return 4 * x + 2 * y + c if over_c else 2 * x + y

    def body(*refs):
        ins, outs = refs[:n], refs[n:2 * n]
        send_sems, recv_sems, local_sems = refs[2 * n:]
        x, y, c = lax.axis_index("x"), lax.axis_index("y"), lax.axis_index("c")
        me = slot(x, y, c)
        peers = [(x + f[0] - 2 * x * f[0], y + f[1] - 2 * y * f[1], c + f[2] - 2 * c * f[2]) for f in flips]

        def copy(i, k, sending):
            px, py, pc = peers[k]
            there = slot(px, py, pc)
            if mode == "swap":
                src, dst = ins[i], outs[i]
            elif mode == "gather":
                src, dst = ins[i], outs[i].at[me if sending else there]
            else:
                src, dst = ins[i].at[there if sending else me], outs[i].at[me if sending else there]
            return pltpu.make_async_remote_copy(src_ref=src, dst_ref=dst, send_sem=send_sems.at[i, k],
                                                recv_sem=recv_sems.at[i, k], device_id=(px, py, pc),
                                                device_id_type=_MESH)

        sends = [copy(i, k, True) for i in range(n) for k in range(nf)]
        for cp in sends:
            cp.start()
        local = []
        if mode != "swap":
            for i in range(n):
                src = ins[i] if mode == "gather" else ins[i].at[me]
                cp = pltpu.make_async_copy(src, outs[i].at[me], local_sems.at[i])
                cp.start()
                local.append(cp)
        for i in range(n):
            for k in range(nf):
                copy(i, k, False).wait_recv()
        for cp in sends:
            cp.wait_send()
        for cp in local:
            cp.wait()

    if mode == "gather":
        out_shape = [jax.ShapeDtypeStruct((n_slots,) + a.shape, a.dtype) for a in arrays]
    else:
        out_shape = [jax.ShapeDtypeStruct(a.shape, a.dtype) for a in arrays]
    any_spec = pl.BlockSpec(memory_space=pl.ANY)
    return pl.pallas_call(
        body,
        out_shape=out_shape,
        in_specs=[any_spec] * n,
        out_specs=[any_spec] * n,
        scratch_shapes=[pltpu.SemaphoreType.DMA((n, nf)), pltpu.SemaphoreType.DMA((n, nf)),
                        pltpu.SemaphoreType.DMA((n,))],
        compiler_params=pltpu.CompilerParams(has_side_effects=True),
        name=name,
    )(*arrays)


WEIGHT_NAMES = ('w_in', 's5_lam_re', 's5_lam_im', 's5_log_step', 's5_b_re', 's5_b_im', 's5_c_re', 's5_c_im', 's5_d',
                's5_w_glu', 's5_b_glu', 'gdn_conv', 'gdn_a_log', 'gdn_dt_bias', 'gdn_o_gain', 'swa_sink', 't5_bias',
                'mla_q_gain', 'mla_kv_gain', 'mla_w_uq', 'mla_w_ukv', 'w_branch', 'w_out', 'mix_pre_gain',
                'mix_post_gain', 'mlp_pre_gain', 'mlp_post_gain', 'w_mlp_in', 'w_mlp_out')
SHARDED = {'w_in': (2, BF16), 's5_w_glu': (1, BF16), 'gdn_conv': (2, F32), 'mla_w_uq': (2, BF16),
           'mla_w_ukv': (2, BF16), 'w_branch': (3, BF16), 'w_out': (1, BF16), 'w_mlp_in': (2, BF16),
           'w_mlp_out': (1, BF16)}
REPLICATED = tuple(n for n in WEIGHT_NAMES if n not in SHARDED)
PACK_COLS = 1024

_IN_A = (('s5_u', 512), ('gdn_qkv', 1536), ('gdn_z', 512), ('gdn_beta', 8), ('gdn_decay', 8))
_IN_B = (('swa_q', 512), ('swa_kv', 256), ('mla_cq', 384), ('mla_ckv', 512), ('mla_kr', 64))
_IN_A_W = sum(w for _, w in _IN_A)
_IN_B_W = sum(w for _, w in _IN_B)
_IN_A_PAD = -_IN_A_W % LANES
_IN_B_PAD = -(_IN_A_W + _IN_A_PAD + _IN_B_W) % 512


def _assemble(g, axis):
    t = jnp.moveaxis(g, 0, axis)
    shape = t.shape[:axis] + (t.shape[axis] * t.shape[axis + 1],) + t.shape[axis + 2:]
    return t.reshape(shape)


def _s5_mixer(u, lam_re, lam_im, log_step, b_re, b_im, c_re, c_im, d_skip, w_glu, b_glu):
    nd, G, P = lam_re.shape
    Hg = b_re.shape[-1]
    lam_re = jnp.minimum(lam_re, -1e-4)
    dt = jnp.exp(log_step)[..., None]
    mag = jnp.exp(lam_re * dt)
    abar_r = mag * jnp.cos(lam_im * dt)
    abar_i = mag * jnp.sin(lam_im * dt)
    den = lam_re * lam_re + lam_im * lam_im
    xr = abar_r - 1.0
    xi = abar_i
    coef_r = (xr * lam_re + xi * lam_im) / den
    coef_i = (xi * lam_re - xr * lam_im) / den
    bbar_r = coef_r[..., None] * b_re - coef_i[..., None] * b_im
    bbar_i = coef_r[..., None] * b_im + coef_i[..., None] * b_re
    eye = jnp.eye(G, dtype=F32)

    def dense_b(bb):
        return jnp.einsum('dgph,gk->ghdkp', bb, eye).reshape(G * Hg, nd, G * P)

    def dense_c(cc):
        return jnp.einsum('dghp,gk->khdgp', cc, eye).reshape(G * Hg, nd, G * P)

    b_cat = _to_scan_cols(jnp.stack([dense_b(bbar_r), dense_b(bbar_i)], axis=2), nd)
    c_cat = _to_scan_cols(jnp.stack([dense_c(c_re), -dense_c(c_im)], axis=2), nd).T
    a_row = _to_scan_cols(jnp.stack([abar_r.reshape(nd, G * P), abar_i.reshape(nd, G * P)], axis=1), nd)[None]
    s = s5_scan(linear(u, b_cat, 's5_in'), a_row, nd)
    y = linear(s, c_cat, 's5_out') + d_skip * u
    y = jax.nn.gelu(y)
    return y * jax.nn.sigmoid(linear(y, w_glu, 's5_glu') + b_glu)


def _gdn_mixer(qkv, z, beta_logits, decay_logits, conv_w, a_log, dt_bias, o_gain):
    L = qkv.shape[0]
    Dh = GDN_HEAD_DIM
    H = z.shape[1] // Dh
    C = GDN_CHUNK
    NC = L // C
    xp = jnp.pad(qkv, ((GDN_CONV // 2, GDN_CONV - 1 - GDN_CONV // 2), (0, 0)))
    conv = xp[0:L] * conv_w[0]
    for j in range(1, GDN_CONV):
        conv = conv + xp[j:j + L] * conv_w[j]
    q, k, v = jnp.split(jax.nn.silu(conv), 3, axis=-1)

    def l2n(t):
        return t * lax.rsqrt(jnp.sum(t * t, axis=-1, keepdims=True) + 1e-6)

    q = l2n(q.reshape(L, H, Dh)) * (Dh ** -0.5)
    k = l2n(k.reshape(L, H, Dh))
    v = v.reshape(L, H, Dh)
    beta = jax.nn.sigmoid(beta_logits).reshape(L, 2, H)
    g = -jnp.exp(a_log) * jax.nn.softplus(decay_logits.reshape(L, 2, H) + dt_bias)

    def seqs(t):
        th = t.transpose(1, 0, 2)
        return jnp.concatenate([th, th[:, ::-1]], axis=0)

    def scalars(t):
        return jnp.concatenate([t[:, 0].T, t[::-1, 1].T], axis=0).reshape(2 * H, NC, C)

    gs = jnp.cumsum(scalars(g), axis=-1)
    o = gated_delta_rule(seqs(q), seqs(k), seqs(v), gs[..., None], gs[:, :, None, :],
                         gs[:, :, -1][..., None, None], scalars(beta)[..., None])
    o = (o[:H] + o[H:, ::-1]).transpose(1, 0, 2)
    o = rms_norm(o.reshape(L * H, Dh), o_gain, 'gdn_onorm').reshape(L, H, Dh)
    o = o * jax.nn.silu(z.reshape(L, H, Dh))
    return o.reshape(L, H * Dh)


def _t5_bucket(rel):
    nb = T5_BUCKETS // 2
    max_exact = nb // 2
    ret = jnp.where(rel > 0, nb, 0)
    n = jnp.abs(rel)
    nf = jnp.maximum(n, 1).astype(F32)
    large = max_exact + (jnp.log(nf / max_exact) / math.log(T5_MAX_DISTANCE / max_exact)
                         * (nb - max_exact)).astype(jnp.int32)
    large = jnp.minimum(large, nb - 1)
    return ret + jnp.where(n < max_exact, n, large)


def _swa_mixer(q, kv, sink, t5_bias):
    L = q.shape[0]
    B = SWA_BLOCK
    HQ = q.shape[1] // SWA_HEAD_DIM
    HKV = HQ // SWA_KV_GROUP
    qh = q.reshape(L, HQ, SWA_HEAD_DIM).transpose(1, 0, 2)
    k, v = jnp.split(kv, 2, axis=-1)

    def heads_padded(t):
        return jnp.pad(t.reshape(L, HKV, SWA_HEAD_DIM).transpose(1, 0, 2), ((0, 0), (B, B), (0, 0)))

    qi = jnp.arange(B)[:, None]
    sj = jnp.arange(3 * B)[None, :]
    bias = t5_bias[_t5_bucket(sj - B - qi)].transpose(2, 0, 1)
    sink_rows = jnp.broadcast_to(sink[:, None, None], (HQ, B, 1))
    o = window_attention(qh, heads_padded(k), heads_padded(v), bias, sink_rows)
    return o.transpose(1, 0, 2).reshape(L, HQ * SWA_HEAD_DIM)


def _apply_rope(x, cos, sin):
    x1, x2 = jnp.split(x, 2, axis=-1)
    return jnp.concatenate([x1 * cos - x2 * sin, x2 * cos + x1 * sin], axis=-1)


def _mla_mixer(c_q, c_kv, k_rope, q_gain, kv_gain, w_uq, w_ukv):
    L = c_q.shape[0]
    H = w_uq.shape[1] // (MLA_NOPE + MLA_ROPE)
    q = linear(rms_norm(c_q, q_gain, 'mla_qnorm'), w_uq, 'mla_uq').reshape(L, H, MLA_NOPE + MLA_ROPE)
    kv = linear(rms_norm(c_kv, kv_gain, 'mla_kvnorm'), w_ukv, 'mla_ukv').reshape(L, H, MLA_NOPE + MLA_V)
    q_nope, q_pe = q[..., :MLA_NOPE], q[..., MLA_NOPE:]
    k_nope, v = kv[..., :MLA_NOPE], kv[..., MLA_NOPE:]
    pos = jnp.arange(L, dtype=F32)
    inv_freq = ROPE_THETA ** (-jnp.arange(0, MLA_ROPE, 2, dtype=F32) / MLA_ROPE)
    ang = pos[:, None] * inv_freq[None, :]
    cos, sin = jnp.cos(ang)[:, None, :], jnp.sin(ang)[:, None, :]
    q_pe = _apply_rope(q_pe, cos, sin)
    k_pe = _apply_rope(k_rope[:, None, :], cos, sin)
    qf = jnp.concatenate([q_nope, q_pe], axis=-1).transpose(1, 0, 2)
    kf = jnp.concatenate([k_nope, jnp.broadcast_to(k_pe, (L, H, MLA_ROPE))], axis=-1).transpose(1, 0, 2)
    o = full_attention(qf, kf, v.transpose(1, 0, 2), (MLA_NOPE + MLA_ROPE) ** -0.5)
    return o.transpose(1, 0, 2).reshape(L, H * MLA_V)


def _split_cols(t, segments, start):
    out = {}
    for name, width in segments:
        out[name] = t[:, start:start + width]
        start += width
    return out, start


def _local_loss(weights, x, target):
    p = {n: (_assemble(weights[n], SHARDED[n][0]) if n in SHARDED else weights[n]) for n in WEIGHT_NAMES}
    L, D = x.shape
    depth = p['w_in'].shape[0]
    for l in range(depth):
        w_in = p['w_in'][l]
        zeros = lambda n: jnp.zeros((D, n), w_in.dtype)
        w_r = jnp.concatenate([w_in[:, :_IN_A_W], zeros(_IN_A_PAD), w_in[:, _IN_A_W:_IN_A_W + _IN_B_W],
                               zeros(_IN_B_PAD)], axis=1)
        w_g = w_in[:, _IN_A_W + _IN_B_W:]
        h = rms_norm(x, p['mix_pre_gain'][l], 'mix_pre')
        proj = linear(h, w_r, 'in_proj')
        gate_logits = linear(h, w_g, 'in_gate')
        seg, end = _split_cols(proj, _IN_A, 0)
        seg_b, _ = _split_cols(proj, _IN_B, end + _IN_A_PAD)
        seg.update(seg_b)
        y_a = _s5_mixer(seg['s5_u'], p['s5_lam_re'][l], p['s5_lam_im'][l], p['s5_log_step'][l], p['s5_b_re'][l],
                        p['s5_b_im'][l], p['s5_c_re'][l], p['s5_c_im'][l], p['s5_d'][l], p['s5_w_glu'][l],
                        p['s5_b_glu'][l])
        y_b = _gdn_mixer(seg['gdn_qkv'], seg['gdn_z'], seg['gdn_beta'], seg['gdn_decay'], p['gdn_conv'][l],
                         p['gdn_a_log'][l], p['gdn_dt_bias'][l], p['gdn_o_gain'][l])
        y_c = _swa_mixer(seg['swa_q'], seg['swa_kv'], p['swa_sink'][l], p['t5_bias'])
        y_d = _mla_mixer(seg['mla_cq'], seg['mla_ckv'], seg['mla_kr'], p['mla_q_gain'][l], p['mla_kv_gain'][l],
                         p['mla_w_uq'][l], p['mla_w_ukv'][l])
        gates = jax.nn.sigmoid(gate_logits)
        merged = None
        for b, y in enumerate((y_a, y_b, y_c, y_d)):
            term = gates[:, b * D:(b + 1) * D] * linear(y, p['w_branch'][l, b], 'branch')
            merged = term if merged is None else merged + term
        x = x + rms_norm(linear(merged, p['w_out'][l], 'mix_out'), p['mix_post_gain'][l], 'mix_post')
        h = rms_norm(x, p['mlp_pre_gain'][l], 'mlp_pre')
        f = linear(jnp.square(jax.nn.relu(linear(h, p['w_mlp_in'][l], 'mlp_in'))), p['w_mlp_out'][l], 'mlp_out')
        x = x + rms_norm(f, p['mlp_post_gain'][l], 'mlp_post')
    return loss_head(x, target)


def _two_d(t, lead):
    return t.reshape(t.shape[:lead] + (-1, t.shape[-1]))


def _pack(arrays):
    flat = jnp.concatenate([a.reshape(-1) for a in arrays])
    pad = -flat.shape[0] % (16 * PACK_COLS)
    return jnp.pad(flat, (0, pad)).reshape(-1, PACK_COLS)


def _unpack(packed, like):
    flat = packed.reshape(-1)
    out, pos = [], 0
    for a in like:
        out.append(flat[pos:pos + a.size].reshape(a.shape))
        pos += a.size
    return out


def kernel(x, w_in, s5_lam_re, s5_lam_im, s5_log_step, s5_b_re, s5_b_im, s5_c_re, s5_c_im, s5_d, s5_w_glu, s5_b_glu, gdn_conv, gdn_a_log, gdn_dt_bias, gdn_o_gain, swa_sink, t5_bias, mla_q_gain, mla_kv_gain, mla_w_uq, mla_w_ukv, w_branch, w_out, mix_pre_gain, mix_post_gain, mlp_pre_gain, mlp_post_gain, w_mlp_in, w_mlp_out, loss_target, m_w_in, m_s5_lam_re, m_s5_lam_im, m_s5_log_step, m_s5_b_re, m_s5_b_im, m_s5_c_re, m_s5_c_im, m_s5_d, m_s5_w_glu, m_s5_b_glu, m_gdn_conv, m_gdn_a_log, m_gdn_dt_bias, m_gdn_o_gain, m_swa_sink, m_t5_bias, m_mla_q_gain, m_mla_kv_gain, m_mla_w_uq, m_mla_w_ukv, m_w_branch, m_w_out, m_mix_pre_gain, m_mix_post_gain, m_mlp_pre_gain, m_mlp_post_gain, m_w_mlp_in, m_w_mlp_out, v_w_in, v_s5_lam_re, v_s5_lam_im, v_s5_log_step, v_s5_b_re, v_s5_b_im, v_s5_c_re, v_s5_c_im, v_s5_d, v_s5_w_glu, v_s5_b_glu, v_gdn_conv, v_gdn_a_log, v_gdn_dt_bias, v_gdn_o_gain, v_swa_sink, v_t5_bias, v_mla_q_gain, v_mla_kv_gain, v_mla_w_uq, v_mla_w_ukv, v_w_branch, v_w_out, v_mix_pre_gain, v_mix_post_gain, v_mlp_pre_gain, v_mlp_post_gain, v_w_mlp_in, v_w_mlp_out):
    w = dict(zip(WEIGHT_NAMES, (w_in, s5_lam_re, s5_lam_im, s5_log_step, s5_b_re, s5_b_im, s5_c_re, s5_c_im, s5_d, s5_w_glu, s5_b_glu, gdn_conv, gdn_a_log, gdn_dt_bias, gdn_o_gain, swa_sink, t5_bias, mla_q_gain, mla_kv_gain, mla_w_uq, mla_w_ukv, w_branch, w_out, mix_pre_gain, mix_post_gain, mlp_pre_gain, mlp_post_gain, w_mlp_in, w_mlp_out)))
    m = dict(zip(WEIGHT_NAMES, (m_w_in, m_s5_lam_re, m_s5_lam_im, m_s5_log_step, m_s5_b_re, m_s5_b_im, m_s5_c_re, m_s5_c_im, m_s5_d, m_s5_w_glu, m_s5_b_glu, m_gdn_conv, m_gdn_a_log, m_gdn_dt_bias, m_gdn_o_gain, m_swa_sink, m_t5_bias, m_mla_q_gain, m_mla_kv_gain, m_mla_w_uq, m_mla_w_ukv, m_w_branch, m_w_out, m_mix_pre_gain, m_mix_post_gain, m_mlp_pre_gain, m_mlp_post_gain, m_w_mlp_in, m_w_mlp_out)))
    v = dict(zip(WEIGHT_NAMES, (v_w_in, v_s5_lam_re, v_s5_lam_im, v_s5_log_step, v_s5_b_re, v_s5_b_im, v_s5_c_re, v_s5_c_im, v_s5_d, v_s5_w_glu, v_s5_b_glu, v_gdn_conv, v_gdn_a_log, v_gdn_dt_bias, v_gdn_o_gain, v_swa_sink, v_t5_bias, v_mla_q_gain, v_mla_kv_gain, v_mla_w_uq, v_mla_w_ukv, v_w_branch, v_w_out, v_mix_pre_gain, v_mix_post_gain, v_mlp_pre_gain, v_mlp_post_gain, v_w_mlp_in, v_w_mlp_out)))
    sharded = tuple(SHARDED)

    gathered = _exchange([w[n].astype(SHARDED[n][1]) for n in sharded], FLIPS_CHIPS, 'gather', 'weights_all_gather')
    weights = dict(zip(sharded, gathered))
    weights.update({n: w[n] for n in REPLICATED})

    loss, (grads, grad_x) = jax.value_and_grad(_local_loss, argnums=(0, 1))(weights, x[0], loss_target[0])
    loss = lax.psum(loss, ('x', 'y', 'c'))

    from_sibling = _exchange([grads[n] for n in sharded], FLIPS_SIBLING, 'swap', 'grads_core_swap')
    chip_sums = [_pair_add_call(_two_d(grads[n], 0), _two_d(r, 0), 'grads_core_add').reshape(r.shape)
                 for n, r in zip(sharded, from_sibling)]
    per_chip = _exchange(chip_sums, FLIPS_CHIPS, 'scatter', 'grads_chip_scatter')
    out = {}
    for n, parts in zip(sharded, per_chip):
        res = _adamw_call(_two_d(parts, 1), _two_d(w[n], 0), _two_d(m[n], 0), _two_d(v[n], 0), 'adamw_sharded')
        out[n] = tuple(r.reshape(w[n].shape) for r in res)

    small = [grads[n] for n in REPLICATED]
    all_parts = _exchange([_pack(small)], FLIPS_ALL, 'gather', 'grads_all_gather')[0]
    res = _adamw_call(all_parts, _pack([w[n] for n in REPLICATED]), _pack([m[n] for n in REPLICATED]),
                      _pack([v[n] for n in REPLICATED]), 'adamw_replicated')
    unpacked = [_unpack(r, small) for r in res]
    for i, n in enumerate(REPLICATED):
        out[n] = tuple(u[i] for u in unpacked)

    return (loss, grad_x[None]) + tuple(out[n][k] for k in range(4) for n in WEIGHT_NAMES)
```

```python
import functools
import math

import numpy as np
import jax
import jax.numpy as jnp
from jax import lax
from jax.experimental import pallas as pl
from jax.experimental.pallas import tpu as pltpu

F32 = jnp.float32
BF16 = jnp.bfloat16

VMEM_LIMIT_BYTES = 56 * 1024 * 1024
SUBLANES = 8
LANES = 128

NORM_EPS = 1e-6
DEPTH = 4
N_BRANCHES = 4
S5_GROUP = 16
S5_STATE = 64
GDN_HEAD_DIM = 128
GDN_CONV = 4
GDN_CHUNK = 64
SWA_HEAD_DIM = 64
SWA_KV_GROUP = 4
WINDOW = 128
SWA_BLOCK = 128
T5_BUCKETS = 32
T5_MAX_DISTANCE = 128
MLA_NOPE = 128
MLA_ROPE = 64
MLA_V = 128
ROPE_THETA = 10000.0

ADAM_LR = 0.001
ADAM_B1 = 0.9
ADAM_B2 = 0.999
ADAM_EPS = 1e-08
ADAM_WD = 0.01
ADAM_STEP = 10

_HIGHEST = lax.Precision.HIGHEST


def _params(sem, vmem=VMEM_LIMIT_BYTES):
    return pltpu.CompilerParams(dimension_semantics=sem, vmem_limit_bytes=vmem)


def _tile(dim, target, align):
    if dim <= target:
        return dim
    t = (target // align) * align
    while t >= align:
        if dim % t == 0:
            return t
        t -= align
    return dim


_DOT_DIMS = {"nn": ((1,), (0,)), "nt": ((1,), (1,)), "tn": ((0,), (0,))}


MM_OPERAND_TILE_BYTES = 12 * 1024 * 1024


def _mm(a, b, form="nn", out_dtype=F32, tm=1024, tn=1024, tk=None, name="mm"):
    if form == "nn":
        (M, K), (K2, N) = a.shape, b.shape
    elif form == "nt":
        (M, K), (N, K2) = a.shape, b.shape
    else:
        (K, M), (K2, N) = a.shape, b.shape
    assert K == K2, (a.shape, b.shape, form)
    tm = _tile(M, tm, LANES if form == "tn" else 16)
    tn = _tile(N, tn, LANES)
    if tk is None:
        tk = MM_OPERAND_TILE_BYTES // (tm * a.dtype.itemsize + tn * b.dtype.itemsize)
    tk = _tile(K, max(LANES, tk // LANES * LANES), LANES)
    nk = K // tk
    dims = (_DOT_DIMS[form], ((), ()))

    def dot(a_ref, b_ref):
        return lax.dot_general(a_ref[...].astype(BF16), b_ref[...].astype(BF16), dims, preferred_element_type=F32)

    def body_single(a_ref, b_ref, o_ref):
        o_ref[...] = dot(a_ref, b_ref).astype(o_ref.dtype)

    def body_acc(a_ref, b_ref, o_ref, acc_ref):
        k = pl.program_id(2)

        @pl.when(k == 0)
        def _():
            acc_ref[...] = jnp.zeros_like(acc_ref)

        acc_ref[...] += dot(a_ref, b_ref)

        @pl.when(k == nk - 1)
        def _():
            o_ref[...] = acc_ref[...].astype(o_ref.dtype)

    body = body_single if nk == 1 else body_acc

    if form == "nn":
        a_spec = pl.BlockSpec((tm, tk), lambda i, j, k: (i, k))
        b_spec = pl.BlockSpec((tk, tn), lambda i, j, k: (k, j))
    elif form == "nt":
        a_spec = pl.BlockSpec((tm, tk), lambda i, j, k: (i, k))
        b_spec = pl.BlockSpec((tn, tk), lambda i, j, k: (j, k))
    else:
        a_spec = pl.BlockSpec((tk, tm), lambda i, j, k: (k, i))
        b_spec = pl.BlockSpec((tk, tn), lambda i, j, k: (k, j))
    return pl.pallas_call(
        body,
        out_shape=jax.ShapeDtypeStruct((M, N), out_dtype),
        grid=(M // tm, N // tn, nk),
        in_specs=[a_spec, b_spec],
        out_specs=pl.BlockSpec((tm, tn), lambda i, j, k: (i, j)),
        scratch_shapes=[] if nk == 1 else [pltpu.VMEM((tm, tn), F32)],
        compiler_params=_params(("parallel", "parallel", "arbitrary")),
        name=name,
    )(a, b)


@functools.partial(jax.custom_vjp, nondiff_argnums=(2,))
def linear(x, w, name):
    return _mm(x, w, "nn", name=name + "_fwd")


def _linear_fwd(x, w, name):
    return _mm(x, w, "nn", name=name + "_fwd"), (x, w)


def _linear_bwd(name, res, dy):
    x, w = res
    dx = _mm(dy, w, "nt", out_dtype=x.dtype, name=name + "_dx")
    dw = _mm(x, dy, "tn", out_dtype=w.dtype, name=name + "_dw")
    return dx, dw


linear.defvjp(_linear_fwd, _linear_bwd)


def _rms_rows(L, D):
    return _tile(L, max(SUBLANES, (1 << 20) // D), SUBLANES)


def _rms_fwd_call(x, gain, name):
    L, D = x.shape
    tm = _rms_rows(L, D)

    def body(x_ref, g_ref, o_ref):
        xv = x_ref[...]
        r = lax.rsqrt(jnp.mean(xv * xv, axis=-1, keepdims=True) + NORM_EPS)
        o_ref[...] = xv * r * g_ref[...]

    return pl.pallas_call(
        body,
        out_shape=jax.ShapeDtypeStruct((L, D), F32),
        grid=(L // tm,),
        in_specs=[pl.BlockSpec((tm, D), lambda i: (i, 0)), pl.BlockSpec((1, D), lambda i: (0, 0))],
        out_specs=pl.BlockSpec((tm, D), lambda i: (i, 0)),
        compiler_params=_params(("parallel",)),
        name=name,
    )(x, gain.reshape(1, D))


def _rms_bwd_call(x, gain, dy, name):
    L, D = x.shape
    tm = _rms_rows(L, D)

    def body(x_ref, g_ref, dy_ref, dx_ref, dg_ref):
        xv = x_ref[...]
        dyv = dy_ref[...]
        r = lax.rsqrt(jnp.mean(xv * xv, axis=-1, keepdims=True) + NORM_EPS)
        xh = xv * r
        dyg = dyv * g_ref[...]
        dx_ref[...] = r * (dyg - xh * jnp.mean(dyg * xh, axis=-1, keepdims=True))

        @pl.when(pl.program_id(0) == 0)
        def _():
            dg_ref[...] = jnp.zeros_like(dg_ref)

        dg_ref[...] += jnp.sum(dyv * xh, axis=0, keepdims=True)

    return pl.pallas_call(
        body,
        out_shape=(jax.ShapeDtypeStruct((L, D), F32), jax.ShapeDtypeStruct((1, D), F32)),
        grid=(L // tm,),
        in_specs=[pl.BlockSpec((tm, D), lambda i: (i, 0)), pl.BlockSpec((1, D), lambda i: (0, 0)),
                  pl.BlockSpec((tm, D), lambda i: (i, 0))],
        out_specs=(pl.BlockSpec((tm, D), lambda i: (i, 0)), pl.BlockSpec((1, D), lambda i: (0, 0))),
        compiler_params=_params(("arbitrary",)),
        name=name,
    )(x, gain.reshape(1, D), dy)


@functools.partial(jax.custom_vjp, nondiff_argnums=(2,))
def rms_norm(x, gain, name):
    return _rms_fwd_call(x, gain, name + "_fwd")


def _rms_norm_fwd(x, gain, name):
    return _rms_fwd_call(x, gain, name + "_fwd"), (x, gain)


def _rms_norm_bwd(name, res, dy):
    x, gain = res
    dx, dg = _rms_bwd_call(x, gain, dy, name + "_bwd")
    return dx, dg.reshape(gain.shape)


rms_norm.defvjp(_rms_norm_fwd, _rms_norm_bwd)


SCAN_COLS = 512
SCAN_ROWS = 256
SCAN_UNROLL = 8


def _to_scan_cols(t, nd):
    P = t.shape[-1]
    pc = _tile(P, SCAN_COLS, LANES)
    lead = t.shape[:-3]
    t = t.reshape(lead + (nd, 2, P // pc, pc))
    t = jnp.swapaxes(t, -3, -2)
    return t.reshape(lead + (nd * 2 * P,))


def _scan_call(b, a, nd, reverse_dir0, name):
    L, NC = b.shape
    P = NC // (2 * nd)
    pc = _tile(P, SCAN_COLS, LANES)
    ncb = P // pc
    T = _tile(L, SCAN_ROWS, SUBLANES)
    nt = L // T
    rev0 = 1 if reverse_dir0 else 0

    def rev_of(d):
        return d + rev0 - 2 * d * rev0

    def body(b_ref, a_ref, o_ref, st_ref):
        rv = rev_of(pl.program_id(0))

        @pl.when(pl.program_id(2) == 0)
        def _():
            st_ref[...] = jnp.zeros_like(st_ref)

        ar = a_ref[:, :pc]
        ai = a_ref[:, pc:]

        def step(j, carry):
            sr, si = carry
            r = j + rv * (T - 1 - 2 * j)
            row = b_ref[pl.ds(r, 1), :]
            nr = ar * sr - ai * si + row[:, :pc]
            ni = ar * si + ai * sr + row[:, pc:]
            o_ref[pl.ds(r, 1), :] = jnp.concatenate([nr, ni], axis=1)
            return nr, ni

        sr, si = lax.fori_loop(0, T, step, (st_ref[:, :pc], st_ref[:, pc:]), unroll=SCAN_UNROLL)
        st_ref[...] = jnp.concatenate([sr, si], axis=1)

    def t_idx(d, i):
        return i + rev_of(d) * (nt - 1 - 2 * i)

    spec = pl.BlockSpec((T, 2 * pc), lambda d, c, i: (t_idx(d, i), d * ncb + c))
    return pl.pallas_call(
        body,
        out_shape=jax.ShapeDtypeStruct(b.shape, F32),
        grid=(nd, ncb, nt),
        in_specs=[spec, pl.BlockSpec((1, 2 * pc), lambda d, c, i: (0, d * ncb + c))],
        out_specs=spec,
        scratch_shapes=[pltpu.VMEM((1, 2 * pc), F32)],
        compiler_params=_params(("arbitrary", "arbitrary", "arbitrary")),
        name=name,
    )(b, a)


def _scan_da_call(g, sp, nd, name):
    L, NC = g.shape
    P = NC // (2 * nd)
    pc = _tile(P, SCAN_COLS, LANES)
    T = _tile(L, SCAN_ROWS * 2, SUBLANES)

    def body(g_ref, s_ref, o_ref):
        @pl.when(pl.program_id(1) == 0)
        def _():
            o_ref[...] = jnp.zeros_like(o_ref)

        gr, gi = g_ref[:, :pc], g_ref[:, pc:]
        sr, si = s_ref[:, :pc], s_ref[:, pc:]
        dar = jnp.sum(gr * sr + gi * si, axis=0, keepdims=True)
        dai = jnp.sum(gi * sr - gr * si, axis=0, keepdims=True)
        o_ref[...] += jnp.concatenate([dar, dai], axis=1)

    spec = pl.BlockSpec((T, 2 * pc), lambda c, i: (i, c))
    return pl.pallas_call(
        body,
        out_shape=jax.ShapeDtypeStruct((1, NC), F32),
        grid=(NC // (2 * pc), L // T),
        in_specs=[spec, spec],
        out_specs=pl.BlockSpec((1, 2 * pc), lambda c, i: (0, c)),
        compiler_params=_params(("arbitrary", "arbitrary")),
        name=name,
    )(g, sp)


def _conj_cols(a, nd):
    P = a.shape[-1] // (2 * nd)
    pc = _tile(P, SCAN_COLS, LANES)
    sign = jnp.tile(jnp.concatenate([jnp.ones((pc,), F32), -jnp.ones((pc,), F32)]), a.shape[-1] // (2 * pc))
    return a * sign


@functools.partial(jax.custom_vjp, nondiff_argnums=(2,))
def s5_scan(b, a, nd):
    return _scan_call(b, a, nd, False, "s5_scan_fwd")


def _s5_scan_fwd(b, a, nd):
    s = _scan_call(b, a, nd, False, "s5_scan_fwd")
    return s, (s, a)


def _s5_scan_bwd(nd, res, ds):
    s, a = res
    g = _scan_call(ds, _conj_cols(a, nd), nd, True, "s5_scan_adj")
    L, NC = s.shape
    half = NC // nd
    zero = jnp.zeros((1, half), F32)
    prev = [jnp.concatenate([zero, s[:-1, :half]], axis=0)]
    if nd == 2:
        prev.append(jnp.concatenate([s[1:, half:], zero], axis=0))
    sp = jnp.concatenate(prev, axis=1)
    da = _scan_da_call(g, sp, nd, "s5_scan_da")
    return g, da


s5_scan.defvjp(_s5_scan_fwd, _s5_scan_bwd)


ATTN_TQ = 1024
ATTN_TK = 1024
ATTN_BWD_TQ = 512
ATTN_BWD_TK = 512


def _attn_fwd_call(q, k, v, scale):
    H, L, DQ = q.shape
    DV = v.shape[-1]
    tq = _tile(L, ATTN_TQ, LANES)
    tk = _tile(L, ATTN_TK, LANES)
    nk = L // tk

    def body(q_ref, k_ref, v_ref, o_ref, lse_ref, m_s, l_s, acc_s):
        j = pl.program_id(2)

        @pl.when(j == 0)
        def _():
            m_s[...] = jnp.full_like(m_s, -jnp.inf)
            l_s[...] = jnp.zeros_like(l_s)
            acc_s[...] = jnp.zeros_like(acc_s)

        s = lax.dot_general(q_ref[0].astype(BF16), k_ref[0].astype(BF16), (((1,), (1,)), ((), ())),
                            preferred_element_type=F32) * scale
        m_old = m_s[...]
        m_new = jnp.maximum(m_old, jnp.max(s, axis=1, keepdims=True))
        alpha = jnp.exp(m_old - m_new)
        p = jnp.exp(s - m_new)
        l_s[...] = alpha * l_s[...] + jnp.sum(p, axis=1, keepdims=True)
        acc_s[...] = alpha * acc_s[...] + jnp.dot(p.astype(BF16), v_ref[0].astype(BF16), preferred_element_type=F32)
        m_s[...] = m_new

        @pl.when(j == nk - 1)
        def _():
            o_ref[0] = acc_s[...] / l_s[...]
            lse_ref[0] = m_s[...] + jnp.log(l_s[...])

    return pl.pallas_call(
        body,
        out_shape=(jax.ShapeDtypeStruct((H, L, DV), F32), jax.ShapeDtypeStruct((H, L, 1), F32)),
        grid=(H, L // tq, nk),
        in_specs=[pl.BlockSpec((1, tq, DQ), lambda h, i, j: (h, i, 0)),
                  pl.BlockSpec((1, tk, DQ), lambda h, i, j: (h, j, 0)),
                  pl.BlockSpec((1, tk, DV), lambda h, i, j: (h, j, 0))],
        out_specs=(pl.BlockSpec((1, tq, DV), lambda h, i, j: (h, i, 0)),
                   pl.BlockSpec((1, tq, 1), lambda h, i, j: (h, i, 0))),
        scratch_shapes=[pltpu.VMEM((tq, 1), F32), pltpu.VMEM((tq, 1), F32), pltpu.VMEM((tq, DV), F32)],
        compiler_params=_params(("parallel", "parallel", "arbitrary")),
        name="mla_attn_fwd",
    )(q, k, v)


def _attn_bwd_call(q, k, v, do, lse, delta, scale):
    H, L, DQ = q.shape
    DV = v.shape[-1]
    tq = _tile(L, ATTN_BWD_TQ, LANES)
    tk = _tile(L, ATTN_BWD_TK, LANES)
    nq = L // tq

    def body(q_ref, k_ref, v_ref, do_ref, lse_ref, dl_ref, dq_ref, dk_ref, dv_ref, dk_s, dv_s):
        j = pl.program_id(1)
        i = pl.program_id(2)

        @pl.when(jnp.logical_and(j == 0, i == 0))
        def _():
            dq_ref[...] = jnp.zeros_like(dq_ref)

        @pl.when(i == 0)
        def _():
            dk_s[...] = jnp.zeros_like(dk_s)
            dv_s[...] = jnp.zeros_like(dv_s)

        qb = q_ref[0].astype(BF16)
        kb = k_ref[0].astype(BF16)
        dob = do_ref[0].astype(BF16)
        s = lax.dot_general(qb, kb, (((1,), (1,)), ((), ())), preferred_element_type=F32) * scale
        p = jnp.exp(s - lse_ref[0])
        dv_s[...] += lax.dot_general(p.astype(BF16), dob, (((0,), (0,)), ((), ())), preferred_element_type=F32)
        dp = lax.dot_general(dob, v_ref[0].astype(BF16), (((1,), (1,)), ((), ())), preferred_element_type=F32)
        ds = (p * (dp - dl_ref[0]) * scale).astype(BF16)
        dk_s[...] += lax.dot_general(ds, qb, (((0,), (0,)), ((), ())), preferred_element_type=F32)
        rows = pl.ds(pl.multiple_of(i * tq, tq), tq)
        dq_ref[0, rows, :] += jnp.dot(ds, kb, preferred_element_type=F32)

        @pl.when(i == nq - 1)
        def _():
            dk_ref[0] = dk_s[...]
            dv_ref[0] = dv_s[...]

    return pl.pallas_call(
        body,
        out_shape=(jax.ShapeDtypeStruct((H, L, DQ), F32), jax.ShapeDtypeStruct((H, L, DQ), F32),
                   jax.ShapeDtypeStruct((H, L, DV), F32)),
        grid=(H, L // tk, nq),
        in_specs=[pl.BlockSpec((1, tq, DQ), lambda h, j, i: (h, i, 0)),
                  pl.BlockSpec((1, tk, DQ), lambda h, j, i: (h, j, 0)),
                  pl.BlockSpec((1, tk, DV), lambda h, j, i: (h, j, 0)),
                  pl.BlockSpec((1, tq, DV), lambda h, j, i: (h, i, 0)),
                  pl.BlockSpec((1, tq, 1), lambda h, j, i: (h, i, 0)),
                  pl.BlockSpec((1, tq, 1), lambda h, j, i: (h, i, 0))],
        out_specs=(pl.BlockSpec((1, L, DQ), lambda h, j, i: (h, 0, 0)),
                   pl.BlockSpec((1, tk, DQ), lambda h, j, i: (h, j, 0)),
                   pl.BlockSpec((1, tk, DV), lambda h, j, i: (h, j, 0))),
        scratch_shapes=[pltpu.VMEM((tk, DQ), F32), pltpu.VMEM((tk, DV), F32)],
        compiler_params=_params(("arbitrary", "arbitrary", "arbitrary")),
        name="mla_attn_bwd",
    )(q, k, v, do, lse, delta)


@functools.partial(jax.custom_vjp, nondiff_argnums=(3,))
def full_attention(q, k, v, scale):
    return _attn_fwd_call(q, k, v, scale)[0]


def _full_attention_fwd(q, k, v, scale):
    o, lse = _attn_fwd_call(q, k, v, scale)
    return o, (q, k, v, o, lse)


def _full_attention_bwd(scale, res, do):
    q, k, v, o, lse = res
    delta = jnp.sum(do * o, axis=-1, keepdims=True)
    return _attn_bwd_call(q, k, v, do, lse, delta, scale)


full_attention.defvjp(_full_attention_fwd, _full_attention_bwd)


_NN = ((1,), (0,))
_NT = ((1,), (1,))
_TN = ((0,), (0,))
_DOT_VJP = {_NN: (("g", "b", _NT), ("a", "g", _TN)),
            _NT: (("g", "b", _NN), ("g", "a", _TN)),
            _TN: (("b", "g", _NT), ("a", "g", _NN))}


def _dot1(a, b, dims):
    return lax.dot_general(a.astype(BF16), b.astype(BF16), (dims, ((), ())), preferred_element_type=F32)


def _split_bf16(t):
    hi = t.astype(BF16)
    return hi, (t - hi.astype(F32)).astype(BF16)


def _dot3_raw(a, b, dims):
    ah, al = _split_bf16(a)
    bh, bl = _split_bf16(b)
    d = lambda p, q: lax.dot_general(p, q, (dims, ((), ())), preferred_element_type=F32)
    return d(ah, bh) + (d(ah, bl) + d(al, bh))


def _with_f32_cotangents(raw):
    @functools.partial(jax.custom_vjp, nondiff_argnums=(2,))
    def dot(a, b, dims):
        return raw(a, b, dims)

    def fwd(a, b, dims):
        return raw(a, b, dims), (a, b)

    def bwd(dims, res, g):
        a, b = res
        ops = {"a": a, "b": b, "g": g}
        (p, q, dp), (r, s, dr) = _DOT_VJP[dims]
        return raw(ops[p], ops[q], dp), raw(ops[r], ops[s], dr)

    dot.defvjp(fwd, bwd)
    return dot


def _halves(t, axis):
    n = t.shape[axis] // 2
    return lax.slice_in_dim(t, 0, n, axis=axis), lax.slice_in_dim(t, n, 2 * n, axis=axis)


@functools.partial(jax.custom_vjp, nondiff_argnums=(2,))
def _stack(a, b, axis):
    return jnp.concatenate([a, b], axis=axis)


_stack.defvjp(lambda a, b, axis: (jnp.concatenate([a, b], axis=axis), None),
              lambda axis, _, g: _halves(g, axis))


@functools.partial(jax.custom_vjp, nondiff_argnums=(1,))
def _unstack(t, axis):
    return _halves(t, axis)


_unstack.defvjp(lambda t, axis: (_halves(t, axis), None),
                lambda axis, _, g: (jnp.concatenate(list(g), axis=axis),))


_bdot = _with_f32_cotangents(_dot1)
_dot3 = _with_f32_cotangents(_dot3_raw)


def _swa_heads(q, k, v, bias, sink, valid):
    R = range(len(q))
    kv = [h // SWA_KV_GROUP for h in R]
    s = [_bdot(q[h], k[kv[h]], _NT) * (SWA_HEAD_DIM ** -0.5) + bias[h] for h in R]
    s = [jnp.where(valid, x, -1e30) for x in s]
    m = [lax.stop_gradient(jnp.maximum(jnp.max(s[h], axis=1, keepdims=True), sink[h])) for h in R]
    p = [jnp.exp(s[h] - m[h]) for h in R]
    den = [jnp.sum(p[h], axis=1, keepdims=True) + jnp.exp(sink[h] - m[h]) for h in R]
    return [_bdot(p[h] / den[h], v[kv[h]], _NN) for h in R]


def _swa_valid(n, L):
    qi = lax.broadcasted_iota(jnp.int32, (SWA_BLOCK, 3 * SWA_BLOCK), 0)
    sj = lax.broadcasted_iota(jnp.int32, (SWA_BLOCK, 3 * SWA_BLOCK), 1)
    rel = sj - SWA_BLOCK - qi
    kpos = n * SWA_BLOCK + sj - SWA_BLOCK
    return (jnp.abs(rel) <= WINDOW) & (kpos >= 0) & (kpos < L)


def _swa_specs(HQ, HKV, L):
    B = SWA_BLOCK
    q_spec = pl.BlockSpec((HQ, B, SWA_HEAD_DIM), lambda n: (0, n, 0))
    kv_spec = pl.BlockSpec((HKV, L + 2 * B, SWA_HEAD_DIM), lambda n: (0, 0, 0))
    bias_spec = pl.BlockSpec((HQ, B, 3 * B), lambda n: (0, 0, 0))
    sink_spec = pl.BlockSpec((HQ, B, 1), lambda n: (0, 0, 0))
    return q_spec, kv_spec, bias_spec, sink_spec


def _swa_fwd_call(q, kpad, vpad, bias, sink):
    HQ, L, _ = q.shape
    HKV = kpad.shape[0]
    B = SWA_BLOCK

    def body(q_ref, k_ref, v_ref, b_ref, s_ref, o_ref):
        n = pl.program_id(0)
        rows = pl.ds(pl.multiple_of(n * B, B), 3 * B)
        out = _swa_heads([q_ref[h] for h in range(HQ)], [k_ref[g, rows, :] for g in range(HKV)],
                         [v_ref[g, rows, :] for g in range(HKV)], [b_ref[h] for h in range(HQ)],
                         [s_ref[h] for h in range(HQ)], _swa_valid(n, L))
        for h in range(HQ):
            o_ref[h] = out[h]

    q_spec, kv_spec, bias_spec, sink_spec = _swa_specs(HQ, HKV, L)
    return pl.pallas_call(
        body,
        out_shape=jax.ShapeDtypeStruct(q.shape, F32),
        grid=(L // B,),
        in_specs=[q_spec, kv_spec, kv_spec, bias_spec, sink_spec],
        out_specs=q_spec,
        compiler_params=_params(("parallel",)),
        name="swa_fwd",
    )(q, kpad, vpad, bias, sink)


def _swa_bwd_call(q, kpad, vpad, bias, sink, do):
    HQ, L, _ = q.shape
    HKV = kpad.shape[0]
    B = SWA_BLOCK

    def body(q_ref, k_ref, v_ref, b_ref, s_ref, do_ref, dq_ref, dk_ref, dv_ref, db_ref, ds_ref):
        n = pl.program_id(0)

        @pl.when(n == 0)
        def _():
            dk_ref[...] = jnp.zeros_like(dk_ref)
            dv_ref[...] = jnp.zeros_like(dv_ref)
            db_ref[...] = jnp.zeros_like(db_ref)
            ds_ref[...] = jnp.zeros_like(ds_ref)

        rows = pl.ds(pl.multiple_of(n * B, B), 3 * B)
        _, vjp = jax.vjp(functools.partial(_swa_heads, valid=_swa_valid(n, L)),
                         [q_ref[h] for h in range(HQ)], [k_ref[g, rows, :] for g in range(HKV)],
                         [v_ref[g, rows, :] for g in range(HKV)], [b_ref[h] for h in range(HQ)],
                         [s_ref[h] for h in range(HQ)])
        dq, dk, dv, db, dsk = vjp([do_ref[h] for h in range(HQ)])
        for h in range(HQ):
            dq_ref[h] = dq[h]
            db_ref[h] += db[h]
            ds_ref[h] += dsk[h]
        for g in range(HKV):
            dk_ref[g, rows, :] += dk[g]
            dv_ref[g, rows, :] += dv[g]

    q_spec, kv_spec, bias_spec, sink_spec = _swa_specs(HQ, HKV, L)
    return pl.pallas_call(
        body,
        out_shape=(jax.ShapeDtypeStruct(q.shape, F32), jax.ShapeDtypeStruct(kpad.shape, F32),
                   jax.ShapeDtypeStruct(vpad.shape, F32), jax.ShapeDtypeStruct(bias.shape, F32),
                   jax.ShapeDtypeStruct(sink.shape, F32)),
        grid=(L // B,),
        in_specs=[q_spec, kv_spec, kv_spec, bias_spec, sink_spec, q_spec],
        out_specs=(q_spec, kv_spec, kv_spec, bias_spec, sink_spec),
        compiler_params=_params(("arbitrary",)),
        name="swa_bwd",
    )(q, kpad, vpad, bias, sink, do)


@jax.custom_vjp
def window_attention(q, kpad, vpad, bias, sink):
    return _swa_fwd_call(q, kpad, vpad, bias, sink)


def _window_attention_fwd(q, kpad, vpad, bias, sink):
    return _swa_fwd_call(q, kpad, vpad, bias, sink), (q, kpad, vpad, bias, sink)


def _window_attention_bwd(res, do):
    return _swa_bwd_call(*res, do)


window_attention.defvjp(_window_attention_fwd, _window_attention_bwd)


def _unit_tri_inverses(a):
    C = a[0].shape[0]
    ii = lax.broadcasted_iota(jnp.int32, (C, C), 0)
    jj = lax.broadcasted_iota(jnp.int32, (C, C), 1)
    pw = [-x for x in a]
    t = [jnp.where(ii == jj, 1.0, 0.0)] * len(a)
    for _ in range(int(math.log2(C))):
        both = [_dot3_raw(jnp.concatenate([ts, ps], axis=0), ps, _NN) for ts, ps in zip(t, pw)]
        t = [ts + b[:C] for ts, b in zip(t, both)]
        pw = [b[C:] for b in both]
    return t


@jax.custom_vjp
def _known_inverse(a, t):
    return t


def _known_inverse_fwd(a, t):
    return t, t


def _known_inverse_bwd(t, dt):
    da = -_dot3_raw(_dot3_raw(t, dt, _TN), t, _NT)
    return da, jnp.zeros_like(t)


_known_inverse.defvjp(_known_inverse_fwd, _known_inverse_bwd)


def _gdn_chunks(S, q, k, v, gc, gr, gl, beta, t_known, backwards):
    R = range(len(q))
    C = q[0].shape[0]
    ii = lax.broadcasted_iota(jnp.int32, (C, C), 0)
    jj = lax.broadcasted_iota(jnp.int32, (C, C), 1)
    causal = [(ii <= jj) if backwards[s] else (ii >= jj) for s in R]
    strict = [(ii < jj) if backwards[s] else (ii > jj) for s in R]
    decay = [jnp.where(causal[s], jnp.exp(jnp.where(causal[s], gc[s] - gr[s], 0.0)), 0.0) for s in R]
    kb = [k[s] * beta[s] for s in R]
    kk_qk = [_unstack(_bdot(_stack(kb[s], q[s], 0), k[s], _NT), 0) for s in R]
    a = [jnp.where(strict[s], kk_qk[s][0] * decay[s], 0.0) for s in R]
    t = _unit_tri_inverses(a) if t_known is None else [_known_inverse(a[s], t_known[s]) for s in R]
    u_w = [_unstack(_dot3(t[s], _stack(v[s] * beta[s], kb[s] * jnp.exp(gc[s]), 1), _NN), 1) for s in R]
    ws_qs = [_unstack(_bdot(_stack(u_w[s][1], q[s] * jnp.exp(gc[s]), 0), S[s], _NN), 0) for s in R]
    v_new = [u_w[s][0] - ws_qs[s][0] for s in R]
    o = [ws_qs[s][1] + _bdot(kk_qk[s][1] * decay[s], v_new[s], _NN) for s in R]
    s_new = [S[s] * jnp.exp(gl[s]) + _bdot(k[s] * jnp.exp(gl[s] - gc[s]), v_new[s], _TN) for s in R]
    return s_new, o, t


def _gdn_specs(H, L, C, Dh, flip):
    NC = L // C
    idx = (lambda c: NC - 1 - c) if flip else (lambda c: c)
    seq = pl.BlockSpec((C, H * Dh), lambda c: (idx(c), 0))

    def scalar(rows, cols, group):
        return pl.BlockSpec((H, 1, rows, cols), lambda c: (group, idx(c), 0, 0))

    return seq, scalar


def _gdn_fwd_call(q, k, v, gc, gr, gl, beta):
    L, HD = q.shape
    C = gc.shape[2]
    NC = L // C
    H = gc.shape[0] // 2
    Dh = HD // H

    def body(qf, kf, vf, qb, kb, vb, gcf, grf, glf, bf, gcb, grb, glb, bb,
             of_ref, ob_ref, stf_ref, stb_ref, tf_ref, tb_ref, s_scr):
        @pl.when(pl.program_id(0) == 0)
        def _():
            s_scr[...] = jnp.zeros_like(s_scr)

        groups = ((qf, kf, vf, gcf, grf, glf, bf), (qb, kb, vb, gcb, grb, glb, bb))
        outs = ((of_ref, stf_ref, tf_ref), (ob_ref, stb_ref, tb_ref))
        seqs = [(d, h) for d in range(2) for h in range(H)]
        cols = [slice(h * Dh, (h + 1) * Dh) for h in range(H)]
        s0 = [s_scr[d * H + h] for d, h in seqs]
        for (d, h), s in zip(seqs, s0):
            outs[d][1][h, 0] = s
        seq_in = lambda j: [groups[d][j][:, cols[h]] for d, h in seqs]
        scal_in = lambda j: [groups[d][j][h, 0] for d, h in seqs]
        s1, o, t = _gdn_chunks(s0, seq_in(0), seq_in(1), seq_in(2), scal_in(3), scal_in(4), scal_in(5), scal_in(6),
                               None, [d == 1 for d, _ in seqs])
        for i, (d, h) in enumerate(seqs):
            s_scr[d * H + h] = s1[i]
            outs[d][0][:, cols[h]] = o[i]
            outs[d][2][h, 0] = t[i]

    seq_f, sc_f = _gdn_specs(H, L, C, Dh, False)
    seq_b, sc_b = _gdn_specs(H, L, C, Dh, True)
    sds = jax.ShapeDtypeStruct
    return pl.pallas_call(
        body,
        out_shape=(sds((L, HD), F32), sds((L, HD), F32), sds((H, NC, Dh, Dh), F32), sds((H, NC, Dh, Dh), F32),
                   sds((H, NC, C, C), F32), sds((H, NC, C, C), F32)),
        grid=(NC,),
        in_specs=[seq_f, seq_f, seq_f, seq_b, seq_b, seq_b,
                  sc_f(C, 1, 0), sc_f(1, C, 0), sc_f(1, 1, 0), sc_f(C, 1, 0),
                  sc_b(C, 1, 1), sc_b(1, C, 1), sc_b(1, 1, 1), sc_b(C, 1, 1)],
        out_specs=(seq_f, seq_b, sc_f(Dh, Dh, 0), sc_b(Dh, Dh, 0), sc_f(C, C, 0), sc_b(C, C, 0)),
        scratch_shapes=[pltpu.VMEM((2 * H, Dh, Dh), F32)],
        compiler_params=_params(("arbitrary",)),
        name="gdn_fwd",
    )(q, k, v, q, k, v, gc, gr, gl, beta, gc, gr, gl, beta)


def _gdn_bwd_call(q, k, v, gc, gr, gl, beta, st_f, st_b, t_f, t_b, do_f, do_b):
    L, HD = q.shape
    C = gc.shape[2]
    NC = L // C
    H = gc.shape[0] // 2
    Dh = HD // H

    def body(qf, kf, vf, qb, kb, vb, gcf, grf, glf, bf, gcb, grb, glb, bb, stf, stb, tf, tb, dof, dob,
             dqf, dkf, dvf, dqb, dkb, dvb, dgcf, dgrf, dglf, dbf, dgcb, dgrb, dglb, dbb, ds_scr):
        @pl.when(pl.program_id(0) == 0)
        def _():
            ds_scr[...] = jnp.zeros_like(ds_scr)

        groups = ((qf, kf, vf, gcf, grf, glf, bf, stf, tf, dof), (qb, kb, vb, gcb, grb, glb, bb, stb, tb, dob))
        outs = ((dqf, dkf, dvf, dgcf, dgrf, dglf, dbf), (dqb, dkb, dvb, dgcb, dgrb, dglb, dbb))
        seqs = [(d, h) for d in range(2) for h in range(H)]
        cols = [slice(h * Dh, (h + 1) * Dh) for h in range(H)]
        seq_in = lambda j: [groups[d][j][:, cols[h]] for d, h in seqs]
        scal_in = lambda j: [groups[d][j][h, 0] for d, h in seqs]
        t_known = scal_in(8)
        backwards = [d == 1 for d, _ in seqs]

        def chunks(*args):
            return _gdn_chunks(*args, t_known, backwards)[:2]

        _, vjp = jax.vjp(chunks, scal_in(7), seq_in(0), seq_in(1), seq_in(2), scal_in(3), scal_in(4), scal_in(5),
                         scal_in(6))
        grads = vjp(([ds_scr[d * H + h] for d, h in seqs], seq_in(9)))
        for i, (d, h) in enumerate(seqs):
            ds_scr[d * H + h] = grads[0][i]
            for j in range(3):
                outs[d][j][:, cols[h]] = grads[1 + j][i]
            for j in range(3, 7):
                outs[d][j][h, 0] = grads[1 + j][i]

    seq_f, sc_f = _gdn_specs(H, L, C, Dh, True)
    seq_b, sc_b = _gdn_specs(H, L, C, Dh, False)
    sds = jax.ShapeDtypeStruct
    seq_out = sds((L, HD), F32)
    half = lambda t: sds((H,) + t.shape[1:], F32)
    scal_f = [sc_f(C, 1, 0), sc_f(1, C, 0), sc_f(1, 1, 0), sc_f(C, 1, 0)]
    scal_b = [sc_b(C, 1, 1), sc_b(1, C, 1), sc_b(1, 1, 1), sc_b(C, 1, 1)]
    scal_b_out = [sc_b(C, 1, 0), sc_b(1, C, 0), sc_b(1, 1, 0), sc_b(C, 1, 0)]
    return pl.pallas_call(
        body,
        out_shape=(seq_out,) * 6 + (half(gc), half(gr), half(gl), half(beta)) * 2,
        grid=(NC,),
        in_specs=[seq_f, seq_f, seq_f, seq_b, seq_b, seq_b] + scal_f + scal_b
                 + [sc_f(Dh, Dh, 0), sc_b(Dh, Dh, 0), sc_f(C, C, 0), sc_b(C, C, 0), seq_f, seq_b],
        out_specs=[seq_f, seq_f, seq_f, seq_b, seq_b, seq_b] + scal_f + scal_b_out,
        scratch_shapes=[pltpu.VMEM((2 * H, Dh, Dh), F32)],
        compiler_params=_params(("arbitrary",)),
        name="gdn_bwd",
    )(q, k, v, q, k, v, gc, gr, gl, beta, gc, gr, gl, beta, st_f, st_b, t_f, t_b, do_f, do_b)


@jax.custom_vjp
def gated_delta_rule(q, k, v, gc, gr, gl, beta):
    return _gdn_fwd_call(q, k, v, gc, gr, gl, beta)[:2]


def _gated_delta_rule_fwd(q, k, v, gc, gr, gl, beta):
    o_f, o_b, st_f, st_b, t_f, t_b = _gdn_fwd_call(q, k, v, gc, gr, gl, beta)
    return (o_f, o_b), (q, k, v, gc, gr, gl, beta, st_f, st_b, t_f, t_b)


def _gated_delta_rule_bwd(res, do):
    (dqf, dkf, dvf, dqb, dkb, dvb, dgcf, dgrf, dglf, dbf, dgcb, dgrb, dglb, dbb) = _gdn_bwd_call(*res, *do)
    cat = lambda a, b: jnp.concatenate([a, b], axis=0)
    return dqf + dqb, dkf + dkb, dvf + dvb, cat(dgcf, dgcb), cat(dgrf, dgrb), cat(dglf, dglb), cat(dbf, dbb)


gated_delta_rule.defvjp(_gated_delta_rule_fwd, _gated_delta_rule_bwd)


GATE_ROWS = 128


def _gate_specs(L, D, nb):
    tm = _tile(L, GATE_ROWS, SUBLANES)
    logit_specs = [pl.BlockSpec((tm, D), functools.partial(lambda i, b: (i, b), b=b)) for b in range(nb)]
    row_spec = pl.BlockSpec((tm, D), lambda i: (i, 0))
    return tm, logit_specs, row_spec


def _gate_fwd_call(logits, branches):
    nb = len(branches)
    L, D = branches[0].shape
    tm, logit_specs, row_spec = _gate_specs(L, D, nb)

    def body(*refs):
        o_ref = refs[2 * nb]
        acc = jax.nn.sigmoid(refs[0][...]) * refs[nb][...]
        for b in range(1, nb):
            acc = acc + jax.nn.sigmoid(refs[b][...]) * refs[nb + b][...]
        o_ref[...] = acc

    return pl.pallas_call(
        body, out_shape=jax.ShapeDtypeStruct((L, D), F32), grid=(L // tm,),
        in_specs=logit_specs + [row_spec] * nb, out_specs=row_spec,
        compiler_params=_params(("parallel",)), name="gate_merge_fwd",
    )(*([logits] * nb), *branches)


def _gate_bwd_call(logits, branches, dm):
    nb = len(branches)
    L, D = branches[0].shape
    tm, logit_specs, row_spec = _gate_specs(L, D, nb)

    def body(*refs):
        dm_v = refs[2 * nb][...]
        dl_ref = refs[2 * nb + 1]
        for b in range(nb):
            sig = jax.nn.sigmoid(refs[b][...])
            refs[2 * nb + 2 + b][...] = dm_v * sig
            dl_ref[:, b * D:(b + 1) * D] = dm_v * refs[nb + b][...] * (sig * (1.0 - sig))

    return pl.pallas_call(
        body,
        out_shape=[jax.ShapeDtypeStruct(logits.shape, F32)] + [jax.ShapeDtypeStruct((L, D), F32)] * nb,
        grid=(L // tm,),
        in_specs=logit_specs + [row_spec] * (nb + 1),
        out_specs=[pl.BlockSpec((tm, nb * D), lambda i: (i, 0))] + [row_spec] * nb,
        compiler_params=_params(("parallel",)), name="gate_merge_bwd",
    )(*([logits] * nb), *branches, dm)


@jax.custom_vjp
def gate_merge(logits, branches):
    return _gate_fwd_call(logits, branches)


def _gate_merge_fwd(logits, branches):
    return _gate_fwd_call(logits, branches), (logits, branches)


def _gate_merge_bwd(res, dm):
    logits, branches = res
    out = _gate_bwd_call(logits, branches, dm)
    return out[0], tuple(out[1:])


gate_merge.defvjp(_gate_merge_fwd, _gate_merge_bwd)


def _loss_fwd_call(y, t):
    L, D = y.shape
    tm = _rms_rows(L, D)

    def body(y_ref, t_ref, o_ref):
        @pl.when(pl.program_id(0) == 0)
        def _():
            o_ref[...] = jnp.zeros_like(o_ref)

        e = y_ref[...] - t_ref[...]
        part = jnp.sum(jnp.sum(e * e, axis=1, keepdims=True), axis=0, keepdims=True)
        o_ref[...] += part * (0.5 / D)

    out = pl.pallas_call(
        body,
        out_shape=jax.ShapeDtypeStruct((SUBLANES, LANES), F32),
        grid=(L // tm,),
        in_specs=[pl.BlockSpec((tm, D), lambda i: (i, 0)), pl.BlockSpec((tm, D), lambda i: (i, 0))],
        out_specs=pl.BlockSpec((SUBLANES, LANES), lambda i: (0, 0)),
        compiler_params=_params(("arbitrary",)),
        name="loss_fwd",
    )(y, t)
    return out[0, 0]


def _loss_bwd_call(y, t, g):
    L, D = y.shape
    tm = _rms_rows(L, D)

    def body(y_ref, t_ref, g_ref, o_ref):
        o_ref[...] = (y_ref[...] - t_ref[...]) * (g_ref[...] * (1.0 / D))

    return pl.pallas_call(
        body,
        out_shape=jax.ShapeDtypeStruct((L, D), F32),
        grid=(L // tm,),
        in_specs=[pl.BlockSpec((tm, D), lambda i: (i, 0)), pl.BlockSpec((tm, D), lambda i: (i, 0)),
                  pl.BlockSpec((1, 1), lambda i: (0, 0))],
        out_specs=pl.BlockSpec((tm, D), lambda i: (i, 0)),
        compiler_params=_params(("parallel",)),
        name="loss_bwd",
    )(y, t, g.reshape(1, 1))


@jax.custom_vjp
def loss_head(y, t):
    return _loss_fwd_call(y, t)


def _loss_head_fwd(y, t):
    return _loss_fwd_call(y, t), (y, t)


def _loss_head_bwd(res, g):
    y, t = res
    return _loss_bwd_call(y, t, g), jnp.zeros_like(t)


loss_head.defvjp(_loss_head_fwd, _loss_head_bwd)


def _rows_for(C, nbuf):
    return max(16, ((24 << 20) // (nbuf * 4 * C)) // 16 * 16)


def _pair_add_call(a, b, name):
    R, C = a.shape
    tm = _tile(R, _rows_for(C, 6), 16)

    def body(a_ref, b_ref, o_ref):
        o_ref[...] = (a_ref[...].astype(F32) + b_ref[...].astype(F32)).astype(o_ref.dtype)

    spec = pl.BlockSpec((tm, C), lambda i: (i, 0))
    return pl.pallas_call(
        body, out_shape=jax.ShapeDtypeStruct(a.shape, a.dtype), grid=(R // tm,),
        in_specs=[spec, spec], out_specs=spec, compiler_params=_params(("parallel",)), name=name,
    )(a, b)


def _adamw_call(parts, w, m, v, name):
    P, R, C = parts.shape
    tm = _tile(R, _rows_for(C, 2 * (P + 7)), 16)

    def body(p_ref, w_ref, m_ref, v_ref, g_ref, d_ref, mo_ref, vo_ref):
        g = p_ref[0].astype(F32)
        for s in range(1, P):
            g = g + p_ref[s].astype(F32)
        m2 = ADAM_B1 * m_ref[...] + (1.0 - ADAM_B1) * g
        v2 = ADAM_B2 * v_ref[...] + (1.0 - ADAM_B2) * jnp.square(g)
        m_hat = m2 / (1.0 - ADAM_B1 ** ADAM_STEP)
        v_hat = v2 / (1.0 - ADAM_B2 ** ADAM_STEP)
        g_ref[...] = g
        d_ref[...] = -ADAM_LR * (m_hat / (jnp.sqrt(v_hat) + ADAM_EPS) + ADAM_WD * w_ref[...])
        mo_ref[...] = m2
        vo_ref[...] = v2

    spec = pl.BlockSpec((tm, C), lambda i: (i, 0))
    out = jax.ShapeDtypeStruct((R, C), F32)
    return pl.pallas_call(
        body, out_shape=(out, out, out, out), grid=(R // tm,),
        in_specs=[pl.BlockSpec((P, tm, C), lambda i: (0, i, 0)), spec, spec, spec],
        out_specs=(spec, spec, spec, spec), compiler_params=_params(("parallel",)), name=name,
    )(parts, w, m, v)


_MESH = pl.DeviceIdType.MESH
FLIPS_CHIPS = ((1, 0, 0), (0, 1, 0), (1, 1, 0))
FLIPS_ALL = ((0, 0, 1), (1, 0, 0), (0, 1, 0), (1, 1, 0), (1, 0, 1), (0, 1, 1), (1, 1, 1))
FLIPS_SIBLING = ((0, 0, 1),)


def _exchange(arrays, flips, mode, name):
    n = len(arrays)
    nf = len(flips)
    over_c = any(f[2] for f in flips)
    n_slots = 8 if over_c else 4

    def slot(x, y, c):
        return 4 * x + 2 * y + c if over_c else 2 * x + y

    def body(*refs):
        ins, outs = refs[:n], refs[n:2 * n]
        send_sems, recv_sems, local_sems = refs[2 * n:]
        x, y, c = lax.axis_index("x"), lax.axis_index("y"), lax.axis_index("c")
        me = slot(x, y, c)
        peers = [(x + f[0] - 2 * x * f[0], y + f[1] - 2 * y * f[1], c + f[2] - 2 * c * f[2]) for f in flips]

        def copy(i, k, sending):
            px, py, pc = peers[k]
            there = slot(px, py, pc)
            if mode == "swap":
                src, dst = ins[i], outs[i]
            elif mode == "gather":
                src, dst = ins[i], outs[i].at[me if sending else there]
            else:
                src, dst = ins[i].at[there if sending else me], outs[i].at[me if sending else there]
            return pltpu.make_async_remote_copy(src_ref=src, dst_ref=dst, send_sem=send_sems.at[i, k],
                                                recv_sem=recv_sems.at[i, k], device_id=(px, py, pc),
                                                device_id_type=_MESH)

        sends = [copy(i, k, True) for i in range(n) for k in range(nf)]
        for cp in sends:
            cp.start()
        local = []
        if mode != "swap":
            for i in range(n):
                src = ins[i] if mode == "gather" else ins[i].at[me]
                cp = pltpu.make_async_copy(src, outs[i].at[me], local_sems.at[i])
                cp.start()
                local.append(cp)
        for i in range(n):
            for k in range(nf):
                copy(i, k, False).wait_recv()
        for cp in sends:
            cp.wait_send()
        for cp in local:
            cp.wait()

    if mode == "gather":
        out_shape = [jax.ShapeDtypeStruct((n_slots,) + a.shape, a.dtype) for a in arrays]
    else:
        out_shape = [jax.ShapeDtypeStruct(a.shape, a.dtype) for a in arrays]
    any_spec = pl.BlockSpec(memory_space=pl.ANY)
    return pl.pallas_call(
        body,
        out_shape=out_shape,
        in_specs=[any_spec] * n,
        out_specs=[any_spec] * n,
        scratch_shapes=[pltpu.SemaphoreType.DMA((n, nf)), pltpu.SemaphoreType.DMA((n, nf)),
                        pltpu.SemaphoreType.DMA((n,))],
        compiler_params=pltpu.CompilerParams(has_side_effects=True),
        name=name,
    )(*arrays)


WEIGHT_NAMES = ('w_in', 's5_lam_re', 's5_lam_im', 's5_log_step', 's5_b_re', 's5_b_im', 's5_c_re', 's5_c_im', 's5_d',
                's5_w_glu', 's5_b_glu', 'gdn_conv', 'gdn_a_log', 'gdn_dt_bias', 'gdn_o_gain', 'swa_sink', 't5_bias',
                'mla_q_gain', 'mla_kv_gain', 'mla_w_uq', 'mla_w_ukv', 'w_branch', 'w_out', 'mix_pre_gain',
                'mix_post_gain', 'mlp_pre_gain', 'mlp_post_gain', 'w_mlp_in', 'w_mlp_out')
SHARDED = {'w_in': (2, BF16), 's5_w_glu': (1, BF16), 'gdn_conv': (2, F32), 'mla_w_uq': (2, BF16),
           'mla_w_ukv': (2, BF16), 'w_branch': (3, BF16), 'w_out': (1, BF16), 'w_mlp_in': (2, BF16),
           'w_mlp_out': (1, BF16)}
REPLICATED = tuple(n for n in WEIGHT_NAMES if n not in SHARDED)
PACK_COLS = 1024

_IN_A = (('s5_u', 512), ('gdn_qkv', 1536), ('gdn_z', 512), ('gdn_beta', 8), ('gdn_decay', 8))
_IN_B = (('swa_q', 512), ('swa_kv', 256), ('mla_cq', 384), ('mla_ckv', 512), ('mla_kr', 64))
_IN_A_W = sum(w for _, w in _IN_A)
_IN_B_W = sum(w for _, w in _IN_B)
_IN_A_PAD = -_IN_A_W % LANES
_IN_B_PAD = -(_IN_A_W + _IN_A_PAD + _IN_B_W) % 512


def _assemble(g, axis):
    t = jnp.moveaxis(g, 0, axis)
    shape = t.shape[:axis] + (t.shape[axis] * t.shape[axis + 1],) + t.shape[axis + 2:]
    return t.reshape(shape)


def _s5_mixer(u, lam_re, lam_im, log_step, b_re, b_im, c_re, c_im, d_skip, w_glu, b_glu):
    nd, G, P = lam_re.shape
    Hg = b_re.shape[-1]
    lam_re = jnp.minimum(lam_re, -1e-4)
    dt = jnp.exp(log_step)[..., None]
    mag = jnp.exp(lam_re * dt)
    abar_r = mag * jnp.cos(lam_im * dt)
    abar_i = mag * jnp.sin(lam_im * dt)
    den = lam_re * lam_re + lam_im * lam_im
    xr = abar_r - 1.0
    xi = abar_i
    coef_r = (xr * lam_re + xi * lam_im) / den
    coef_i = (xi * lam_re - xr * lam_im) / den
    bbar_r = coef_r[..., None] * b_re - coef_i[..., None] * b_im
    bbar_i = coef_r[..., None] * b_im + coef_i[..., None] * b_re
    eye = jnp.eye(G, dtype=F32)

    def dense_b(bb):
        return jnp.einsum('dgph,gk->ghdkp', bb, eye).reshape(G * Hg, nd, G * P)

    def dense_c(cc):
        return jnp.einsum('dghp,gk->khdgp', cc, eye).reshape(G * Hg, nd, G * P)

    b_cat = _to_scan_cols(jnp.stack([dense_b(bbar_r), dense_b(bbar_i)], axis=2), nd)
    c_cat = _to_scan_cols(jnp.stack([dense_c(c_re), -dense_c(c_im)], axis=2), nd).T
    a_row = _to_scan_cols(jnp.stack([abar_r.reshape(nd, G * P), abar_i.reshape(nd, G * P)], axis=1), nd)[None]
    s = s5_scan(linear(u, b_cat, 's5_in'), a_row, nd)
    y = linear(s, c_cat, 's5_out') + d_skip * u
    y = jax.nn.gelu(y)
    return y * jax.nn.sigmoid(linear(y, w_glu, 's5_glu') + b_glu)


def _gdn_mixer(qkv, z, beta_logits, decay_logits, conv_w, a_log, dt_bias, o_gain):
    L = qkv.shape[0]
    Dh = GDN_HEAD_DIM
    H = z.shape[1] // Dh
    C = GDN_CHUNK
    NC = L // C
    xp = jnp.pad(qkv, ((GDN_CONV // 2, GDN_CONV - 1 - GDN_CONV // 2), (0, 0)))
    conv = xp[0:L] * conv_w[0]
    for j in range(1, GDN_CONV):
        conv = conv + xp[j:j + L] * conv_w[j]
    q, k, v = jnp.split(jax.nn.silu(conv), 3, axis=-1)

    def l2n(t):
        return t * lax.rsqrt(jnp.sum(t * t, axis=-1, keepdims=True) + 1e-6)

    q = (l2n(q.reshape(L, H, Dh)) * (Dh ** -0.5)).reshape(L, H * Dh)
    k = l2n(k.reshape(L, H, Dh)).reshape(L, H * Dh)
    beta = jax.nn.sigmoid(beta_logits).reshape(L, 2, H)
    g = -jnp.exp(a_log) * jax.nn.softplus(decay_logits.reshape(L, 2, H) + dt_bias)

    def per_chunk(t, d):
        return t[:, d].T.reshape(H, NC, C)

    g_with = jnp.cumsum(per_chunk(g, 0), axis=-1)
    g_against = jnp.cumsum(per_chunk(g, 1)[..., ::-1], axis=-1)[..., ::-1]
    gs = jnp.concatenate([g_with, g_against], axis=0)
    totals = jnp.concatenate([g_with[..., -1], g_against[..., 0]], axis=0)
    betas = jnp.concatenate([per_chunk(beta, 0), per_chunk(beta, 1)], axis=0)
    o_with, o_against = gated_delta_rule(q, k, v, gs[..., None], gs[:, :, None, :], totals[..., None, None],
                                         betas[..., None])
    o = o_with + o_against
    o = rms_norm(o.reshape(L * H, Dh), o_gain, 'gdn_onorm').reshape(L, H, Dh)
    o = o * jax.nn.silu(z.reshape(L, H, Dh))
    return o.reshape(L, H * Dh)


def _t5_bucket(rel):
    nb = T5_BUCKETS // 2
    max_exact = nb // 2
    ret = jnp.where(rel > 0, nb, 0)
    n = jnp.abs(rel)
    nf = jnp.maximum(n, 1).astype(F32)
    large = max_exact + (jnp.log(nf / max_exact) / math.log(T5_MAX_DISTANCE / max_exact)
                         * (nb - max_exact)).astype(jnp.int32)
    large = jnp.minimum(large, nb - 1)
    return ret + jnp.where(n < max_exact, n, large)


def _swa_mixer(q, kv, sink, t5_bias):
    L = q.shape[0]
    B = SWA_BLOCK
    HQ = q.shape[1] // SWA_HEAD_DIM
    HKV = HQ // SWA_KV_GROUP
    qh = q.reshape(L, HQ, SWA_HEAD_DIM).transpose(1, 0, 2)
    k, v = jnp.split(kv, 2, axis=-1)

    def heads_padded(t):
        return jnp.pad(t.reshape(L, HKV, SWA_HEAD_DIM).transpose(1, 0, 2), ((0, 0), (B, B), (0, 0)))

    qi = jnp.arange(B)[:, None]
    sj = jnp.arange(3 * B)[None, :]
    one_hot = (_t5_bucket(sj - B - qi)[..., None] == jnp.arange(T5_BUCKETS)).astype(F32)
    bias = jnp.einsum('qsb,bh->hqs', one_hot, t5_bias, precision=_HIGHEST)
    sink_rows = jnp.broadcast_to(sink[:, None, None], (HQ, B, 1))
    o = window_attention(qh, heads_padded(k), heads_padded(v), bias, sink_rows)
    return o.transpose(1, 0, 2).reshape(L, HQ * SWA_HEAD_DIM)


def _apply_rope(x, cos, sin):
    x1, x2 = jnp.split(x, 2, axis=-1)
    return jnp.concatenate([x1 * cos - x2 * sin, x2 * cos + x1 * sin], axis=-1)


def _mla_mixer(c_q, c_kv, k_rope, q_gain, kv_gain, w_uq, w_ukv):
    L = c_q.shape[0]
    H = w_uq.shape[1] // (MLA_NOPE + MLA_ROPE)
    q = linear(rms_norm(c_q, q_gain, 'mla_qnorm'), w_uq, 'mla_uq').reshape(L, H, MLA_NOPE + MLA_ROPE)
    kv = linear(rms_norm(c_kv, kv_gain, 'mla_kvnorm'), w_ukv, 'mla_ukv').reshape(L, H, MLA_NOPE + MLA_V)
    q_nope, q_pe = q[..., :MLA_NOPE], q[..., MLA_NOPE:]
    k_nope, v = kv[..., :MLA_NOPE], kv[..., MLA_NOPE:]
    pos = jnp.arange(L, dtype=F32)
    inv_freq = ROPE_THETA ** (-jnp.arange(0, MLA_ROPE, 2, dtype=F32) / MLA_ROPE)
    ang = pos[:, None] * inv_freq[None, :]
    cos, sin = jnp.cos(ang)[:, None, :], jnp.sin(ang)[:, None, :]
    q_pe = _apply_rope(q_pe, cos, sin)
    k_pe = _apply_rope(k_rope[:, None, :], cos, sin)
    qf = jnp.concatenate([q_nope, q_pe], axis=-1).transpose(1, 0, 2)
    kf = jnp.concatenate([k_nope, jnp.broadcast_to(k_pe, (L, H, MLA_ROPE))], axis=-1).transpose(1, 0, 2)
    o = full_attention(qf, kf, v.transpose(1, 0, 2), (MLA_NOPE + MLA_ROPE) ** -0.5)
    return o.transpose(1, 0, 2).reshape(L, H * MLA_V)


def _split_cols(t, segments, start):
    out = {}
    for name, width in segments:
        out[name] = t[:, start:start + width]
        start += width
    return out, start


def _local_loss(weights, x, target):
    p = {n: (_assemble(weights[n], SHARDED[n][0]) if n in SHARDED else weights[n]) for n in WEIGHT_NAMES}
    L, D = x.shape
    depth = p['w_in'].shape[0]
    for l in range(depth):
        w_in = p['w_in'][l]
        zeros = lambda n: jnp.zeros((D, n), w_in.dtype)
        w_r = jnp.concatenate([w_in[:, :_IN_A_W], zeros(_IN_A_PAD), w_in[:, _IN_A_W:_IN_A_W + _IN_B_W],
                               zeros(_IN_B_PAD)], axis=1)
        w_g = w_in[:, _IN_A_W + _IN_B_W:]
        h = rms_norm(x, p['mix_pre_gain'][l], 'mix_pre')
        proj = linear(h, w_r, 'in_proj')
        gate_logits = linear(h, w_g, 'in_gate')
        seg, end = _split_cols(proj, _IN_A, 0)
        seg_b, _ = _split_cols(proj, _IN_B, end + _IN_A_PAD)
        seg.update(seg_b)
        y_a = _s5_mixer(seg['s5_u'], p['s5_lam_re'][l], p['s5_lam_im'][l], p['s5_log_step'][l], p['s5_b_re'][l],
                        p['s5_b_im'][l], p['s5_c_re'][l], p['s5_c_im'][l], p['s5_d'][l], p['s5_w_glu'][l],
                        p['s5_b_glu'][l])
        y_b = _gdn_mixer(seg['gdn_qkv'], seg['gdn_z'], seg['gdn_beta'], seg['gdn_decay'], p['gdn_conv'][l],
                         p['gdn_a_log'][l], p['gdn_dt_bias'][l], p['gdn_o_gain'][l])
        y_c = _swa_mixer(seg['swa_q'], seg['swa_kv'], p['swa_sink'][l], p['t5_bias'])
        y_d = _mla_mixer(seg['mla_cq'], seg['mla_ckv'], seg['mla_kr'], p['mla_q_gain'][l], p['mla_kv_gain'][l],
                         p['mla_w_uq'][l], p['mla_w_ukv'][l])
        branches = tuple(linear(y, p['w_branch'][l, b], 'branch') for b, y in enumerate((y_a, y_b, y_c, y_d)))
        merged = gate_merge(gate_logits, branches)
        x = x + rms_norm(linear(merged, p['w_out'][l], 'mix_out'), p['mix_post_gain'][l], 'mix_post')
        h = rms_norm(x, p['mlp_pre_gain'][l], 'mlp_pre')
        f = linear(jnp.square(jax.nn.relu(linear(h, p['w_mlp_in'][l], 'mlp_in'))), p['w_mlp_out'][l], 'mlp_out')
        x = x + rms_norm(f, p['mlp_post_gain'][l], 'mlp_post')
    return loss_head(x, target)


def _two_d(t, lead):
    return t.reshape(t.shape[:lead] + (-1, t.shape[-1]))


def _pack(arrays):
    flat = jnp.concatenate([a.reshape(-1) for a in arrays])
    pad = -flat.shape[0] % (16 * PACK_COLS)
    return jnp.pad(flat, (0, pad)).reshape(-1, PACK_COLS)


def _unpack(packed, like):
    flat = packed.reshape(-1)
    out, pos = [], 0
    for a in like:
        out.append(flat[pos:pos + a.size].reshape(a.shape))
        pos += a.size
    return out


def kernel(x, w_in, s5_lam_re, s5_lam_im, s5_log_step, s5_b_re, s5_b_im, s5_c_re, s5_c_im, s5_d, s5_w_glu, s5_b_glu, gdn_conv, gdn_a_log, gdn_dt_bias, gdn_o_gain, swa_sink, t5_bias, mla_q_gain, mla_kv_gain, mla_w_uq, mla_w_ukv, w_branch, w_out, mix_pre_gain, mix_post_gain, mlp_pre_gain, mlp_post_gain, w_mlp_in, w_mlp_out, loss_target, m_w_in, m_s5_lam_re, m_s5_lam_im, m_s5_log_step, m_s5_b_re, m_s5_b_im, m_s5_c_re, m_s5_c_im, m_s5_d, m_s5_w_glu, m_s5_b_glu, m_gdn_conv, m_gdn_a_log, m_gdn_dt_bias, m_gdn_o_gain, m_swa_sink, m_t5_bias, m_mla_q_gain, m_mla_kv_gain, m_mla_w_uq, m_mla_w_ukv, m_w_branch, m_w_out, m_mix_pre_gain, m_mix_post_gain, m_mlp_pre_gain, m_mlp_post_gain, m_w_mlp_in, m_w_mlp_out, v_w_in, v_s5_lam_re, v_s5_lam_im, v_s5_log_step, v_s5_b_re, v_s5_b_im, v_s5_c_re, v_s5_c_im, v_s5_d, v_s5_w_glu, v_s5_b_glu, v_gdn_conv, v_gdn_a_log, v_gdn_dt_bias, v_gdn_o_gain, v_swa_sink, v_t5_bias, v_mla_q_gain, v_mla_kv_gain, v_mla_w_uq, v_mla_w_ukv, v_w_branch, v_w_out, v_mix_pre_gain, v_mix_post_gain, v_mlp_pre_gain, v_mlp_post_gain, v_w_mlp_in, v_w_mlp_out):
    w = dict(zip(WEIGHT_NAMES, (w_in, s5_lam_re, s5_lam_im, s5_log_step, s5_b_re, s5_b_im, s5_c_re, s5_c_im, s5_d, s5_w_glu, s5_b_glu, gdn_conv, gdn_a_log, gdn_dt_bias, gdn_o_gain, swa_sink, t5_bias, mla_q_gain, mla_kv_gain, mla_w_uq, mla_w_ukv, w_branch, w_out, mix_pre_gain, mix_post_gain, mlp_pre_gain, mlp_post_gain, w_mlp_in, w_mlp_out)))
    m = dict(zip(WEIGHT_NAMES, (m_w_in, m_s5_lam_re, m_s5_lam_im, m_s5_log_step, m_s5_b_re, m_s5_b_im, m_s5_c_re, m_s5_c_im, m_s5_d, m_s5_w_glu, m_s5_b_glu, m_gdn_conv, m_gdn_a_log, m_gdn_dt_bias, m_gdn_o_gain, m_swa_sink, m_t5_bias, m_mla_q_gain, m_mla_kv_gain, m_mla_w_uq, m_mla_w_ukv, m_w_branch, m_w_out, m_mix_pre_gain, m_mix_post_gain, m_mlp_pre_gain, m_mlp_post_gain, m_w_mlp_in, m_w_mlp_out)))
    v = dict(zip(WEIGHT_NAMES, (v_w_in, v_s5_lam_re, v_s5_lam_im, v_s5_log_step, v_s5_b_re, v_s5_b_im, v_s5_c_re, v_s5_c_im, v_s5_d, v_s5_w_glu, v_s5_b_glu, v_gdn_conv, v_gdn_a_log, v_gdn_dt_bias, v_gdn_o_gain, v_swa_sink, v_t5_bias, v_mla_q_gain, v_mla_kv_gain, v_mla_w_uq, v_mla_w_ukv, v_w_branch, v_w_out, v_mix_pre_gain, v_mix_post_gain, v_mlp_pre_gain, v_mlp_post_gain, v_w_mlp_in, v_w_mlp_out)))
    sharded = tuple(SHARDED)

    gathered = _exchange([w[n].astype(SHARDED[n][1]) for n in sharded], FLIPS_CHIPS, 'gather', 'weights_all_gather')
    weights = dict(zip(sharded, gathered))
    weights.update({n: w[n] for n in REPLICATED})

    loss, (grads, grad_x) = jax.value_and_grad(_local_loss, argnums=(0, 1))(weights, x[0], loss_target[0])
    loss = lax.psum(loss, ('x', 'y', 'c'))

    from_sibling = _exchange([grads[n] for n in sharded], FLIPS_SIBLING, 'swap', 'grads_core_swap')
    chip_sums = [_pair_add_call(_two_d(grads[n], 0), _two_d(r, 0), 'grads_core_add').reshape(r.shape)
                 for n, r in zip(sharded, from_sibling)]
    per_chip = _exchange(chip_sums, FLIPS_CHIPS, 'scatter', 'grads_chip_scatter')
    out = {}
    for n, parts in zip(sharded, per_chip):
        res = _adamw_call(_two_d(parts, 1), _two_d(w[n], 0), _two_d(m[n], 0), _two_d(v[n], 0), 'adamw_sharded')
        out[n] = tuple(r.reshape(w[n].shape) for r in res)

    small = [grads[n] for n in REPLICATED]
    all_parts = _exchange([_pack(small)], FLIPS_ALL, 'gather', 'grads_all_gather')[0]
    res = _adamw_call(all_parts, _pack([w[n] for n in REPLICATED]), _pack([m[n] for n in REPLICATED]),
                      _pack([v[n] for n in REPLICATED]), 'adamw_replicated')
    unpacked = [_unpack(r, small) for r in res]
    for i, n in enumerate(REPLICATED):
        out[n] = tuple(u[i] for u in unpacked)

    return (loss, grad_x[None]) + tuple(out[n][k] for k in range(4) for n in WEIGHT_NAMES)
```

```python
import functools
import math

import numpy as np
import jax
import jax.numpy as jnp
from jax import lax
from jax.experimental import pallas as pl
from jax.experimental.pallas import tpu as pltpu

F32 = jnp.float32
BF16 = jnp.bfloat16

VMEM_LIMIT_BYTES = 56 * 1024 * 1024
SUBLANES = 8
LANES = 128

NORM_EPS = 1e-6
DEPTH = 4
N_BRANCHES = 4
S5_GROUP = 16
S5_STATE = 64
GDN_HEAD_DIM = 128
GDN_CONV = 4
GDN_CHUNK = 64
SWA_HEAD_DIM = 64
SWA_KV_GROUP = 4
WINDOW = 128
SWA_BLOCK = 128
T5_BUCKETS = 32
T5_MAX_DISTANCE = 128
MLA_NOPE = 128
MLA_ROPE = 64
MLA_V = 128
ROPE_THETA = 10000.0

ADAM_LR = 0.001
ADAM_B1 = 0.9
ADAM_B2 = 0.999
ADAM_EPS = 1e-08
ADAM_WD = 0.01
ADAM_STEP = 10

_HIGHEST = lax.Precision.HIGHEST


def _params(sem, vmem=VMEM_LIMIT_BYTES):
    return pltpu.CompilerParams(dimension_semantics=sem, vmem_limit_bytes=vmem)


def _tile(dim, target, align):
    if dim <= target:
        return dim
    t = (target // align) * align
    while t >= align:
        if dim % t == 0:
            return t
        t -= align
    return dim


_DOT_DIMS = {"nn": ((1,), (0,)), "nt": ((1,), (1,)), "tn": ((0,), (0,))}


MM_OPERAND_TILE_BYTES = 12 * 1024 * 1024


def _mm(a, b, form="nn", out_dtype=F32, tm=1024, tn=1024, tk=None, relu2_a=False, relu2_grad_of=None, name="mm"):
    if form == "nn":
        (M, K), (K2, N) = a.shape, b.shape
    elif form == "nt":
        (M, K), (N, K2) = a.shape, b.shape
    else:
        (K, M), (K2, N) = a.shape, b.shape
    assert K == K2, (a.shape, b.shape, form)
    tm = _tile(M, tm, LANES if form == "tn" else 16)
    tn = _tile(N, tn, LANES)
    if tk is None:
        tk = MM_OPERAND_TILE_BYTES // (tm * a.dtype.itemsize + tn * b.dtype.itemsize)
    tk = _tile(K, max(LANES, tk // LANES * LANES), LANES)
    nk = K // tk
    dims = (_DOT_DIMS[form], ((), ()))
    n_in = 2 if relu2_grad_of is None else 3

    def dot(a_ref, b_ref):
        av = a_ref[...]
        if relu2_a:
            av = jnp.square(jnp.maximum(av, 0.0))
        return lax.dot_general(av.astype(BF16), b_ref[...].astype(BF16), dims, preferred_element_type=F32)

    def finish(acc, refs):
        if relu2_grad_of is not None:
            acc = acc * (2.0 * jnp.maximum(refs[2][...], 0.0))
        refs[n_in][...] = acc.astype(refs[n_in].dtype)

    def body_single(*refs):
        finish(dot(refs[0], refs[1]), refs)

    def body_acc(*refs):
        acc_ref = refs[n_in + 1]
        k = pl.program_id(2)

        @pl.when(k == 0)
        def _():
            acc_ref[...] = jnp.zeros_like(acc_ref)

        acc_ref[...] += dot(refs[0], refs[1])

        @pl.when(k == nk - 1)
        def _():
            finish(acc_ref[...], refs)

    body = body_single if nk == 1 else body_acc
    extra = [] if relu2_grad_of is None else [relu2_grad_of]
    extra_specs = [pl.BlockSpec((tm, tn), lambda i, j, k: (i, j))] * len(extra)

    if form == "nn":
        a_spec = pl.BlockSpec((tm, tk), lambda i, j, k: (i, k))
        b_spec = pl.BlockSpec((tk, tn), lambda i, j, k: (k, j))
    elif form == "nt":
        a_spec = pl.BlockSpec((tm, tk), lambda i, j, k: (i, k))
        b_spec = pl.BlockSpec((tn, tk), lambda i, j, k: (j, k))
    else:
        a_spec = pl.BlockSpec((tk, tm), lambda i, j, k: (k, i))
        b_spec = pl.BlockSpec((tk, tn), lambda i, j, k: (k, j))
    return pl.pallas_call(
        body,
        out_shape=jax.ShapeDtypeStruct((M, N), out_dtype),
        grid=(M // tm, N // tn, nk),
        in_specs=[a_spec, b_spec] + extra_specs,
        out_specs=pl.BlockSpec((tm, tn), lambda i, j, k: (i, j)),
        scratch_shapes=[] if nk == 1 else [pltpu.VMEM((tm, tn), F32)],
        compiler_params=_params(("parallel", "parallel", "arbitrary")),
        name=name,
    )(a, b, *extra)


@jax.custom_vjp
def relu2_linear(x, w):
    return _mm(x, w, "nn", relu2_a=True, name="mlp_out_fwd")


def _relu2_linear_fwd(x, w):
    return _mm(x, w, "nn", relu2_a=True, name="mlp_out_fwd"), (x, w)


def _relu2_linear_bwd(res, dy):
    x, w = res
    dx = _mm(dy, w, "nt", relu2_grad_of=x, name="mlp_out_dx")
    dw = _mm(x, dy, "tn", out_dtype=w.dtype, relu2_a=True, name="mlp_out_dw")
    return dx, dw


relu2_linear.defvjp(_relu2_linear_fwd, _relu2_linear_bwd)


@functools.partial(jax.custom_vjp, nondiff_argnums=(2,))
def linear(x, w, name):
    return _mm(x, w, "nn", name=name + "_fwd")


def _linear_fwd(x, w, name):
    return _mm(x, w, "nn", name=name + "_fwd"), (x, w)


def _linear_bwd(name, res, dy):
    x, w = res
    dx = _mm(dy, w, "nt", out_dtype=x.dtype, name=name + "_dx")
    dw = _mm(x, dy, "tn", out_dtype=w.dtype, name=name + "_dw")
    return dx, dw


linear.defvjp(_linear_fwd, _linear_bwd)


def _rms_rows(L, D):
    return _tile(L, max(SUBLANES, (1 << 20) // D), SUBLANES)


def _rms_fwd_call(x, gain, name):
    L, D = x.shape
    tm = _rms_rows(L, D)

    def body(x_ref, g_ref, o_ref):
        xv = x_ref[...]
        r = lax.rsqrt(jnp.mean(xv * xv, axis=-1, keepdims=True) + NORM_EPS)
        o_ref[...] = xv * r * g_ref[...]

    return pl.pallas_call(
        body,
        out_shape=jax.ShapeDtypeStruct((L, D), F32),
        grid=(L // tm,),
        in_specs=[pl.BlockSpec((tm, D), lambda i: (i, 0)), pl.BlockSpec((1, D), lambda i: (0, 0))],
        out_specs=pl.BlockSpec((tm, D), lambda i: (i, 0)),
        compiler_params=_params(("parallel",)),
        name=name,
    )(x, gain.reshape(1, D))


def _rms_bwd_call(x, gain, dy, name):
    L, D = x.shape
    tm = _rms_rows(L, D)

    def body(x_ref, g_ref, dy_ref, dx_ref, dg_ref):
        xv = x_ref[...]
        dyv = dy_ref[...]
        r = lax.rsqrt(jnp.mean(xv * xv, axis=-1, keepdims=True) + NORM_EPS)
        xh = xv * r
        dyg = dyv * g_ref[...]
        dx_ref[...] = r * (dyg - xh * jnp.mean(dyg * xh, axis=-1, keepdims=True))

        @pl.when(pl.program_id(0) == 0)
        def _():
            dg_ref[...] = jnp.zeros_like(dg_ref)

        dg_ref[...] += jnp.sum(dyv * xh, axis=0, keepdims=True)

    return pl.pallas_call(
        body,
        out_shape=(jax.ShapeDtypeStruct((L, D), F32), jax.ShapeDtypeStruct((1, D), F32)),
        grid=(L // tm,),
        in_specs=[pl.BlockSpec((tm, D), lambda i: (i, 0)), pl.BlockSpec((1, D), lambda i: (0, 0)),
                  pl.BlockSpec((tm, D), lambda i: (i, 0))],
        out_specs=(pl.BlockSpec((tm, D), lambda i: (i, 0)), pl.BlockSpec((1, D), lambda i: (0, 0))),
        compiler_params=_params(("arbitrary",)),
        name=name,
    )(x, gain.reshape(1, D), dy)


@functools.partial(jax.custom_vjp, nondiff_argnums=(2,))
def rms_norm(x, gain, name):
    return _rms_fwd_call(x, gain, name + "_fwd")


def _rms_norm_fwd(x, gain, name):
    return _rms_fwd_call(x, gain, name + "_fwd"), (x, gain)


def _rms_norm_bwd(name, res, dy):
    x, gain = res
    dx, dg = _rms_bwd_call(x, gain, dy, name + "_bwd")
    return dx, dg.reshape(gain.shape)


rms_norm.defvjp(_rms_norm_fwd, _rms_norm_bwd)


SCAN_COLS = 512
SCAN_ROWS = 256
SCAN_UNROLL = 8


def _to_scan_cols(t, nd):
    P = t.shape[-1]
    pc = _tile(P, SCAN_COLS, LANES)
    lead = t.shape[:-3]
    t = t.reshape(lead + (nd, 2, P // pc, pc))
    t = jnp.swapaxes(t, -3, -2)
    return t.reshape(lead + (nd * 2 * P,))


def _scan_call(b, a, nd, reverse_dir0, name):
    L, NC = b.shape
    P = NC // (2 * nd)
    pc = _tile(P, SCAN_COLS, LANES)
    ncb = P // pc
    T = _tile(L, SCAN_ROWS, SUBLANES)
    nt = L // T
    rev0 = 1 if reverse_dir0 else 0
    G = SUBLANES

    def rev_of(d):
        return d + rev0 - 2 * d * rev0

    a3 = a.reshape(nd * ncb, 2, pc)
    cmul = lambda x, y: (x[0] * y[0] - x[1] * y[1], x[0] * y[1] + x[1] * y[0])
    pows = [(a3[:, 0], a3[:, 1])]
    for _ in range(G - 1):
        pows.append(cmul(pows[-1], pows[0]))
    cols = lambda t: jnp.stack(t, axis=1).reshape(nd, NC // nd)
    steps = jnp.stack([cols(pows[0]), cols(pows[1]), cols(pows[3])], axis=0).reshape(3, NC)
    with_time = jnp.stack([cols(p) for p in pows], axis=0)
    carry_pows = jnp.concatenate([with_time[::-1, d] if rev_of(d) else with_time[:, d] for d in range(nd)], axis=1)

    def body(b_ref, step_ref, pow_ref, o_ref, carry_ref):
        rv = rev_of(pl.program_id(0))

        @pl.when(pl.program_id(2) == 0)
        def _():
            carry_ref[...] = jnp.zeros_like(carry_ref)

        row = lax.broadcasted_iota(jnp.int32, (G, pc), 0)
        mults = [(step_ref[j:j + 1, :pc], step_ref[j:j + 1, pc:]) for j in range(3)]
        pr, pi = pow_ref[:, :pc], pow_ref[:, pc:]

        def run(backwards):
            def shifted(x, k):
                if backwards:
                    return jnp.where(row < G - k, pltpu.roll(x, G - k, 0), 0.0)
                return jnp.where(row >= k, pltpu.roll(x, k, 0), 0.0)

            def group(g, carry):
                cr, ci = carry
                r0 = pl.multiple_of((T // G - 1 - g if backwards else g) * G, G)
                x = b_ref[pl.ds(r0, G), :]
                xr, xi = x[:, :pc], x[:, pc:]
                for j, (mr, mi) in enumerate(mults):
                    sr, si = shifted(xr, 1 << j), shifted(xi, 1 << j)
                    xr, xi = xr + (mr * sr - mi * si), xi + (mr * si + mi * sr)
                o_ref[pl.ds(r0, G), :] = jnp.concatenate([xr + (pr * cr - pi * ci), xi + (pr * ci + pi * cr)], axis=1)
                last = o_ref[pl.ds(r0 + (0 if backwards else G - 1), 1), :]
                return last[:, :pc], last[:, pc:]

            cr, ci = lax.fori_loop(0, T // G, group, (carry_ref[:, :pc], carry_ref[:, pc:]), unroll=SCAN_UNROLL)
            carry_ref[...] = jnp.concatenate([cr, ci], axis=1)

        pl.when(rv == 0)(functools.partial(run, False))
        pl.when(rv == 1)(functools.partial(run, True))

    def t_idx(d, i):
        return i + rev_of(d) * (nt - 1 - 2 * i)

    spec = pl.BlockSpec((T, 2 * pc), lambda d, c, i: (t_idx(d, i), d * ncb + c))
    return pl.pallas_call(
        body,
        out_shape=jax.ShapeDtypeStruct(b.shape, F32),
        grid=(nd, ncb, nt),
        in_specs=[spec, pl.BlockSpec((3, 2 * pc), lambda d, c, i: (0, d * ncb + c)),
                  pl.BlockSpec((G, 2 * pc), lambda d, c, i: (0, d * ncb + c))],
        out_specs=spec,
        scratch_shapes=[pltpu.VMEM((1, 2 * pc), F32)],
        compiler_params=_params(("arbitrary", "arbitrary", "arbitrary")),
        name=name,
    )(b, steps, carry_pows)


def _scan_da_call(g, sp, nd, name):
    L, NC = g.shape
    P = NC // (2 * nd)
    pc = _tile(P, SCAN_COLS, LANES)
    T = _tile(L, SCAN_ROWS * 2, SUBLANES)

    def body(g_ref, s_ref, o_ref):
        @pl.when(pl.program_id(1) == 0)
        def _():
            o_ref[...] = jnp.zeros_like(o_ref)

        gr, gi = g_ref[:, :pc], g_ref[:, pc:]
        sr, si = s_ref[:, :pc], s_ref[:, pc:]
        dar = jnp.sum(gr * sr + gi * si, axis=0, keepdims=True)
        dai = jnp.sum(gi * sr - gr * si, axis=0, keepdims=True)
        o_ref[...] += jnp.concatenate([dar, dai], axis=1)

    spec = pl.BlockSpec((T, 2 * pc), lambda c, i: (i, c))
    return pl.pallas_call(
        body,
        out_shape=jax.ShapeDtypeStruct((1, NC), F32),
        grid=(NC // (2 * pc), L // T),
        in_specs=[spec, spec],
        out_specs=pl.BlockSpec((1, 2 * pc), lambda c, i: (0, c)),
        compiler_params=_params(("arbitrary", "arbitrary")),
        name=name,
    )(g, sp)


def _conj_cols(a, nd):
    P = a.shape[-1] // (2 * nd)
    pc = _tile(P, SCAN_COLS, LANES)
    sign = jnp.tile(jnp.concatenate([jnp.ones((pc,), F32), -jnp.ones((pc,), F32)]), a.shape[-1] // (2 * pc))
    return a * sign


@functools.partial(jax.custom_vjp, nondiff_argnums=(2,))
def s5_scan(b, a, nd):
    return _scan_call(b, a, nd, False, "s5_scan_fwd")


def _s5_scan_fwd(b, a, nd):
    s = _scan_call(b, a, nd, False, "s5_scan_fwd")
    return s, (s, a)


def _s5_scan_bwd(nd, res, ds):
    s, a = res
    g = _scan_call(ds, _conj_cols(a, nd), nd, True, "s5_scan_adj")
    L, NC = s.shape
    half = NC // nd
    zero = jnp.zeros((1, half), F32)
    prev = [jnp.concatenate([zero, s[:-1, :half]], axis=0)]
    if nd == 2:
        prev.append(jnp.concatenate([s[1:, half:], zero], axis=0))
    sp = jnp.concatenate(prev, axis=1)
    da = _scan_da_call(g, sp, nd, "s5_scan_da")
    return g, da


s5_scan.defvjp(_s5_scan_fwd, _s5_scan_bwd)


ATTN_TQ = 1024
ATTN_TK = 1024
ATTN_BWD_TQ = 512
ATTN_BWD_TK = 512


def _attn_fwd_call(q, k, v, scale):
    H, L, DQ = q.shape
    DV = v.shape[-1]
    tq = _tile(L, ATTN_TQ, LANES)
    tk = _tile(L, ATTN_TK, LANES)
    nk = L // tk

    def body(q_ref, k_ref, v_ref, o_ref, lse_ref, m_s, l_s, acc_s):
        j = pl.program_id(2)

        @pl.when(j == 0)
        def _():
            m_s[...] = jnp.full_like(m_s, -jnp.inf)
            l_s[...] = jnp.zeros_like(l_s)
            acc_s[...] = jnp.zeros_like(acc_s)

        s = lax.dot_general(q_ref[0].astype(BF16), k_ref[0].astype(BF16), (((1,), (1,)), ((), ())),
                            preferred_element_type=F32) * scale
        m_old = m_s[...]
        m_new = jnp.maximum(m_old, jnp.max(s, axis=1, keepdims=True))
        alpha = jnp.exp(m_old - m_new)
        p = jnp.exp(s - m_new)
        l_s[...] = alpha * l_s[...] + jnp.sum(p, axis=1, keepdims=True)
        acc_s[...] = alpha * acc_s[...] + jnp.dot(p.astype(BF16), v_ref[0].astype(BF16), preferred_element_type=F32)
        m_s[...] = m_new

        @pl.when(j == nk - 1)
        def _():
            o_ref[0] = acc_s[...] / l_s[...]
            lse_ref[0] = m_s[...] + jnp.log(l_s[...])

    return pl.pallas_call(
        body,
        out_shape=(jax.ShapeDtypeStruct((H, L, DV), F32), jax.ShapeDtypeStruct((H, L, 1), F32)),
        grid=(H, L // tq, nk),
        in_specs=[pl.BlockSpec((1, tq, DQ), lambda h, i, j: (h, i, 0)),
                  pl.BlockSpec((1, tk, DQ), lambda h, i, j: (h, j, 0)),
                  pl.BlockSpec((1, tk, DV), lambda h, i, j: (h, j, 0))],
        out_specs=(pl.BlockSpec((1, tq, DV), lambda h, i, j: (h, i, 0)),
                   pl.BlockSpec((1, tq, 1), lambda h, i, j: (h, i, 0))),
        scratch_shapes=[pltpu.VMEM((tq, 1), F32), pltpu.VMEM((tq, 1), F32), pltpu.VMEM((tq, DV), F32)],
        compiler_params=_params(("parallel", "parallel", "arbitrary")),
        name="mla_attn_fwd",
    )(q, k, v)


def _attn_bwd_call(q, k, v, do, lse, delta, scale):
    H, L, DQ = q.shape
    DV = v.shape[-1]
    tq = _tile(L, ATTN_BWD_TQ, LANES)
    tk = _tile(L, ATTN_BWD_TK, LANES)
    nq = L // tq

    def body(q_ref, k_ref, v_ref, do_ref, lse_ref, dl_ref, dq_ref, dk_ref, dv_ref, dk_s, dv_s):
        j = pl.program_id(1)
        i = pl.program_id(2)

        @pl.when(jnp.logical_and(j == 0, i == 0))
        def _():
            dq_ref[...] = jnp.zeros_like(dq_ref)

        @pl.when(i == 0)
        def _():
            dk_s[...] = jnp.zeros_like(dk_s)
            dv_s[...] = jnp.zeros_like(dv_s)

        qb = q_ref[0].astype(BF16)
        kb = k_ref[0].astype(BF16)
        dob = do_ref[0].astype(BF16)
        s = lax.dot_general(qb, kb, (((1,), (1,)), ((), ())), preferred_element_type=F32) * scale
        p = jnp.exp(s - lse_ref[0])
        dv_s[...] += lax.dot_general(p.astype(BF16), dob, (((0,), (0,)), ((), ())), preferred_element_type=F32)
        dp = lax.dot_general(dob, v_ref[0].astype(BF16), (((1,), (1,)), ((), ())), preferred_element_type=F32)
        ds = (p * (dp - dl_ref[0]) * scale).astype(BF16)
        dk_s[...] += lax.dot_general(ds, qb, (((0,), (0,)), ((), ())), preferred_element_type=F32)
        rows = pl.ds(pl.multiple_of(i * tq, tq), tq)
        dq_ref[0, rows, :] += jnp.dot(ds, kb, preferred_element_type=F32)

        @pl.when(i == nq - 1)
        def _():
            dk_ref[0] = dk_s[...]
            dv_ref[0] = dv_s[...]

    return pl.pallas_call(
        body,
        out_shape=(jax.ShapeDtypeStruct((H, L, DQ), F32), jax.ShapeDtypeStruct((H, L, DQ), F32),
                   jax.ShapeDtypeStruct((H, L, DV), F32)),
        grid=(H, L // tk, nq),
        in_specs=[pl.BlockSpec((1, tq, DQ), lambda h, j, i: (h, i, 0)),
                  pl.BlockSpec((1, tk, DQ), lambda h, j, i: (h, j, 0)),
                  pl.BlockSpec((1, tk, DV), lambda h, j, i: (h, j, 0)),
                  pl.BlockSpec((1, tq, DV), lambda h, j, i: (h, i, 0)),
                  pl.BlockSpec((1, tq, 1), lambda h, j, i: (h, i, 0)),
                  pl.BlockSpec((1, tq, 1), lambda h, j, i: (h, i, 0))],
        out_specs=(pl.BlockSpec((1, L, DQ), lambda h, j, i: (h, 0, 0)),
                   pl.BlockSpec((1, tk, DQ), lambda h, j, i: (h, j, 0)),
                   pl.BlockSpec((1, tk, DV), lambda h, j, i: (h, j, 0))),
        scratch_shapes=[pltpu.VMEM((tk, DQ), F32), pltpu.VMEM((tk, DV), F32)],
        compiler_params=_params(("arbitrary", "arbitrary", "arbitrary")),
        name="mla_attn_bwd",
    )(q, k, v, do, lse, delta)


@functools.partial(jax.custom_vjp, nondiff_argnums=(3,))
def full_attention(q, k, v, scale):
    return _attn_fwd_call(q, k, v, scale)[0]


def _full_attention_fwd(q, k, v, scale):
    o, lse = _attn_fwd_call(q, k, v, scale)
    return o, (q, k, v, o, lse)


def _full_attention_bwd(scale, res, do):
    q, k, v, o, lse = res
    delta = jnp.sum(do * o, axis=-1, keepdims=True)
    return _attn_bwd_call(q, k, v, do, lse, delta, scale)


full_attention.defvjp(_full_attention_fwd, _full_attention_bwd)


_NN = ((1,), (0,))
_NT = ((1,), (1,))
_TN = ((0,), (0,))
_DOT_VJP = {_NN: (("g", "b", _NT), ("a", "g", _TN)),
            _NT: (("g", "b", _NN), ("g", "a", _TN)),
            _TN: (("b", "g", _NT), ("a", "g", _NN))}


def _dot1(a, b, dims):
    return lax.dot_general(a.astype(BF16), b.astype(BF16), (dims, ((), ())), preferred_element_type=F32)


def _split_bf16(t):
    hi = t.astype(BF16)
    return hi, (t - hi.astype(F32)).astype(BF16)


def _dot3_raw(a, b, dims):
    ah, al = _split_bf16(a)
    bh, bl = _split_bf16(b)
    d = lambda p, q: lax.dot_general(p, q, (dims, ((), ())), preferred_element_type=F32)
    return d(ah, bh) + (d(ah, bl) + d(al, bh))


def _with_f32_cotangents(raw):
    @functools.partial(jax.custom_vjp, nondiff_argnums=(2,))
    def dot(a, b, dims):
        return raw(a, b, dims)

    def fwd(a, b, dims):
        return raw(a, b, dims), (a, b)

    def bwd(dims, res, g):
        a, b = res
        ops = {"a": a, "b": b, "g": g}
        (p, q, dp), (r, s, dr) = _DOT_VJP[dims]
        return raw(ops[p], ops[q], dp), raw(ops[r], ops[s], dr)

    dot.defvjp(fwd, bwd)
    return dot


def _halves(t, axis):
    n = t.shape[axis] // 2
    return lax.slice_in_dim(t, 0, n, axis=axis), lax.slice_in_dim(t, n, 2 * n, axis=axis)


@functools.partial(jax.custom_vjp, nondiff_argnums=(2,))
def _stack(a, b, axis):
    return jnp.concatenate([a, b], axis=axis)


_stack.defvjp(lambda a, b, axis: (jnp.concatenate([a, b], axis=axis), None),
              lambda axis, _, g: _halves(g, axis))


@functools.partial(jax.custom_vjp, nondiff_argnums=(1,))
def _unstack(t, axis):
    return _halves(t, axis)


_unstack.defvjp(lambda t, axis: (_halves(t, axis), None),
                lambda axis, _, g: (jnp.concatenate(list(g), axis=axis),))


_bdot = _with_f32_cotangents(_dot1)
_dot3 = _with_f32_cotangents(_dot3_raw)


def _swa_heads(q, k, v, bias, sink, valid):
    R = range(len(q))
    kv = [h // SWA_KV_GROUP for h in R]
    s = [_bdot(q[h], k[kv[h]], _NT) * (SWA_HEAD_DIM ** -0.5) + bias[h] for h in R]
    s = [jnp.where(valid, x, -1e30) for x in s]
    m = [lax.stop_gradient(jnp.maximum(jnp.max(s[h], axis=1, keepdims=True), sink[h])) for h in R]
    p = [jnp.exp(s[h] - m[h]) for h in R]
    den = [jnp.sum(p[h], axis=1, keepdims=True) + jnp.exp(sink[h] - m[h]) for h in R]
    return [_bdot(p[h] / den[h], v[kv[h]], _NN) for h in R]


def _swa_valid(n, L):
    qi = lax.broadcasted_iota(jnp.int32, (SWA_BLOCK, 3 * SWA_BLOCK), 0)
    sj = lax.broadcasted_iota(jnp.int32, (SWA_BLOCK, 3 * SWA_BLOCK), 1)
    rel = sj - SWA_BLOCK - qi
    kpos = n * SWA_BLOCK + sj - SWA_BLOCK
    return (jnp.abs(rel) <= WINDOW) & (kpos >= 0) & (kpos < L)


def _swa_specs(HQ, HKV, L):
    B = SWA_BLOCK
    q_spec = pl.BlockSpec((HQ, B, SWA_HEAD_DIM), lambda n: (0, n, 0))
    kv_spec = pl.BlockSpec((HKV, L + 2 * B, SWA_HEAD_DIM), lambda n: (0, 0, 0))
    bias_spec = pl.BlockSpec((HQ, B, 3 * B), lambda n: (0, 0, 0))
    sink_spec = pl.BlockSpec((HQ, B, 1), lambda n: (0, 0, 0))
    return q_spec, kv_spec, bias_spec, sink_spec


def _swa_fwd_call(q, kpad, vpad, bias, sink):
    HQ, L, _ = q.shape
    HKV = kpad.shape[0]
    B = SWA_BLOCK

    def body(q_ref, k_ref, v_ref, b_ref, s_ref, o_ref):
        n = pl.program_id(0)
        rows = pl.ds(pl.multiple_of(n * B, B), 3 * B)
        out = _swa_heads([q_ref[h] for h in range(HQ)], [k_ref[g, rows, :] for g in range(HKV)],
                         [v_ref[g, rows, :] for g in range(HKV)], [b_ref[h] for h in range(HQ)],
                         [s_ref[h] for h in range(HQ)], _swa_valid(n, L))
        for h in range(HQ):
            o_ref[h] = out[h]

    q_spec, kv_spec, bias_spec, sink_spec = _swa_specs(HQ, HKV, L)
    return pl.pallas_call(
        body,
        out_shape=jax.ShapeDtypeStruct(q.shape, F32),
        grid=(L // B,),
        in_specs=[q_spec, kv_spec, kv_spec, bias_spec, sink_spec],
        out_specs=q_spec,
        compiler_params=_params(("parallel",)),
        name="swa_fwd",
    )(q, kpad, vpad, bias, sink)


def _swa_bwd_call(q, kpad, vpad, bias, sink, do):
    HQ, L, _ = q.shape
    HKV = kpad.shape[0]
    B = SWA_BLOCK

    def body(q_ref, k_ref, v_ref, b_ref, s_ref, do_ref, dq_ref, dk_ref, dv_ref, db_ref, ds_ref):
        n = pl.program_id(0)

        @pl.when(n == 0)
        def _():
            dk_ref[...] = jnp.zeros_like(dk_ref)
            dv_ref[...] = jnp.zeros_like(dv_ref)
            db_ref[...] = jnp.zeros_like(db_ref)
            ds_ref[...] = jnp.zeros_like(ds_ref)

        rows = pl.ds(pl.multiple_of(n * B, B), 3 * B)
        _, vjp = jax.vjp(functools.partial(_swa_heads, valid=_swa_valid(n, L)),
                         [q_ref[h] for h in range(HQ)], [k_ref[g, rows, :] for g in range(HKV)],
                         [v_ref[g, rows, :] for g in range(HKV)], [b_ref[h] for h in range(HQ)],
                         [s_ref[h] for h in range(HQ)])
        dq, dk, dv, db, dsk = vjp([do_ref[h] for h in range(HQ)])
        for h in range(HQ):
            dq_ref[h] = dq[h]
            db_ref[h] += db[h]
            ds_ref[h] += dsk[h]
        for g in range(HKV):
            dk_ref[g, rows, :] += dk[g]
            dv_ref[g, rows, :] += dv[g]

    q_spec, kv_spec, bias_spec, sink_spec = _swa_specs(HQ, HKV, L)
    return pl.pallas_call(
        body,
        out_shape=(jax.ShapeDtypeStruct(q.shape, F32), jax.ShapeDtypeStruct(kpad.shape, F32),
                   jax.ShapeDtypeStruct(vpad.shape, F32), jax.ShapeDtypeStruct(bias.shape, F32),
                   jax.ShapeDtypeStruct(sink.shape, F32)),
        grid=(L // B,),
        in_specs=[q_spec, kv_spec, kv_spec, bias_spec, sink_spec, q_spec],
        out_specs=(q_spec, kv_spec, kv_spec, bias_spec, sink_spec),
        compiler_params=_params(("arbitrary",)),
        name="swa_bwd",
    )(q, kpad, vpad, bias, sink, do)


@jax.custom_vjp
def window_attention(q, kpad, vpad, bias, sink):
    return _swa_fwd_call(q, kpad, vpad, bias, sink)


def _window_attention_fwd(q, kpad, vpad, bias, sink):
    return _swa_fwd_call(q, kpad, vpad, bias, sink), (q, kpad, vpad, bias, sink)


def _window_attention_bwd(res, do):
    return _swa_bwd_call(*res, do)


window_attention.defvjp(_window_attention_fwd, _window_attention_bwd)


def _unit_tri_inverses(a):
    C = a[0].shape[0]
    ii = lax.broadcasted_iota(jnp.int32, (C, C), 0)
    jj = lax.broadcasted_iota(jnp.int32, (C, C), 1)
    pw = [-x for x in a]
    t = [jnp.where(ii == jj, 1.0, 0.0)] * len(a)
    for _ in range(int(math.log2(C))):
        both = [_dot3_raw(jnp.concatenate([ts, ps], axis=0), ps, _NN) for ts, ps in zip(t, pw)]
        t = [ts + b[:C] for ts, b in zip(t, both)]
        pw = [b[C:] for b in both]
    return t


@jax.custom_vjp
def _known_inverse(a, t):
    return t


def _known_inverse_fwd(a, t):
    return t, t


def _known_inverse_bwd(t, dt):
    da = -_dot3_raw(_dot3_raw(t, dt, _TN), t, _NT)
    return da, jnp.zeros_like(t)


_known_inverse.defvjp(_known_inverse_fwd, _known_inverse_bwd)


def _gdn_chunks(S, q, k, v, gc, gr, gl, beta, t_known, backwards):
    R = range(len(q))
    C = q[0].shape[0]
    ii = lax.broadcasted_iota(jnp.int32, (C, C), 0)
    jj = lax.broadcasted_iota(jnp.int32, (C, C), 1)
    causal = [(ii <= jj) if backwards[s] else (ii >= jj) for s in R]
    strict = [(ii < jj) if backwards[s] else (ii > jj) for s in R]
    decay = [jnp.where(causal[s], jnp.exp(jnp.where(causal[s], gc[s] - gr[s], 0.0)), 0.0) for s in R]
    kb = [k[s] * beta[s] for s in R]
    kk_qk = [_unstack(_bdot(_stack(kb[s], q[s], 0), k[s], _NT), 0) for s in R]
    a = [jnp.where(strict[s], kk_qk[s][0] * decay[s], 0.0) for s in R]
    t = _unit_tri_inverses(a) if t_known is None else [_known_inverse(a[s], t_known[s]) for s in R]
    u_w = [_unstack(_dot3(t[s], _stack(v[s] * beta[s], kb[s] * jnp.exp(gc[s]), 1), _NN), 1) for s in R]
    ws_qs = [_unstack(_bdot(_stack(u_w[s][1], q[s] * jnp.exp(gc[s]), 0), S[s], _NN), 0) for s in R]
    v_new = [u_w[s][0] - ws_qs[s][0] for s in R]
    o = [ws_qs[s][1] + _bdot(kk_qk[s][1] * decay[s], v_new[s], _NN) for s in R]
    s_new = [S[s] * jnp.exp(gl[s]) + _bdot(k[s] * jnp.exp(gl[s] - gc[s]), v_new[s], _TN) for s in R]
    return s_new, o, t


def _gdn_specs(H, L, C, Dh, flip):
    NC = L // C
    idx = (lambda c: NC - 1 - c) if flip else (lambda c: c)
    seq = pl.BlockSpec((C, H * Dh), lambda c: (idx(c), 0))

    def scalar(rows, cols, group):
        return pl.BlockSpec((H, 1, rows, cols), lambda c: (group, idx(c), 0, 0))

    return seq, scalar


def _gdn_fwd_call(q, k, v, gc, gr, gl, beta):
    L, HD = q.shape
    C = gc.shape[2]
    NC = L // C
    H = gc.shape[0] // 2
    Dh = HD // H

    def body(qf, kf, vf, qb, kb, vb, gcf, grf, glf, bf, gcb, grb, glb, bb,
             of_ref, ob_ref, stf_ref, stb_ref, tf_ref, tb_ref, s_scr):
        @pl.when(pl.program_id(0) == 0)
        def _():
            s_scr[...] = jnp.zeros_like(s_scr)

        groups = ((qf, kf, vf, gcf, grf, glf, bf), (qb, kb, vb, gcb, grb, glb, bb))
        outs = ((of_ref, stf_ref, tf_ref), (ob_ref, stb_ref, tb_ref))
        seqs = [(d, h) for d in range(2) for h in range(H)]
        cols = [slice(h * Dh, (h + 1) * Dh) for h in range(H)]
        s0 = [s_scr[d * H + h] for d, h in seqs]
        for (d, h), s in zip(seqs, s0):
            outs[d][1][h, 0] = s
        seq_in = lambda j: [groups[d][j][:, cols[h]] for d, h in seqs]
        scal_in = lambda j: [groups[d][j][h, 0] for d, h in seqs]
        s1, o, t = _gdn_chunks(s0, seq_in(0), seq_in(1), seq_in(2), scal_in(3), scal_in(4), scal_in(5), scal_in(6),
                               None, [d == 1 for d, _ in seqs])
        for i, (d, h) in enumerate(seqs):
            s_scr[d * H + h] = s1[i]
            outs[d][0][:, cols[h]] = o[i]
            outs[d][2][h, 0] = t[i]

    seq_f, sc_f = _gdn_specs(H, L, C, Dh, False)
    seq_b, sc_b = _gdn_specs(H, L, C, Dh, True)
    sds = jax.ShapeDtypeStruct
    return pl.pallas_call(
        body,
        out_shape=(sds((L, HD), F32), sds((L, HD), F32), sds((H, NC, Dh, Dh), F32), sds((H, NC, Dh, Dh), F32),
                   sds((H, NC, C, C), F32), sds((H, NC, C, C), F32)),
        grid=(NC,),
        in_specs=[seq_f, seq_f, seq_f, seq_b, seq_b, seq_b,
                  sc_f(C, 1, 0), sc_f(1, C, 0), sc_f(1, 1, 0), sc_f(C, 1, 0),
                  sc_b(C, 1, 1), sc_b(1, C, 1), sc_b(1, 1, 1), sc_b(C, 1, 1)],
        out_specs=(seq_f, seq_b, sc_f(Dh, Dh, 0), sc_b(Dh, Dh, 0), sc_f(C, C, 0), sc_b(C, C, 0)),
        scratch_shapes=[pltpu.VMEM((2 * H, Dh, Dh), F32)],
        compiler_params=_params(("arbitrary",)),
        name="gdn_fwd",
    )(q, k, v, q, k, v, gc, gr, gl, beta, gc, gr, gl, beta)


def _gdn_bwd_call(q, k, v, gc, gr, gl, beta, st_f, st_b, t_f, t_b, do_f, do_b):
    L, HD = q.shape
    C = gc.shape[2]
    NC = L // C
    H = gc.shape[0] // 2
    Dh = HD // H

    def body(qf, kf, vf, qb, kb, vb, gcf, grf, glf, bf, gcb, grb, glb, bb, stf, stb, tf, tb, dof, dob,
             dqf, dkf, dvf, dqb, dkb, dvb, dgcf, dgrf, dglf, dbf, dgcb, dgrb, dglb, dbb, ds_scr):
        @pl.when(pl.program_id(0) == 0)
        def _():
            ds_scr[...] = jnp.zeros_like(ds_scr)

        groups = ((qf, kf, vf, gcf, grf, glf, bf, stf, tf, dof), (qb, kb, vb, gcb, grb, glb, bb, stb, tb, dob))
        outs = ((dqf, dkf, dvf, dgcf, dgrf, dglf, dbf), (dqb, dkb, dvb, dgcb, dgrb, dglb, dbb))
        seqs = [(d, h) for d in range(2) for h in range(H)]
        cols = [slice(h * Dh, (h + 1) * Dh) for h in range(H)]
        seq_in = lambda j: [groups[d][j][:, cols[h]] for d, h in seqs]
        scal_in = lambda j: [groups[d][j][h, 0] for d, h in seqs]
        t_known = scal_in(8)
        backwards = [d == 1 for d, _ in seqs]

        def chunks(*args):
            return _gdn_chunks(*args, t_known, backwards)[:2]

        _, vjp = jax.vjp(chunks, scal_in(7), seq_in(0), seq_in(1), seq_in(2), scal_in(3), scal_in(4), scal_in(5),
                         scal_in(6))
        grads = vjp(([ds_scr[d * H + h] for d, h in seqs], seq_in(9)))
        for i, (d, h) in enumerate(seqs):
            ds_scr[d * H + h] = grads[0][i]
            for j in range(3):
                outs[d][j][:, cols[h]] = grads[1 + j][i]
            for j in range(3, 7):
                outs[d][j][h, 0] = grads[1 + j][i]

    seq_f, sc_f = _gdn_specs(H, L, C, Dh, True)
    seq_b, sc_b = _gdn_specs(H, L, C, Dh, False)
    sds = jax.ShapeDtypeStruct
    seq_out = sds((L, HD), F32)
    half = lambda t: sds((H,) + t.shape[1:], F32)
    scal_f = [sc_f(C, 1, 0), sc_f(1, C, 0), sc_f(1, 1, 0), sc_f(C, 1, 0)]
    scal_b = [sc_b(C, 1, 1), sc_b(1, C, 1), sc_b(1, 1, 1), sc_b(C, 1, 1)]
    scal_b_out = [sc_b(C, 1, 0), sc_b(1, C, 0), sc_b(1, 1, 0), sc_b(C, 1, 0)]
    return pl.pallas_call(
        body,
        out_shape=(seq_out,) * 6 + (half(gc), half(gr), half(gl), half(beta)) * 2,
        grid=(NC,),
        in_specs=[seq_f, seq_f, seq_f, seq_b, seq_b, seq_b] + scal_f + scal_b
                 + [sc_f(Dh, Dh, 0), sc_b(Dh, Dh, 0), sc_f(C, C, 0), sc_b(C, C, 0), seq_f, seq_b],
        out_specs=[seq_f, seq_f, seq_f, seq_b, seq_b, seq_b] + scal_f + scal_b_out,
        scratch_shapes=[pltpu.VMEM((2 * H, Dh, Dh), F32)],
        compiler_params=_params(("arbitrary",)),
        name="gdn_bwd",
    )(q, k, v, q, k, v, gc, gr, gl, beta, gc, gr, gl, beta, st_f, st_b, t_f, t_b, do_f, do_b)


@jax.custom_vjp
def gated_delta_rule(q, k, v, gc, gr, gl, beta):
    return _gdn_fwd_call(q, k, v, gc, gr, gl, beta)[:2]


def _gated_delta_rule_fwd(q, k, v, gc, gr, gl, beta):
    o_f, o_b, st_f, st_b, t_f, t_b = _gdn_fwd_call(q, k, v, gc, gr, gl, beta)
    return (o_f, o_b), (q, k, v, gc, gr, gl, beta, st_f, st_b, t_f, t_b)


def _gated_delta_rule_bwd(res, do):
    (dqf, dkf, dvf, dqb, dkb, dvb, dgcf, dgrf, dglf, dbf, dgcb, dgrb, dglb, dbb) = _gdn_bwd_call(*res, *do)
    cat = lambda a, b: jnp.concatenate([a, b], axis=0)
    return dqf + dqb, dkf + dkb, dvf + dvb, cat(dgcf, dgcb), cat(dgrf, dgrb), cat(dglf, dglb), cat(dbf, dbb)


gated_delta_rule.defvjp(_gated_delta_rule_fwd, _gated_delta_rule_bwd)


GATE_ROWS = 128


def _gate_specs(L, D, nb):
    tm = _tile(L, GATE_ROWS, SUBLANES)
    logit_specs = [pl.BlockSpec((tm, D), functools.partial(lambda i, b: (i, b), b=b)) for b in range(nb)]
    row_spec = pl.BlockSpec((tm, D), lambda i: (i, 0))
    return tm, logit_specs, row_spec


def _gate_fwd_call(logits, branches):
    nb = len(branches)
    L, D = branches[0].shape
    tm, logit_specs, row_spec = _gate_specs(L, D, nb)

    def body(*refs):
        o_ref = refs[2 * nb]
        acc = jax.nn.sigmoid(refs[0][...]) * refs[nb][...]
        for b in range(1, nb):
            acc = acc + jax.nn.sigmoid(refs[b][...]) * refs[nb + b][...]
        o_ref[...] = acc

    return pl.pallas_call(
        body, out_shape=jax.ShapeDtypeStruct((L, D), F32), grid=(L // tm,),
        in_specs=logit_specs + [row_spec] * nb, out_specs=row_spec,
        compiler_params=_params(("parallel",)), name="gate_merge_fwd",
    )(*([logits] * nb), *branches)


def _gate_bwd_call(logits, branches, dm):
    nb = len(branches)
    L, D = branches[0].shape
    tm, logit_specs, row_spec = _gate_specs(L, D, nb)

    def body(*refs):
        dm_v = refs[2 * nb][...]
        dl_ref = refs[2 * nb + 1]
        for b in range(nb):
            sig = jax.nn.sigmoid(refs[b][...])
            refs[2 * nb + 2 + b][...] = dm_v * sig
            dl_ref[:, b * D:(b + 1) * D] = dm_v * refs[nb + b][...] * (sig * (1.0 - sig))

    return pl.pallas_call(
        body,
        out_shape=[jax.ShapeDtypeStruct(logits.shape, F32)] + [jax.ShapeDtypeStruct((L, D), F32)] * nb,
        grid=(L // tm,),
        in_specs=logit_specs + [row_spec] * (nb + 1),
        out_specs=[pl.BlockSpec((tm, nb * D), lambda i: (i, 0))] + [row_spec] * nb,
        compiler_params=_params(("parallel",)), name="gate_merge_bwd",
    )(*([logits] * nb), *branches, dm)


@jax.custom_vjp
def gate_merge(logits, branches):
    return _gate_fwd_call(logits, branches)


def _gate_merge_fwd(logits, branches):
    return _gate_fwd_call(logits, branches), (logits, branches)


def _gate_merge_bwd(res, dm):
    logits, branches = res
    out = _gate_bwd_call(logits, branches, dm)
    return out[0], tuple(out[1:])


gate_merge.defvjp(_gate_merge_fwd, _gate_merge_bwd)


def _loss_fwd_call(y, t):
    L, D = y.shape
    tm = _rms_rows(L, D)

    def body(y_ref, t_ref, o_ref):
        @pl.when(pl.program_id(0) == 0)
        def _():
            o_ref[...] = jnp.zeros_like(o_ref)

        e = y_ref[...] - t_ref[...]
        part = jnp.sum(jnp.sum(e * e, axis=1, keepdims=True), axis=0, keepdims=True)
        o_ref[...] += part * (0.5 / D)

    out = pl.pallas_call(
        body,
        out_shape=jax.ShapeDtypeStruct((SUBLANES, LANES), F32),
        grid=(L // tm,),
        in_specs=[pl.BlockSpec((tm, D), lambda i: (i, 0)), pl.BlockSpec((tm, D), lambda i: (i, 0))],
        out_specs=pl.BlockSpec((SUBLANES, LANES), lambda i: (0, 0)),
        compiler_params=_params(("arbitrary",)),
        name="loss_fwd",
    )(y, t)
    return out[0, 0]


def _loss_bwd_call(y, t, g):
    L, D = y.shape
    tm = _rms_rows(L, D)

    def body(y_ref, t_ref, g_ref, o_ref):
        o_ref[...] = (y_ref[...] - t_ref[...]) * (g_ref[...] * (1.0 / D))

    return pl.pallas_call(
        body,
        out_shape=jax.ShapeDtypeStruct((L, D), F32),
        grid=(L // tm,),
        in_specs=[pl.BlockSpec((tm, D), lambda i: (i, 0)), pl.BlockSpec((tm, D), lambda i: (i, 0)),
                  pl.BlockSpec((1, 1), lambda i: (0, 0))],
        out_specs=pl.BlockSpec((tm, D), lambda i: (i, 0)),
        compiler_params=_params(("parallel",)),
        name="loss_bwd",
    )(y, t, g.reshape(1, 1))


@jax.custom_vjp
def loss_head(y, t):
    return _loss_fwd_call(y, t)


def _loss_head_fwd(y, t):
    return _loss_fwd_call(y, t), (y, t)


def _loss_head_bwd(res, g):
    y, t = res
    return _loss_bwd_call(y, t, g), jnp.zeros_like(t)


loss_head.defvjp(_loss_head_fwd, _loss_head_bwd)


def _rows_for(C, nbuf):
    return max(16, ((24 << 20) // (nbuf * 4 * C)) // 16 * 16)


def _pair_add_call(a, b, name):
    R, C = a.shape
    tm = _tile(R, _rows_for(C, 6), 16)

    def body(a_ref, b_ref, o_ref):
        o_ref[...] = (a_ref[...].astype(F32) + b_ref[...].astype(F32)).astype(o_ref.dtype)

    spec = pl.BlockSpec((tm, C), lambda i: (i, 0))
    return pl.pallas_call(
        body, out_shape=jax.ShapeDtypeStruct(a.shape, a.dtype), grid=(R // tm,),
        in_specs=[spec, spec], out_specs=spec, compiler_params=_params(("parallel",)), name=name,
    )(a, b)


def _adamw_call(parts, w, m, v, name):
    P, R, C = parts.shape
    tm = _tile(R, _rows_for(C, 2 * (P + 7)), 16)

    def body(p_ref, w_ref, m_ref, v_ref, g_ref, d_ref, mo_ref, vo_ref):
        g = p_ref[0].astype(F32)
        for s in range(1, P):
            g = g + p_ref[s].astype(F32)
        m2 = ADAM_B1 * m_ref[...] + (1.0 - ADAM_B1) * g
        v2 = ADAM_B2 * v_ref[...] + (1.0 - ADAM_B2) * jnp.square(g)
        m_hat = m2 / (1.0 - ADAM_B1 ** ADAM_STEP)
        v_hat = v2 / (1.0 - ADAM_B2 ** ADAM_STEP)
        g_ref[...] = g
        d_ref[...] = -ADAM_LR * (m_hat / (jnp.sqrt(v_hat) + ADAM_EPS) + ADAM_WD * w_ref[...])
        mo_ref[...] = m2
        vo_ref[...] = v2

    spec = pl.BlockSpec((tm, C), lambda i: (i, 0))
    out = jax.ShapeDtypeStruct((R, C), F32)
    return pl.pallas_call(
        body, out_shape=(out, out, out, out), grid=(R // tm,),
        in_specs=[pl.BlockSpec((P, tm, C), lambda i: (0, i, 0)), spec, spec, spec],
        out_specs=(spec, spec, spec, spec), compiler_params=_params(("parallel",)), name=name,
    )(parts, w, m, v)


_MESH = pl.DeviceIdType.MESH
FLIPS_CHIPS = ((1, 0, 0), (0, 1, 0), (1, 1, 0))
FLIPS_ALL = ((0, 0, 1), (1, 0, 0), (0, 1, 0), (1, 1, 0), (1, 0, 1), (0, 1, 1), (1, 1, 1))
FLIPS_SIBLING = ((0, 0, 1),)


def _exchange(arrays, flips, mode, name):
    n = len(arrays)
    nf = len(flips)
    if flips == FLIPS_SIBLING:
        n_slots, slot = 2, (lambda x, y, c: c)
    elif any(f[2] for f in flips):
        n_slots, slot = 8, (lambda x, y, c: 4 * x + 2 * y + c)
    else:
        n_slots, slot = 4, (lambda x, y, c: 2 * x + y)

    def body(*refs):
        ins, outs = refs[:n], refs[n:2 * n]
        send_sems, recv_sems, local_sems = refs[2 * n:]
        x, y, c = lax.axis_index("x"), lax.axis_index("y"), lax.axis_index("c")
        me = slot(x, y, c)
        peers = [(x + f[0] - 2 * x * f[0], y + f[1] - 2 * y * f[1], c + f[2] - 2 * c * f[2]) for f in flips]

        def copy(i, k, sending):
            px, py, pc = peers[k]
            there = slot(px, py, pc)
            if mode == "swap":
                src, dst = ins[i], outs[i]
            elif mode == "swap_half":
                h = arrays[i].shape[1] // 2
                src, dst = ins[i].at[:, pl.ds(pc * h, h)], outs[i]
            elif mode == "gather":
                src, dst = ins[i], outs[i].at[me if sending else there]
            else:
                src, dst = ins[i].at[there if sending else me], outs[i].at[me if sending else there]
            return pltpu.make_async_remote_copy(src_ref=src, dst_ref=dst, send_sem=send_sems.at[i, k],
                                                recv_sem=recv_sems.at[i, k], device_id=(px, py, pc),
                                                device_id_type=_MESH)

        sends = [copy(i, k, True) for i in range(n) for k in range(nf)]
        for cp in sends:
            cp.start()
        local = []
        if mode in ("gather", "scatter"):
            for i in range(n):
                src = ins[i] if mode == "gather" else ins[i].at[me]
                cp = pltpu.make_async_copy(src, outs[i].at[me], local_sems.at[i])
                cp.start()
                local.append(cp)
        for i in range(n):
            for k in range(nf):
                copy(i, k, False).wait_recv()
        for cp in sends:
            cp.wait_send()
        for cp in local:
            cp.wait()

    if mode == "gather":
        out_shape = [jax.ShapeDtypeStruct((n_slots,) + a.shape, a.dtype) for a in arrays]
    elif mode == "swap_half":
        out_shape = [jax.ShapeDtypeStruct((a.shape[0], a.shape[1] // 2) + a.shape[2:], a.dtype) for a in arrays]
    else:
        out_shape = [jax.ShapeDtypeStruct(a.shape, a.dtype) for a in arrays]
    any_spec = pl.BlockSpec(memory_space=pl.ANY)
    return pl.pallas_call(
        body,
        out_shape=out_shape,
        in_specs=[any_spec] * n,
        out_specs=[any_spec] * n,
        scratch_shapes=[pltpu.SemaphoreType.DMA((n, nf)), pltpu.SemaphoreType.DMA((n, nf)),
                        pltpu.SemaphoreType.DMA((n,))],
        compiler_params=pltpu.CompilerParams(has_side_effects=True),
        name=name,
    )(*arrays)


def _two_level_all_gather(arrays, name):
    n = len(arrays)
    nf = len(FLIPS_CHIPS)

    def body(*refs):
        ins, outs = refs[:n], refs[n:2 * n]
        send_sems, recv_sems, pass_send_sems, pass_recv_sems, local_sems = refs[2 * n:]
        x, y, c = lax.axis_index("x"), lax.axis_index("y"), lax.axis_index("c")
        me = 2 * x + y
        chips = [(x + f[0] - 2 * x * f[0], y + f[1] - 2 * y * f[1]) for f in FLIPS_CHIPS]

        def rows(i, core):
            h = arrays[i].shape[0] // 2
            return pl.ds(core * h, h)

        def fetch(i, k, sending):
            px, py = chips[k]
            dst = outs[i].at[me if sending else 2 * px + py, rows(i, c)]
            return pltpu.make_async_remote_copy(src_ref=ins[i].at[rows(i, c)], dst_ref=dst,
                                                send_sem=send_sems.at[i, k], recv_sem=recv_sems.at[i, k],
                                                device_id=(px, py, c), device_id_type=_MESH)

        def hand_on(i, k, sending):
            px, py = chips[k]
            part = outs[i].at[2 * px + py, rows(i, c if sending else 1 - c)]
            return pltpu.make_async_remote_copy(src_ref=part, dst_ref=part, send_sem=pass_send_sems.at[i, k],
                                                recv_sem=pass_recv_sems.at[i, k], device_id=(x, y, 1 - c),
                                                device_id_type=_MESH)

        pairs = [(i, k) for i in range(n) for k in range(nf)]
        for i, k in pairs:
            fetch(i, k, True).start()
        local = [pltpu.make_async_copy(ins[i], outs[i].at[me], local_sems.at[i]) for i in range(n)]
        for cp in local:
            cp.start()
        for i, k in pairs:
            fetch(i, k, False).wait_recv()
            hand_on(i, k, True).start()
        for i, k in pairs:
            hand_on(i, k, False).wait_recv()
        for i, k in pairs:
            fetch(i, k, True).wait_send()
            hand_on(i, k, True).wait_send()
        for cp in local:
            cp.wait()

    any_spec = pl.BlockSpec(memory_space=pl.ANY)
    sems = pltpu.SemaphoreType.DMA((n, nf))
    return pl.pallas_call(
        body,
        out_shape=[jax.ShapeDtypeStruct((4,) + a.shape, a.dtype) for a in arrays],
        in_specs=[any_spec] * n,
        out_specs=[any_spec] * n,
        scratch_shapes=[sems, sems, sems, sems, pltpu.SemaphoreType.DMA((n,))],
        compiler_params=pltpu.CompilerParams(has_side_effects=True),
        name=name,
    )(*arrays)


def _sum_parts_call(parts, name):
    P, R, C = parts.shape
    tm = _tile(R, _rows_for(C, 2 * (P + 2)), 16)

    def body(p_ref, o_ref):
        g = p_ref[0].astype(F32)
        for s in range(1, P):
            g = g + p_ref[s].astype(F32)
        o_ref[...] = g

    return pl.pallas_call(
        body, out_shape=jax.ShapeDtypeStruct((R, C), F32), grid=(R // tm,),
        in_specs=[pl.BlockSpec((P, tm, C), lambda i: (0, i, 0))], out_specs=pl.BlockSpec((tm, C), lambda i: (i, 0)),
        compiler_params=_params(("parallel",)), name=name,
    )(parts)


WEIGHT_NAMES = ('w_in', 's5_lam_re', 's5_lam_im', 's5_log_step', 's5_b_re', 's5_b_im', 's5_c_re', 's5_c_im', 's5_d',
                's5_w_glu', 's5_b_glu', 'gdn_conv', 'gdn_a_log', 'gdn_dt_bias', 'gdn_o_gain', 'swa_sink', 't5_bias',
                'mla_q_gain', 'mla_kv_gain', 'mla_w_uq', 'mla_w_ukv', 'w_branch', 'w_out', 'mix_pre_gain',
                'mix_post_gain', 'mlp_pre_gain', 'mlp_post_gain', 'w_mlp_in', 'w_mlp_out')
SHARDED = {'w_in': (2, BF16), 's5_w_glu': (1, BF16), 'gdn_conv': (2, F32), 'mla_w_uq': (2, BF16),
           'mla_w_ukv': (2, BF16), 'w_branch': (3, BF16), 'w_out': (1, BF16), 'w_mlp_in': (2, BF16),
           'w_mlp_out': (1, BF16)}
REPLICATED = tuple(n for n in WEIGHT_NAMES if n not in SHARDED)
PACK_COLS = 1024

_IN_A = (('s5_u', 512), ('gdn_qkv', 1536), ('gdn_z', 512), ('gdn_beta', 8), ('gdn_decay', 8))
_IN_B = (('swa_q', 512), ('swa_kv', 256), ('mla_cq', 384), ('mla_ckv', 512), ('mla_kr', 64))
_IN_A_W = sum(w for _, w in _IN_A)
_IN_B_W = sum(w for _, w in _IN_B)
_IN_A_PAD = -_IN_A_W % LANES
_IN_B_PAD = -(_IN_A_W + _IN_A_PAD + _IN_B_W) % 512


def _assemble(g, axis):
    t = jnp.moveaxis(g, 0, axis)
    shape = t.shape[:axis] + (t.shape[axis] * t.shape[axis + 1],) + t.shape[axis + 2:]
    return t.reshape(shape)


def _s5_mixer(u, lam_re, lam_im, log_step, b_re, b_im, c_re, c_im, d_skip, w_glu, b_glu):
    nd, G, P = lam_re.shape
    Hg = b_re.shape[-1]
    lam_re = jnp.minimum(lam_re, -1e-4)
    dt = jnp.exp(log_step)[..., None]
    mag = jnp.exp(lam_re * dt)
    abar_r = mag * jnp.cos(lam_im * dt)
    abar_i = mag * jnp.sin(lam_im * dt)
    den = lam_re * lam_re + lam_im * lam_im
    xr = abar_r - 1.0
    xi = abar_i
    coef_r = (xr * lam_re + xi * lam_im) / den
    coef_i = (xi * lam_re - xr * lam_im) / den
    bbar_r = coef_r[..., None] * b_re - coef_i[..., None] * b_im
    bbar_i = coef_r[..., None] * b_im + coef_i[..., None] * b_re
    eye = jnp.eye(G, dtype=F32)

    def dense_b(bb):
        return jnp.einsum('dgph,gk->ghdkp', bb, eye).reshape(G * Hg, nd, G * P)

    def dense_c(cc):
        return jnp.einsum('dghp,gk->khdgp', cc, eye).reshape(G * Hg, nd, G * P)

    b_cat = _to_scan_cols(jnp.stack([dense_b(bbar_r), dense_b(bbar_i)], axis=2), nd)
    c_cat = _to_scan_cols(jnp.stack([dense_c(c_re), -dense_c(c_im)], axis=2), nd).T
    a_row = _to_scan_cols(jnp.stack([abar_r.reshape(nd, G * P), abar_i.reshape(nd, G * P)], axis=1), nd)[None]
    s = s5_scan(linear(u, b_cat, 's5_in'), a_row, nd)
    y = linear(s, c_cat, 's5_out') + d_skip * u
    y = jax.nn.gelu(y)
    return y * jax.nn.sigmoid(linear(y, w_glu, 's5_glu') + b_glu)


def _gdn_mixer(qkv, z, beta_logits, decay_logits, conv_w, a_log, dt_bias, o_gain):
    L = qkv.shape[0]
    Dh = GDN_HEAD_DIM
    H = z.shape[1] // Dh
    C = GDN_CHUNK
    NC = L // C
    xp = jnp.pad(qkv, ((GDN_CONV // 2, GDN_CONV - 1 - GDN_CONV // 2), (0, 0)))
    conv = xp[0:L] * conv_w[0]
    for j in range(1, GDN_CONV):
        conv = conv + xp[j:j + L] * conv_w[j]
    q, k, v = jnp.split(jax.nn.silu(conv), 3, axis=-1)

    def l2n(t):
        return t * lax.rsqrt(jnp.sum(t * t, axis=-1, keepdims=True) + 1e-6)

    q = (l2n(q.reshape(L, H, Dh)) * (Dh ** -0.5)).reshape(L, H * Dh)
    k = l2n(k.reshape(L, H, Dh)).reshape(L, H * Dh)
    beta = jax.nn.sigmoid(beta_logits).reshape(L, 2, H)
    g = -jnp.exp(a_log) * jax.nn.softplus(decay_logits.reshape(L, 2, H) + dt_bias)

    def per_chunk(t, d):
        return t[:, d].T.reshape(H, NC, C)

    g_with = jnp.cumsum(per_chunk(g, 0), axis=-1)
    g_against = jnp.cumsum(per_chunk(g, 1)[..., ::-1], axis=-1)[..., ::-1]
    gs = jnp.concatenate([g_with, g_against], axis=0)
    totals = jnp.concatenate([g_with[..., -1], g_against[..., 0]], axis=0)
    betas = jnp.concatenate([per_chunk(beta, 0), per_chunk(beta, 1)], axis=0)
    o_with, o_against = gated_delta_rule(q, k, v, gs[..., None], gs[:, :, None, :], totals[..., None, None],
                                         betas[..., None])
    o = o_with + o_against
    o = rms_norm(o.reshape(L * H, Dh), o_gain, 'gdn_onorm').reshape(L, H, Dh)
    o = o * jax.nn.silu(z.reshape(L, H, Dh))
    return o.reshape(L, H * Dh)


def _t5_bucket(rel):
    nb = T5_BUCKETS // 2
    max_exact = nb // 2
    ret = jnp.where(rel > 0, nb, 0)
    n = jnp.abs(rel)
    nf = jnp.maximum(n, 1).astype(F32)
    large = max_exact + (jnp.log(nf / max_exact) / math.log(T5_MAX_DISTANCE / max_exact)
                         * (nb - max_exact)).astype(jnp.int32)
    large = jnp.minimum(large, nb - 1)
    return ret + jnp.where(n < max_exact, n, large)


def _swa_mixer(q, kv, sink, t5_bias):
    L = q.shape[0]
    B = SWA_BLOCK
    HQ = q.shape[1] // SWA_HEAD_DIM
    HKV = HQ // SWA_KV_GROUP
    qh = q.reshape(L, HQ, SWA_HEAD_DIM).transpose(1, 0, 2)
    k, v = jnp.split(kv, 2, axis=-1)

    def heads_padded(t):
        return jnp.pad(t.reshape(L, HKV, SWA_HEAD_DIM).transpose(1, 0, 2), ((0, 0), (B, B), (0, 0)))

    qi = jnp.arange(B)[:, None]
    sj = jnp.arange(3 * B)[None, :]
    one_hot = (_t5_bucket(sj - B - qi)[..., None] == jnp.arange(T5_BUCKETS)).astype(F32)
    bias = jnp.einsum('qsb,bh->hqs', one_hot, t5_bias, precision=_HIGHEST)
    sink_rows = jnp.broadcast_to(sink[:, None, None], (HQ, B, 1))
    o = window_attention(qh, heads_padded(k), heads_padded(v), bias, sink_rows)
    return o.transpose(1, 0, 2).reshape(L, HQ * SWA_HEAD_DIM)


def _apply_rope(x, cos, sin):
    x1, x2 = jnp.split(x, 2, axis=-1)
    return jnp.concatenate([x1 * cos - x2 * sin, x2 * cos + x1 * sin], axis=-1)


def _mla_mixer(c_q, c_kv, k_rope, q_gain, kv_gain, w_uq, w_ukv):
    L = c_q.shape[0]
    H = w_uq.shape[1] // (MLA_NOPE + MLA_ROPE)
    q = linear(rms_norm(c_q, q_gain, 'mla_qnorm'), w_uq, 'mla_uq').reshape(L, H, MLA_NOPE + MLA_ROPE)
    kv = linear(rms_norm(c_kv, kv_gain, 'mla_kvnorm'), w_ukv, 'mla_ukv').reshape(L, H, MLA_NOPE + MLA_V)
    q_nope, q_pe = q[..., :MLA_NOPE], q[..., MLA_NOPE:]
    k_nope, v = kv[..., :MLA_NOPE], kv[..., MLA_NOPE:]
    pos = jnp.arange(L, dtype=F32)
    inv_freq = ROPE_THETA ** (-jnp.arange(0, MLA_ROPE, 2, dtype=F32) / MLA_ROPE)
    ang = pos[:, None] * inv_freq[None, :]
    cos, sin = jnp.cos(ang)[:, None, :], jnp.sin(ang)[:, None, :]
    q_pe = _apply_rope(q_pe, cos, sin)
    k_pe = _apply_rope(k_rope[:, None, :], cos, sin)
    qf = jnp.concatenate([q_nope, q_pe], axis=-1).transpose(1, 0, 2)
    kf = jnp.concatenate([k_nope, jnp.broadcast_to(k_pe, (L, H, MLA_ROPE))], axis=-1).transpose(1, 0, 2)
    o = full_attention(qf, kf, v.transpose(1, 0, 2), (MLA_NOPE + MLA_ROPE) ** -0.5)
    return o.transpose(1, 0, 2).reshape(L, H * MLA_V)


def _split_cols(t, segments, start):
    out = {}
    for name, width in segments:
        out[name] = t[:, start:start + width]
        start += width
    return out, start


def _local_loss(weights, x, target):
    p = {n: (_assemble(weights[n], SHARDED[n][0]) if n in SHARDED else weights[n]) for n in WEIGHT_NAMES}
    L, D = x.shape
    depth = p['w_in'].shape[0]
    for l in range(depth):
        w_in = p['w_in'][l]
        zeros = lambda n: jnp.zeros((D, n), w_in.dtype)
        w_r = jnp.concatenate([w_in[:, :_IN_A_W], zeros(_IN_A_PAD), w_in[:, _IN_A_W:_IN_A_W + _IN_B_W],
                               zeros(_IN_B_PAD)], axis=1)
        w_g = w_in[:, _IN_A_W + _IN_B_W:]
        h = rms_norm(x, p['mix_pre_gain'][l], 'mix_pre')
        proj = linear(h, w_r, 'in_proj')
        gate_logits = linear(h, w_g, 'in_gate')
        seg, end = _split_cols(proj, _IN_A, 0)
        seg_b, _ = _split_cols(proj, _IN_B, end + _IN_A_PAD)
        seg.update(seg_b)
        y_a = _s5_mixer(seg['s5_u'], p['s5_lam_re'][l], p['s5_lam_im'][l], p['s5_log_step'][l], p['s5_b_re'][l],
                        p['s5_b_im'][l], p['s5_c_re'][l], p['s5_c_im'][l], p['s5_d'][l], p['s5_w_glu'][l],
                        p['s5_b_glu'][l])
        y_b = _gdn_mixer(seg['gdn_qkv'], seg['gdn_z'], seg['gdn_beta'], seg['gdn_decay'], p['gdn_conv'][l],
                         p['gdn_a_log'][l], p['gdn_dt_bias'][l], p['gdn_o_gain'][l])
        y_c = _swa_mixer(seg['swa_q'], seg['swa_kv'], p['swa_sink'][l], p['t5_bias'])
        y_d = _mla_mixer(seg['mla_cq'], seg['mla_ckv'], seg['mla_kr'], p['mla_q_gain'][l], p['mla_kv_gain'][l],
                         p['mla_w_uq'][l], p['mla_w_ukv'][l])
        branches = tuple(linear(y, p['w_branch'][l, b], 'branch') for b, y in enumerate((y_a, y_b, y_c, y_d)))
        merged = gate_merge(gate_logits, branches)
        x = x + rms_norm(linear(merged, p['w_out'][l], 'mix_out'), p['mix_post_gain'][l], 'mix_post')
        h = rms_norm(x, p['mlp_pre_gain'][l], 'mlp_pre')
        f = relu2_linear(linear(h, p['w_mlp_in'][l], 'mlp_in'), p['w_mlp_out'][l])
        x = x + rms_norm(f, p['mlp_post_gain'][l], 'mlp_post')
    return loss_head(x, target)


def _two_d(t, lead):
    return t.reshape(t.shape[:lead] + (-1, t.shape[-1]))


def _pack(arrays):
    flat = jnp.concatenate([a.reshape(-1) for a in arrays])
    pad = -flat.shape[0] % (16 * PACK_COLS)
    return jnp.pad(flat, (0, pad)).reshape(-1, PACK_COLS)


def _unpack(packed, like):
    flat = packed.reshape(-1)
    out, pos = [], 0
    for a in like:
        out.append(flat[pos:pos + a.size].reshape(a.shape))
        pos += a.size
    return out


def kernel(x, w_in, s5_lam_re, s5_lam_im, s5_log_step, s5_b_re, s5_b_im, s5_c_re, s5_c_im, s5_d, s5_w_glu, s5_b_glu, gdn_conv, gdn_a_log, gdn_dt_bias, gdn_o_gain, swa_sink, t5_bias, mla_q_gain, mla_kv_gain, mla_w_uq, mla_w_ukv, w_branch, w_out, mix_pre_gain, mix_post_gain, mlp_pre_gain, mlp_post_gain, w_mlp_in, w_mlp_out, loss_target, m_w_in, m_s5_lam_re, m_s5_lam_im, m_s5_log_step, m_s5_b_re, m_s5_b_im, m_s5_c_re, m_s5_c_im, m_s5_d, m_s5_w_glu, m_s5_b_glu, m_gdn_conv, m_gdn_a_log, m_gdn_dt_bias, m_gdn_o_gain, m_swa_sink, m_t5_bias, m_mla_q_gain, m_mla_kv_gain, m_mla_w_uq, m_mla_w_ukv, m_w_branch, m_w_out, m_mix_pre_gain, m_mix_post_gain, m_mlp_pre_gain, m_mlp_post_gain, m_w_mlp_in, m_w_mlp_out, v_w_in, v_s5_lam_re, v_s5_lam_im, v_s5_log_step, v_s5_b_re, v_s5_b_im, v_s5_c_re, v_s5_c_im, v_s5_d, v_s5_w_glu, v_s5_b_glu, v_gdn_conv, v_gdn_a_log, v_gdn_dt_bias, v_gdn_o_gain, v_swa_sink, v_t5_bias, v_mla_q_gain, v_mla_kv_gain, v_mla_w_uq, v_mla_w_ukv, v_w_branch, v_w_out, v_mix_pre_gain, v_mix_post_gain, v_mlp_pre_gain, v_mlp_post_gain, v_w_mlp_in, v_w_mlp_out):
    w = dict(zip(WEIGHT_NAMES, (w_in, s5_lam_re, s5_lam_im, s5_log_step, s5_b_re, s5_b_im, s5_c_re, s5_c_im, s5_d, s5_w_glu, s5_b_glu, gdn_conv, gdn_a_log, gdn_dt_bias, gdn_o_gain, swa_sink, t5_bias, mla_q_gain, mla_kv_gain, mla_w_uq, mla_w_ukv, w_branch, w_out, mix_pre_gain, mix_post_gain, mlp_pre_gain, mlp_post_gain, w_mlp_in, w_mlp_out)))
    m = dict(zip(WEIGHT_NAMES, (m_w_in, m_s5_lam_re, m_s5_lam_im, m_s5_log_step, m_s5_b_re, m_s5_b_im, m_s5_c_re, m_s5_c_im, m_s5_d, m_s5_w_glu, m_s5_b_glu, m_gdn_conv, m_gdn_a_log, m_gdn_dt_bias, m_gdn_o_gain, m_swa_sink, m_t5_bias, m_mla_q_gain, m_mla_kv_gain, m_mla_w_uq, m_mla_w_ukv, m_w_branch, m_w_out, m_mix_pre_gain, m_mix_post_gain, m_mlp_pre_gain, m_mlp_post_gain, m_w_mlp_in, m_w_mlp_out)))
    v = dict(zip(WEIGHT_NAMES, (v_w_in, v_s5_lam_re, v_s5_lam_im, v_s5_log_step, v_s5_b_re, v_s5_b_im, v_s5_c_re, v_s5_c_im, v_s5_d, v_s5_w_glu, v_s5_b_glu, v_gdn_conv, v_gdn_a_log, v_gdn_dt_bias, v_gdn_o_gain, v_swa_sink, v_t5_bias, v_mla_q_gain, v_mla_kv_gain, v_mla_w_uq, v_mla_w_ukv, v_w_branch, v_w_out, v_mix_pre_gain, v_mix_post_gain, v_mlp_pre_gain, v_mlp_post_gain, v_w_mlp_in, v_w_mlp_out)))
    sharded = tuple(SHARDED)

    gathered = _two_level_all_gather([w[n].astype(SHARDED[n][1]) for n in sharded], 'weights_all_gather')
    weights = dict(zip(sharded, gathered))
    weights.update({n: w[n] for n in REPLICATED})

    loss, (grads, grad_x) = jax.value_and_grad(_local_loss, argnums=(0, 1))(weights, x[0], loss_target[0])
    loss = lax.psum(loss, ('x', 'y', 'c'))

    core = lax.axis_index('c')
    from_sibling = _exchange([grads[n] for n in sharded], FLIPS_SIBLING, 'swap_half', 'grads_core_swap')
    chip_sums = []
    for n, r in zip(sharded, from_sibling):
        mine = lax.dynamic_slice_in_dim(grads[n], core * r.shape[1], r.shape[1], axis=1)
        chip_sums.append(_pair_add_call(_two_d(mine, 0), _two_d(r, 0), 'grads_core_add').reshape(r.shape))
    per_chip = _exchange(chip_sums, FLIPS_CHIPS, 'scatter', 'grads_chip_scatter')
    halves = [_sum_parts_call(_two_d(parts, 1), 'grads_chip_sum').reshape(parts.shape[1:]) for parts in per_chip]
    both_halves = _exchange(halves, FLIPS_SIBLING, 'gather', 'grads_core_gather')
    out = {}
    for n, g in zip(sharded, both_halves):
        res = _adamw_call(_two_d(g, 0)[None], _two_d(w[n], 0), _two_d(m[n], 0), _two_d(v[n], 0), 'adamw_sharded')
        out[n] = tuple(r.reshape(w[n].shape) for r in res)

    small = [grads[n] for n in REPLICATED]
    all_parts = _exchange([_pack(small)], FLIPS_ALL, 'gather', 'grads_all_gather')[0]
    res = _adamw_call(all_parts, _pack([w[n] for n in REPLICATED]), _pack([m[n] for n in REPLICATED]),
                      _pack([v[n] for n in REPLICATED]), 'adamw_replicated')
    unpacked = [_unpack(r, small) for r in res]
    for i, n in enumerate(REPLICATED):
        out[n] = tuple(u[i] for u in unpacked)

    return (loss, grad_x[None]) + tuple(out[n][k] for k in range(4) for n in WEIGHT_NAMES)
```

```python
import functools
import math

import numpy as np
import jax
import jax.numpy as jnp
from jax import lax
from jax.experimental import pallas as pl
from jax.experimental.pallas import tpu as pltpu

F32 = jnp.float32
BF16 = jnp.bfloat16

VMEM_LIMIT_BYTES = 56 * 1024 * 1024
SUBLANES = 8
LANES = 128

NORM_EPS = 1e-6
DEPTH = 4
N_BRANCHES = 4
S5_GROUP = 16
S5_STATE = 64
GDN_HEAD_DIM = 128
GDN_CONV = 4
GDN_CHUNK = 64
SWA_HEAD_DIM = 64
SWA_KV_GROUP = 4
WINDOW = 128
SWA_BLOCK = 128
T5_BUCKETS = 32
T5_MAX_DISTANCE = 128
MLA_NOPE = 128
MLA_ROPE = 64
MLA_V = 128
ROPE_THETA = 10000.0

ADAM_LR = 0.001
ADAM_B1 = 0.9
ADAM_B2 = 0.999
ADAM_EPS = 1e-08
ADAM_WD = 0.01
ADAM_STEP = 10

_HIGHEST = lax.Precision.HIGHEST


def _params(sem, vmem=VMEM_LIMIT_BYTES):
    return pltpu.CompilerParams(dimension_semantics=sem, vmem_limit_bytes=vmem)


def _tile(dim, target, align):
    if dim <= target:
        return dim
    t = (target // align) * align
    while t >= align:
        if dim % t == 0:
            return t
        t -= align
    return dim


_DOT_DIMS = {"nn": ((1,), (0,)), "nt": ((1,), (1,)), "tn": ((0,), (0,))}


MM_OPERAND_TILE_BYTES = 12 * 1024 * 1024


def _mm(a, b, form="nn", out_dtype=F32, tm=1024, tn=1024, tk=None, relu2_a=False, relu2_grad_of=None, name="mm"):
    if form == "nn":
        (M, K), (K2, N) = a.shape, b.shape
    elif form == "nt":
        (M, K), (N, K2) = a.shape, b.shape
    else:
        (K, M), (K2, N) = a.shape, b.shape
    assert K == K2, (a.shape, b.shape, form)
    tm = _tile(M, tm, LANES if form == "tn" else 16)
    tn = _tile(N, tn, LANES)
    if tk is None:
        tk = MM_OPERAND_TILE_BYTES // (tm * a.dtype.itemsize + tn * b.dtype.itemsize)
    tk = _tile(K, max(LANES, tk // LANES * LANES), LANES)
    nk = K // tk
    dims = (_DOT_DIMS[form], ((), ()))
    n_in = 2 if relu2_grad_of is None else 3

    def dot(a_ref, b_ref):
        av = a_ref[...]
        if relu2_a:
            av = jnp.square(jnp.maximum(av, 0.0))
        return lax.dot_general(av.astype(BF16), b_ref[...].astype(BF16), dims, preferred_element_type=F32)

    def finish(acc, refs):
        if relu2_grad_of is not None:
            acc = acc * (2.0 * jnp.maximum(refs[2][...], 0.0))
        refs[n_in][...] = acc.astype(refs[n_in].dtype)

    def body_single(*refs):
        finish(dot(refs[0], refs[1]), refs)

    def body_acc(*refs):
        acc_ref = refs[n_in + 1]
        k = pl.program_id(2)

        @pl.when(k == 0)
        def _():
            acc_ref[...] = jnp.zeros_like(acc_ref)

        acc_ref[...] += dot(refs[0], refs[1])

        @pl.when(k == nk - 1)
        def _():
            finish(acc_ref[...], refs)

    body = body_single if nk == 1 else body_acc
    extra = [] if relu2_grad_of is None else [relu2_grad_of]
    extra_specs = [pl.BlockSpec((tm, tn), lambda i, j, k: (i, j))] * len(extra)

    if form == "nn":
        a_spec = pl.BlockSpec((tm, tk), lambda i, j, k: (i, k))
        b_spec = pl.BlockSpec((tk, tn), lambda i, j, k: (k, j))
    elif form == "nt":
        a_spec = pl.BlockSpec((tm, tk), lambda i, j, k: (i, k))
        b_spec = pl.BlockSpec((tn, tk), lambda i, j, k: (j, k))
    else:
        a_spec = pl.BlockSpec((tk, tm), lambda i, j, k: (k, i))
        b_spec = pl.BlockSpec((tk, tn), lambda i, j, k: (k, j))
    return pl.pallas_call(
        body,
        out_shape=jax.ShapeDtypeStruct((M, N), out_dtype),
        grid=(M // tm, N // tn, nk),
        in_specs=[a_spec, b_spec] + extra_specs,
        out_specs=pl.BlockSpec((tm, tn), lambda i, j, k: (i, j)),
        scratch_shapes=[] if nk == 1 else [pltpu.VMEM((tm, tn), F32)],
        compiler_params=_params(("parallel", "parallel", "arbitrary")),
        name=name,
    )(a, b, *extra)


@jax.custom_vjp
def relu2_linear(x, w):
    return _mm(x, w, "nn", relu2_a=True, name="mlp_out_fwd")


def _relu2_linear_fwd(x, w):
    return _mm(x, w, "nn", relu2_a=True, name="mlp_out_fwd"), (x, w)


def _relu2_linear_bwd(res, dy):
    x, w = res
    dx = _mm(dy, w, "nt", relu2_grad_of=x, name="mlp_out_dx")
    dw = _mm(x, dy, "tn", out_dtype=w.dtype, relu2_a=True, name="mlp_out_dw")
    return dx, dw


relu2_linear.defvjp(_relu2_linear_fwd, _relu2_linear_bwd)


@functools.partial(jax.custom_vjp, nondiff_argnums=(2,))
def linear(x, w, name):
    return _mm(x, w, "nn", name=name + "_fwd")


def _linear_fwd(x, w, name):
    return _mm(x, w, "nn", name=name + "_fwd"), (x, w)


def _linear_bwd(name, res, dy):
    x, w = res
    dx = _mm(dy, w, "nt", out_dtype=x.dtype, name=name + "_dx")
    dw = _mm(x, dy, "tn", out_dtype=w.dtype, name=name + "_dw")
    return dx, dw


linear.defvjp(_linear_fwd, _linear_bwd)


def _rms_rows(L, D):
    return _tile(L, max(SUBLANES, (1 << 20) // D), SUBLANES)


def _rms_fwd_call(x, gain, name):
    L, D = x.shape
    tm = _rms_rows(L, D)

    def body(x_ref, g_ref, o_ref):
        xv = x_ref[...]
        r = lax.rsqrt(jnp.mean(xv * xv, axis=-1, keepdims=True) + NORM_EPS)
        o_ref[...] = xv * r * g_ref[...]

    return pl.pallas_call(
        body,
        out_shape=jax.ShapeDtypeStruct((L, D), F32),
        grid=(L // tm,),
        in_specs=[pl.BlockSpec((tm, D), lambda i: (i, 0)), pl.BlockSpec((1, D), lambda i: (0, 0))],
        out_specs=pl.BlockSpec((tm, D), lambda i: (i, 0)),
        compiler_params=_params(("parallel",)),
        name=name,
    )(x, gain.reshape(1, D))


def _rms_bwd_call(x, gain, dy, name):
    L, D = x.shape
    tm = _rms_rows(L, D)

    def body(x_ref, g_ref, dy_ref, dx_ref, dg_ref):
        xv = x_ref[...]
        dyv = dy_ref[...]
        r = lax.rsqrt(jnp.mean(xv * xv, axis=-1, keepdims=True) + NORM_EPS)
        xh = xv * r
        dyg = dyv * g_ref[...]
        dx_ref[...] = r * (dyg - xh * jnp.mean(dyg * xh, axis=-1, keepdims=True))

        @pl.when(pl.program_id(0) == 0)
        def _():
            dg_ref[...] = jnp.zeros_like(dg_ref)

        dg_ref[...] += jnp.sum(dyv * xh, axis=0, keepdims=True)

    return pl.pallas_call(
        body,
        out_shape=(jax.ShapeDtypeStruct((L, D), F32), jax.ShapeDtypeStruct((1, D), F32)),
        grid=(L // tm,),
        in_specs=[pl.BlockSpec((tm, D), lambda i: (i, 0)), pl.BlockSpec((1, D), lambda i: (0, 0)),
                  pl.BlockSpec((tm, D), lambda i: (i, 0))],
        out_specs=(pl.BlockSpec((tm, D), lambda i: (i, 0)), pl.BlockSpec((1, D), lambda i: (0, 0))),
        compiler_params=_params(("arbitrary",)),
        name=name,
    )(x, gain.reshape(1, D), dy)


@functools.partial(jax.custom_vjp, nondiff_argnums=(2,))
def rms_norm(x, gain, name):
    return _rms_fwd_call(x, gain, name + "_fwd")


def _rms_norm_fwd(x, gain, name):
    return _rms_fwd_call(x, gain, name + "_fwd"), (x, gain)


def _rms_norm_bwd(name, res, dy):
    x, gain = res
    dx, dg = _rms_bwd_call(x, gain, dy, name + "_bwd")
    return dx, dg.reshape(gain.shape)


rms_norm.defvjp(_rms_norm_fwd, _rms_norm_bwd)


SCAN_COLS = 512
SCAN_ROWS = 256
SCAN_UNROLL = 8


def _to_scan_cols(t, nd):
    P = t.shape[-1]
    pc = _tile(P, SCAN_COLS, LANES)
    lead = t.shape[:-3]
    t = t.reshape(lead + (nd, 2, P // pc, pc))
    t = jnp.swapaxes(t, -3, -2)
    return t.reshape(lead + (nd * 2 * P,))


def _scan_call(b, a, nd, reverse_dir0, name):
    L, NC = b.shape
    P = NC // (2 * nd)
    pc = _tile(P, SCAN_COLS, LANES)
    ncb = P // pc
    T = _tile(L, SCAN_ROWS, SUBLANES)
    nt = L // T
    rev0 = 1 if reverse_dir0 else 0
    G = SUBLANES

    def rev_of(d):
        return d + rev0 - 2 * d * rev0

    a3 = a.reshape(nd * ncb, 2, pc)
    cmul = lambda x, y: (x[0] * y[0] - x[1] * y[1], x[0] * y[1] + x[1] * y[0])
    pows = [(a3[:, 0], a3[:, 1])]
    for _ in range(G - 1):
        pows.append(cmul(pows[-1], pows[0]))
    cols = lambda t: jnp.stack(t, axis=1).reshape(nd, NC // nd)
    steps = jnp.stack([cols(pows[0]), cols(pows[1]), cols(pows[3])], axis=0).reshape(3, NC)
    with_time = jnp.stack([cols(p) for p in pows], axis=0)
    carry_pows = jnp.concatenate([with_time[::-1, d] if rev_of(d) else with_time[:, d] for d in range(nd)], axis=1)

    def body(b_ref, step_ref, pow_ref, o_ref, carry_ref):
        rv = rev_of(pl.program_id(0))

        @pl.when(pl.program_id(2) == 0)
        def _():
            carry_ref[...] = jnp.zeros_like(carry_ref)

        row = lax.broadcasted_iota(jnp.int32, (G, pc), 0)
        mults = [(step_ref[j:j + 1, :pc], step_ref[j:j + 1, pc:]) for j in range(3)]
        pr, pi = pow_ref[:, :pc], pow_ref[:, pc:]

        def run(backwards):
            def shifted(x, k):
                if backwards:
                    return jnp.where(row < G - k, pltpu.roll(x, G - k, 0), 0.0)
                return jnp.where(row >= k, pltpu.roll(x, k, 0), 0.0)

            def group(g, carry):
                cr, ci = carry
                r0 = pl.multiple_of((T // G - 1 - g if backwards else g) * G, G)
                x = b_ref[pl.ds(r0, G), :]
                xr, xi = x[:, :pc], x[:, pc:]
                for j, (mr, mi) in enumerate(mults):
                    sr, si = shifted(xr, 1 << j), shifted(xi, 1 << j)
                    xr, xi = xr + (mr * sr - mi * si), xi + (mr * si + mi * sr)
                o_ref[pl.ds(r0, G), :] = jnp.concatenate([xr + (pr * cr - pi * ci), xi + (pr * ci + pi * cr)], axis=1)
                last = o_ref[pl.ds(r0 + (0 if backwards else G - 1), 1), :]
                return last[:, :pc], last[:, pc:]

            cr, ci = lax.fori_loop(0, T // G, group, (carry_ref[:, :pc], carry_ref[:, pc:]), unroll=SCAN_UNROLL)
            carry_ref[...] = jnp.concatenate([cr, ci], axis=1)

        pl.when(rv == 0)(functools.partial(run, False))
        pl.when(rv == 1)(functools.partial(run, True))

    def t_idx(d, i):
        return i + rev_of(d) * (nt - 1 - 2 * i)

    spec = pl.BlockSpec((T, 2 * pc), lambda d, c, i: (t_idx(d, i), d * ncb + c))
    return pl.pallas_call(
        body,
        out_shape=jax.ShapeDtypeStruct(b.shape, F32),
        grid=(nd, ncb, nt),
        in_specs=[spec, pl.BlockSpec((3, 2 * pc), lambda d, c, i: (0, d * ncb + c)),
                  pl.BlockSpec((G, 2 * pc), lambda d, c, i: (0, d * ncb + c))],
        out_specs=spec,
        scratch_shapes=[pltpu.VMEM((1, 2 * pc), F32)],
        compiler_params=_params(("arbitrary", "arbitrary", "arbitrary")),
        name=name,
    )(b, steps, carry_pows)


def _scan_da_call(g, sp, nd, name):
    L, NC = g.shape
    P = NC // (2 * nd)
    pc = _tile(P, SCAN_COLS, LANES)
    T = _tile(L, SCAN_ROWS * 2, SUBLANES)

    def body(g_ref, s_ref, o_ref):
        @pl.when(pl.program_id(1) == 0)
        def _():
            o_ref[...] = jnp.zeros_like(o_ref)

        gr, gi = g_ref[:, :pc], g_ref[:, pc:]
        sr, si = s_ref[:, :pc], s_ref[:, pc:]
        dar = jnp.sum(gr * sr + gi * si, axis=0, keepdims=True)
        dai = jnp.sum(gi * sr - gr * si, axis=0, keepdims=True)
        o_ref[...] += jnp.concatenate([dar, dai], axis=1)

    spec = pl.BlockSpec((T, 2 * pc), lambda c, i: (i, c))
    return pl.pallas_call(
        body,
        out_shape=jax.ShapeDtypeStruct((1, NC), F32),
        grid=(NC // (2 * pc), L // T),
        in_specs=[spec, spec],
        out_specs=pl.BlockSpec((1, 2 * pc), lambda c, i: (0, c)),
        compiler_params=_params(("arbitrary", "arbitrary")),
        name=name,
    )(g, sp)


def _conj_cols(a, nd):
    P = a.shape[-1] // (2 * nd)
    pc = _tile(P, SCAN_COLS, LANES)
    sign = jnp.tile(jnp.concatenate([jnp.ones((pc,), F32), -jnp.ones((pc,), F32)]), a.shape[-1] // (2 * pc))
    return a * sign


@functools.partial(jax.custom_vjp, nondiff_argnums=(2,))
def s5_scan(b, a, nd):
    return _scan_call(b, a, nd, False, "s5_scan_fwd")


def _s5_scan_fwd(b, a, nd):
    s = _scan_call(b, a, nd, False, "s5_scan_fwd")
    return s, (s, a)


def _s5_scan_bwd(nd, res, ds):
    s, a = res
    g = _scan_call(ds, _conj_cols(a, nd), nd, True, "s5_scan_adj")
    L, NC = s.shape
    half = NC // nd
    zero = jnp.zeros((1, half), F32)
    prev = [jnp.concatenate([zero, s[:-1, :half]], axis=0)]
    if nd == 2:
        prev.append(jnp.concatenate([s[1:, half:], zero], axis=0))
    sp = jnp.concatenate(prev, axis=1)
    da = _scan_da_call(g, sp, nd, "s5_scan_da")
    return g, da


s5_scan.defvjp(_s5_scan_fwd, _s5_scan_bwd)


LOG2E = math.log2(math.e)
ATTN_TQ = 1024
ATTN_TK = 1024
ATTN_BWD_TQ = 512
ATTN_BWD_TK = 512


def _attn_fwd_call(q, k, v, scale):
    H, L, DQ = q.shape
    DV = v.shape[-1]
    tq = _tile(L, ATTN_TQ, LANES)
    tk = _tile(L, ATTN_TK, LANES)
    nk = L // tk

    def body(q_ref, k_ref, v_ref, o_ref, lse_ref, m_s, l_s, acc_s):
        j = pl.program_id(2)

        @pl.when(j == 0)
        def _():
            m_s[...] = jnp.full_like(m_s, -jnp.inf)
            l_s[...] = jnp.zeros_like(l_s)
            acc_s[...] = jnp.zeros_like(acc_s)

        s = lax.dot_general(q_ref[0].astype(BF16), k_ref[0].astype(BF16), (((1,), (1,)), ((), ())),
                            preferred_element_type=F32) * (scale * LOG2E)
        m_old = m_s[...]
        m_new = jnp.maximum(m_old, jnp.max(s, axis=1, keepdims=True))
        alpha = jnp.exp2(m_old - m_new)
        p = jnp.exp2(s - m_new)
        l_s[...] = alpha * l_s[...] + jnp.sum(p, axis=1, keepdims=True)
        acc_s[...] = alpha * acc_s[...] + jnp.dot(p.astype(BF16), v_ref[0].astype(BF16), preferred_element_type=F32)
        m_s[...] = m_new

        @pl.when(j == nk - 1)
        def _():
            o_ref[0] = acc_s[...] / l_s[...]
            lse_ref[0] = m_s[...] + jnp.log(l_s[...]) * LOG2E

    return pl.pallas_call(
        body,
        out_shape=(jax.ShapeDtypeStruct((H, L, DV), F32), jax.ShapeDtypeStruct((H, L, 1), F32)),
        grid=(H, L // tq, nk),
        in_specs=[pl.BlockSpec((1, tq, DQ), lambda h, i, j: (h, i, 0)),
                  pl.BlockSpec((1, tk, DQ), lambda h, i, j: (h, j, 0)),
                  pl.BlockSpec((1, tk, DV), lambda h, i, j: (h, j, 0))],
        out_specs=(pl.BlockSpec((1, tq, DV), lambda h, i, j: (h, i, 0)),
                   pl.BlockSpec((1, tq, 1), lambda h, i, j: (h, i, 0))),
        scratch_shapes=[pltpu.VMEM((tq, 1), F32), pltpu.VMEM((tq, 1), F32), pltpu.VMEM((tq, DV), F32)],
        compiler_params=_params(("parallel", "parallel", "arbitrary")),
        name="mla_attn_fwd",
    )(q, k, v)


def _attn_bwd_call(q, k, v, do, lse, delta, scale):
    H, L, DQ = q.shape
    DV = v.shape[-1]
    tq = _tile(L, ATTN_BWD_TQ, LANES)
    tk = _tile(L, ATTN_BWD_TK, LANES)
    nq = L // tq

    def body(q_ref, k_ref, v_ref, do_ref, lse_ref, dl_ref, dq_ref, dk_ref, dv_ref, dk_s, dv_s):
        j = pl.program_id(1)
        i = pl.program_id(2)

        @pl.when(jnp.logical_and(j == 0, i == 0))
        def _():
            dq_ref[...] = jnp.zeros_like(dq_ref)

        @pl.when(i == 0)
        def _():
            dk_s[...] = jnp.zeros_like(dk_s)
            dv_s[...] = jnp.zeros_like(dv_s)

        qb = q_ref[0].astype(BF16)
        kb = k_ref[0].astype(BF16)
        dob = do_ref[0].astype(BF16)
        s = lax.dot_general(qb, kb, (((1,), (1,)), ((), ())), preferred_element_type=F32) * (scale * LOG2E)
        p = jnp.exp2(s - lse_ref[0])
        dv_s[...] += lax.dot_general(p.astype(BF16), dob, (((0,), (0,)), ((), ())), preferred_element_type=F32)
        dp = lax.dot_general(dob, v_ref[0].astype(BF16), (((1,), (1,)), ((), ())), preferred_element_type=F32)
        ds = (p * (dp - dl_ref[0]) * scale).astype(BF16)
        dk_s[...] += lax.dot_general(ds, qb, (((0,), (0,)), ((), ())), preferred_element_type=F32)
        rows = pl.ds(pl.multiple_of(i * tq, tq), tq)
        dq_ref[0, rows, :] += jnp.dot(ds, kb, preferred_element_type=F32)

        @pl.when(i == nq - 1)
        def _():
            dk_ref[0] = dk_s[...]
            dv_ref[0] = dv_s[...]

    return pl.pallas_call(
        body,
        out_shape=(jax.ShapeDtypeStruct((H, L, DQ), F32), jax.ShapeDtypeStruct((H, L, DQ), F32),
                   jax.ShapeDtypeStruct((H, L, DV), F32)),
        grid=(H, L // tk, nq),
        in_specs=[pl.BlockSpec((1, tq, DQ), lambda h, j, i: (h, i, 0)),
                  pl.BlockSpec((1, tk, DQ), lambda h, j, i: (h, j, 0)),
                  pl.BlockSpec((1, tk, DV), lambda h, j, i: (h, j, 0)),
                  pl.BlockSpec((1, tq, DV), lambda h, j, i: (h, i, 0)),
                  pl.BlockSpec((1, tq, 1), lambda h, j, i: (h, i, 0)),
                  pl.BlockSpec((1, tq, 1), lambda h, j, i: (h, i, 0))],
        out_specs=(pl.BlockSpec((1, L, DQ), lambda h, j, i: (h, 0, 0)),
                   pl.BlockSpec((1, tk, DQ), lambda h, j, i: (h, j, 0)),
                   pl.BlockSpec((1, tk, DV), lambda h, j, i: (h, j, 0))),
        scratch_shapes=[pltpu.VMEM((tk, DQ), F32), pltpu.VMEM((tk, DV), F32)],
        compiler_params=_params(("arbitrary", "arbitrary", "arbitrary")),
        name="mla_attn_bwd",
    )(q, k, v, do, lse, delta)


@functools.partial(jax.custom_vjp, nondiff_argnums=(3,))
def full_attention(q, k, v, scale):
    return _attn_fwd_call(q, k, v, scale)[0]


def _full_attention_fwd(q, k, v, scale):
    o, lse = _attn_fwd_call(q, k, v, scale)
    return o, (q, k, v, o, lse)


def _full_attention_bwd(scale, res, do):
    q, k, v, o, lse = res
    delta = jnp.sum(do * o, axis=-1, keepdims=True)
    return _attn_bwd_call(q, k, v, do, lse, delta, scale)


full_attention.defvjp(_full_attention_fwd, _full_attention_bwd)


_NN = ((1,), (0,))
_NT = ((1,), (1,))
_TN = ((0,), (0,))
_DOT_VJP = {_NN: (("g", "b", _NT), ("a", "g", _TN)),
            _NT: (("g", "b", _NN), ("g", "a", _TN)),
            _TN: (("b", "g", _NT), ("a", "g", _NN))}


def _dot1(a, b, dims):
    return lax.dot_general(a.astype(BF16), b.astype(BF16), (dims, ((), ())), preferred_element_type=F32)


def _split_bf16(t):
    hi = t.astype(BF16)
    return hi, (t - hi.astype(F32)).astype(BF16)


def _dot3_raw(a, b, dims):
    ah, al = _split_bf16(a)
    bh, bl = _split_bf16(b)
    d = lambda p, q: lax.dot_general(p, q, (dims, ((), ())), preferred_element_type=F32)
    return d(ah, bh) + (d(ah, bl) + d(al, bh))


def _with_f32_cotangents(raw):
    @functools.partial(jax.custom_vjp, nondiff_argnums=(2,))
    def dot(a, b, dims):
        return raw(a, b, dims)

    def fwd(a, b, dims):
        return raw(a, b, dims), (a, b)

    def bwd(dims, res, g):
        a, b = res
        ops = {"a": a, "b": b, "g": g}
        (p, q, dp), (r, s, dr) = _DOT_VJP[dims]
        return raw(ops[p], ops[q], dp), raw(ops[r], ops[s], dr)

    dot.defvjp(fwd, bwd)
    return dot


def _halves(t, axis):
    n = t.shape[axis] // 2
    return lax.slice_in_dim(t, 0, n, axis=axis), lax.slice_in_dim(t, n, 2 * n, axis=axis)


@functools.partial(jax.custom_vjp, nondiff_argnums=(2,))
def _stack(a, b, axis):
    return jnp.concatenate([a, b], axis=axis)


_stack.defvjp(lambda a, b, axis: (jnp.concatenate([a, b], axis=axis), None),
              lambda axis, _, g: _halves(g, axis))


@functools.partial(jax.custom_vjp, nondiff_argnums=(1,))
def _unstack(t, axis):
    return _halves(t, axis)


_unstack.defvjp(lambda t, axis: (_halves(t, axis), None),
                lambda axis, _, g: (jnp.concatenate(list(g), axis=axis),))


_bdot = _with_f32_cotangents(_dot1)
_dot3 = _with_f32_cotangents(_dot3_raw)


def _swa_heads(q, k, v, bias, sink, valid):
    R = range(len(q))
    kv = [h // SWA_KV_GROUP for h in R]
    s = [_bdot(q[h], k[kv[h]], _NT) * (SWA_HEAD_DIM ** -0.5) + bias[h] for h in R]
    s = [jnp.where(valid, x, -1e30) for x in s]
    m = [lax.stop_gradient(jnp.maximum(jnp.max(s[h], axis=1, keepdims=True), sink[h])) for h in R]
    p = [jnp.exp(s[h] - m[h]) for h in R]
    den = [jnp.sum(p[h], axis=1, keepdims=True) + jnp.exp(sink[h] - m[h]) for h in R]
    return [_bdot(p[h] / den[h], v[kv[h]], _NN) for h in R]


def _swa_valid(n, L):
    qi = lax.broadcasted_iota(jnp.int32, (SWA_BLOCK, 3 * SWA_BLOCK), 0)
    sj = lax.broadcasted_iota(jnp.int32, (SWA_BLOCK, 3 * SWA_BLOCK), 1)
    rel = sj - SWA_BLOCK - qi
    kpos = n * SWA_BLOCK + sj - SWA_BLOCK
    return (jnp.abs(rel) <= WINDOW) & (kpos >= 0) & (kpos < L)


def _swa_specs(HQ, HKV, L):
    B = SWA_BLOCK
    q_spec = pl.BlockSpec((HQ, B, SWA_HEAD_DIM), lambda n: (0, n, 0))
    kv_spec = pl.BlockSpec((HKV, L + 2 * B, SWA_HEAD_DIM), lambda n: (0, 0, 0))
    bias_spec = pl.BlockSpec((HQ, B, 3 * B), lambda n: (0, 0, 0))
    sink_spec = pl.BlockSpec((HQ, B, 1), lambda n: (0, 0, 0))
    return q_spec, kv_spec, bias_spec, sink_spec


def _swa_fwd_call(q, kpad, vpad, bias, sink):
    HQ, L, _ = q.shape
    HKV = kpad.shape[0]
    B = SWA_BLOCK

    def body(q_ref, k_ref, v_ref, b_ref, s_ref, o_ref):
        n = pl.program_id(0)
        rows = pl.ds(pl.multiple_of(n * B, B), 3 * B)
        out = _swa_heads([q_ref[h] for h in range(HQ)], [k_ref[g, rows, :] for g in range(HKV)],
                         [v_ref[g, rows, :] for g in range(HKV)], [b_ref[h] for h in range(HQ)],
                         [s_ref[h] for h in range(HQ)], _swa_valid(n, L))
        for h in range(HQ):
            o_ref[h] = out[h]

    q_spec, kv_spec, bias_spec, sink_spec = _swa_specs(HQ, HKV, L)
    return pl.pallas_call(
        body,
        out_shape=jax.ShapeDtypeStruct(q.shape, F32),
        grid=(L // B,),
        in_specs=[q_spec, kv_spec, kv_spec, bias_spec, sink_spec],
        out_specs=q_spec,
        compiler_params=_params(("parallel",)),
        name="swa_fwd",
    )(q, kpad, vpad, bias, sink)


def _swa_bwd_call(q, kpad, vpad, bias, sink, do):
    HQ, L, _ = q.shape
    HKV = kpad.shape[0]
    B = SWA_BLOCK

    def body(q_ref, k_ref, v_ref, b_ref, s_ref, do_ref, dq_ref, dk_ref, dv_ref, db_ref, ds_ref):
        n = pl.program_id(0)

        @pl.when(n == 0)
        def _():
            dk_ref[...] = jnp.zeros_like(dk_ref)
            dv_ref[...] = jnp.zeros_like(dv_ref)
            db_ref[...] = jnp.zeros_like(db_ref)
            ds_ref[...] = jnp.zeros_like(ds_ref)

        rows = pl.ds(pl.multiple_of(n * B, B), 3 * B)
        _, vjp = jax.vjp(functools.partial(_swa_heads, valid=_swa_valid(n, L)),
                         [q_ref[h] for h in range(HQ)], [k_ref[g, rows, :] for g in range(HKV)],
                         [v_ref[g, rows, :] for g in range(HKV)], [b_ref[h] for h in range(HQ)],
                         [s_ref[h] for h in range(HQ)])
        dq, dk, dv, db, dsk = vjp([do_ref[h] for h in range(HQ)])
        for h in range(HQ):
            dq_ref[h] = dq[h]
            db_ref[h] += db[h]
            ds_ref[h] += dsk[h]
        for g in range(HKV):
            dk_ref[g, rows, :] += dk[g]
            dv_ref[g, rows, :] += dv[g]

    q_spec, kv_spec, bias_spec, sink_spec = _swa_specs(HQ, HKV, L)
    return pl.pallas_call(
        body,
        out_shape=(jax.ShapeDtypeStruct(q.shape, F32), jax.ShapeDtypeStruct(kpad.shape, F32),
                   jax.ShapeDtypeStruct(vpad.shape, F32), jax.ShapeDtypeStruct(bias.shape, F32),
                   jax.ShapeDtypeStruct(sink.shape, F32)),
        grid=(L // B,),
        in_specs=[q_spec, kv_spec, kv_spec, bias_spec, sink_spec, q_spec],
        out_specs=(q_spec, kv_spec, kv_spec, bias_spec, sink_spec),
        compiler_params=_params(("arbitrary",)),
        name="swa_bwd",
    )(q, kpad, vpad, bias, sink, do)


@jax.custom_vjp
def window_attention(q, kpad, vpad, bias, sink):
    return _swa_fwd_call(q, kpad, vpad, bias, sink)


def _window_attention_fwd(q, kpad, vpad, bias, sink):
    return _swa_fwd_call(q, kpad, vpad, bias, sink), (q, kpad, vpad, bias, sink)


def _window_attention_bwd(res, do):
    return _swa_bwd_call(*res, do)


window_attention.defvjp(_window_attention_fwd, _window_attention_bwd)


def _unit_tri_inverses(a):
    C = a[0].shape[0]
    ii = lax.broadcasted_iota(jnp.int32, (C, C), 0)
    jj = lax.broadcasted_iota(jnp.int32, (C, C), 1)
    pw = [-x for x in a]
    t = [jnp.where(ii == jj, 1.0, 0.0)] * len(a)
    for _ in range(int(math.log2(C))):
        both = [_dot3_raw(jnp.concatenate([ts, ps], axis=0), ps, _NN) for ts, ps in zip(t, pw)]
        t = [ts + b[:C] for ts, b in zip(t, both)]
        pw = [b[C:] for b in both]
    return t


@jax.custom_vjp
def _known_inverse(a, t):
    return t


def _known_inverse_fwd(a, t):
    return t, t


def _known_inverse_bwd(t, dt):
    da = -_dot3_raw(_dot3_raw(t, dt, _TN), t, _NT)
    return da, jnp.zeros_like(t)


_known_inverse.defvjp(_known_inverse_fwd, _known_inverse_bwd)


def _gdn_chunks(S, q, k, v, gc, gr, gl, beta, t_known, backwards):
    R = range(len(q))
    C = q[0].shape[0]
    ii = lax.broadcasted_iota(jnp.int32, (C, C), 0)
    jj = lax.broadcasted_iota(jnp.int32, (C, C), 1)
    causal = [(ii <= jj) if backwards[s] else (ii >= jj) for s in R]
    strict = [(ii < jj) if backwards[s] else (ii > jj) for s in R]
    decay = [jnp.where(causal[s], jnp.exp(jnp.where(causal[s], gc[s] - gr[s], 0.0)), 0.0) for s in R]
    kb = [k[s] * beta[s] for s in R]
    kk_qk = [_unstack(_bdot(_stack(kb[s], q[s], 0), k[s], _NT), 0) for s in R]
    a = [jnp.where(strict[s], kk_qk[s][0] * decay[s], 0.0) for s in R]
    t = _unit_tri_inverses(a) if t_known is None else [_known_inverse(a[s], t_known[s]) for s in R]
    u_w = [_unstack(_dot3(t[s], _stack(v[s] * beta[s], kb[s] * jnp.exp(gc[s]), 1), _NN), 1) for s in R]
    ws_qs = [_unstack(_bdot(_stack(u_w[s][1], q[s] * jnp.exp(gc[s]), 0), S[s], _NN), 0) for s in R]
    v_new = [u_w[s][0] - ws_qs[s][0] for s in R]
    o = [ws_qs[s][1] + _bdot(kk_qk[s][1] * decay[s], v_new[s], _NN) for s in R]
    s_new = [S[s] * jnp.exp(gl[s]) + _bdot(k[s] * jnp.exp(gl[s] - gc[s]), v_new[s], _TN) for s in R]
    return s_new, o, t


def _gdn_specs(H, L, C, Dh, flip):
    NC = L // C
    idx = (lambda c: NC - 1 - c) if flip else (lambda c: c)
    seq = pl.BlockSpec((C, H * Dh), lambda c: (idx(c), 0))

    def scalar(rows, cols, group):
        return pl.BlockSpec((H, 1, rows, cols), lambda c: (group, idx(c), 0, 0))

    return seq, scalar


def _gdn_fwd_call(q, k, v, gc, gr, gl, beta):
    L, HD = q.shape
    C = gc.shape[2]
    NC = L // C
    H = gc.shape[0] // 2
    Dh = HD // H

    def body(qf, kf, vf, qb, kb, vb, gcf, grf, glf, bf, gcb, grb, glb, bb,
             of_ref, ob_ref, stf_ref, stb_ref, tf_ref, tb_ref, s_scr):
        @pl.when(pl.program_id(0) == 0)
        def _():
            s_scr[...] = jnp.zeros_like(s_scr)

        groups = ((qf, kf, vf, gcf, grf, glf, bf), (qb, kb, vb, gcb, grb, glb, bb))
        outs = ((of_ref, stf_ref, tf_ref), (ob_ref, stb_ref, tb_ref))
        seqs = [(d, h) for d in range(2) for h in range(H)]
        cols = [slice(h * Dh, (h + 1) * Dh) for h in range(H)]
        s0 = [s_scr[d * H + h] for d, h in seqs]
        for (d, h), s in zip(seqs, s0):
            outs[d][1][h, 0] = s
        seq_in = lambda j: [groups[d][j][:, cols[h]] for d, h in seqs]
        scal_in = lambda j: [groups[d][j][h, 0] for d, h in seqs]
        s1, o, t = _gdn_chunks(s0, seq_in(0), seq_in(1), seq_in(2), scal_in(3), scal_in(4), scal_in(5), scal_in(6),
                               None, [d == 1 for d, _ in seqs])
        for i, (d, h) in enumerate(seqs):
            s_scr[d * H + h] = s1[i]
            outs[d][0][:, cols[h]] = o[i]
            outs[d][2][h, 0] = t[i]

    seq_f, sc_f = _gdn_specs(H, L, C, Dh, False)
    seq_b, sc_b = _gdn_specs(H, L, C, Dh, True)
    sds = jax.ShapeDtypeStruct
    return pl.pallas_call(
        body,
        out_shape=(sds((L, HD), F32), sds((L, HD), F32), sds((H, NC, Dh, Dh), F32), sds((H, NC, Dh, Dh), F32),
                   sds((H, NC, C, C), F32), sds((H, NC, C, C), F32)),
        grid=(NC,),
        in_specs=[seq_f, seq_f, seq_f, seq_b, seq_b, seq_b,
                  sc_f(C, 1, 0), sc_f(1, C, 0), sc_f(1, 1, 0), sc_f(C, 1, 0),
                  sc_b(C, 1, 1), sc_b(1, C, 1), sc_b(1, 1, 1), sc_b(C, 1, 1)],
        out_specs=(seq_f, seq_b, sc_f(Dh, Dh, 0), sc_b(Dh, Dh, 0), sc_f(C, C, 0), sc_b(C, C, 0)),
        scratch_shapes=[pltpu.VMEM((2 * H, Dh, Dh), F32)],
        compiler_params=_params(("arbitrary",)),
        name="gdn_fwd",
    )(q, k, v, q, k, v, gc, gr, gl, beta, gc, gr, gl, beta)


def _gdn_bwd_call(q, k, v, gc, gr, gl, beta, st_f, st_b, t_f, t_b, do_f, do_b):
    L, HD = q.shape
    C = gc.shape[2]
    NC = L // C
    H = gc.shape[0] // 2
    Dh = HD // H

    def body(qf, kf, vf, qb, kb, vb, gcf, grf, glf, bf, gcb, grb, glb, bb, stf, stb, tf, tb, dof, dob,
             dqf, dkf, dvf, dqb, dkb, dvb, dgcf, dgrf, dglf, dbf, dgcb, dgrb, dglb, dbb, ds_scr):
        @pl.when(pl.program_id(0) == 0)
        def _():
            ds_scr[...] = jnp.zeros_like(ds_scr)

        groups = ((qf, kf, vf, gcf, grf, glf, bf, stf, tf, dof), (qb, kb, vb, gcb, grb, glb, bb, stb, tb, dob))
        outs = ((dqf, dkf, dvf, dgcf, dgrf, dglf, dbf), (dqb, dkb, dvb, dgcb, dgrb, dglb, dbb))
        seqs = [(d, h) for d in range(2) for h in range(H)]
        cols = [slice(h * Dh, (h + 1) * Dh) for h in range(H)]
        seq_in = lambda j: [groups[d][j][:, cols[h]] for d, h in seqs]
        scal_in = lambda j: [groups[d][j][h, 0] for d, h in seqs]
        t_known = scal_in(8)
        backwards = [d == 1 for d, _ in seqs]

        def chunks(*args):
            return _gdn_chunks(*args, t_known, backwards)[:2]

        _, vjp = jax.vjp(chunks, scal_in(7), seq_in(0), seq_in(1), seq_in(2), scal_in(3), scal_in(4), scal_in(5),
                         scal_in(6))
        grads = vjp(([ds_scr[d * H + h] for d, h in seqs], seq_in(9)))
        for i, (d, h) in enumerate(seqs):
            ds_scr[d * H + h] = grads[0][i]
            for j in range(3):
                outs[d][j][:, cols[h]] = grads[1 + j][i]
            for j in range(3, 7):
                outs[d][j][h, 0] = grads[1 + j][i]

    seq_f, sc_f = _gdn_specs(H, L, C, Dh, True)
    seq_b, sc_b = _gdn_specs(H, L, C, Dh, False)
    sds = jax.ShapeDtypeStruct
    seq_out = sds((L, HD), F32)
    half = lambda t: sds((H,) + t.shape[1:], F32)
    scal_f = [sc_f(C, 1, 0), sc_f(1, C, 0), sc_f(1, 1, 0), sc_f(C, 1, 0)]
    scal_b = [sc_b(C, 1, 1), sc_b(1, C, 1), sc_b(1, 1, 1), sc_b(C, 1, 1)]
    scal_b_out = [sc_b(C, 1, 0), sc_b(1, C, 0), sc_b(1, 1, 0), sc_b(C, 1, 0)]
    return pl.pallas_call(
        body,
        out_shape=(seq_out,) * 6 + (half(gc), half(gr), half(gl), half(beta)) * 2,
        grid=(NC,),
        in_specs=[seq_f, seq_f, seq_f, seq_b, seq_b, seq_b] + scal_f + scal_b
                 + [sc_f(Dh, Dh, 0), sc_b(Dh, Dh, 0), sc_f(C, C, 0), sc_b(C, C, 0), seq_f, seq_b],
        out_specs=[seq_f, seq_f, seq_f, seq_b, seq_b, seq_b] + scal_f + scal_b_out,
        scratch_shapes=[pltpu.VMEM((2 * H, Dh, Dh), F32)],
        compiler_params=_params(("arbitrary",)),
        name="gdn_bwd",
    )(q, k, v, q, k, v, gc, gr, gl, beta, gc, gr, gl, beta, st_f, st_b, t_f, t_b, do_f, do_b)


@jax.custom_vjp
def gated_delta_rule(q, k, v, gc, gr, gl, beta):
    return _gdn_fwd_call(q, k, v, gc, gr, gl, beta)[:2]


def _gated_delta_rule_fwd(q, k, v, gc, gr, gl, beta):
    o_f, o_b, st_f, st_b, t_f, t_b = _gdn_fwd_call(q, k, v, gc, gr, gl, beta)
    return (o_f, o_b), (q, k, v, gc, gr, gl, beta, st_f, st_b, t_f, t_b)


def _gated_delta_rule_bwd(res, do):
    (dqf, dkf, dvf, dqb, dkb, dvb, dgcf, dgrf, dglf, dbf, dgcb, dgrb, dglb, dbb) = _gdn_bwd_call(*res, *do)
    cat = lambda a, b: jnp.concatenate([a, b], axis=0)
    return dqf + dqb, dkf + dkb, dvf + dvb, cat(dgcf, dgcb), cat(dgrf, dgrb), cat(dglf, dglb), cat(dbf, dbb)


gated_delta_rule.defvjp(_gated_delta_rule_fwd, _gated_delta_rule_bwd)


GATE_ROWS = 128


def _gate_specs(L, D, nb):
    tm = _tile(L, GATE_ROWS, SUBLANES)
    logit_specs = [pl.BlockSpec((tm, D), functools.partial(lambda i, b: (i, b), b=b)) for b in range(nb)]
    row_spec = pl.BlockSpec((tm, D), lambda i: (i, 0))
    return tm, logit_specs, row_spec


def _gate_fwd_call(logits, branches):
    nb = len(branches)
    L, D = branches[0].shape
    tm, logit_specs, row_spec = _gate_specs(L, D, nb)

    def body(*refs):
        o_ref = refs[2 * nb]
        acc = jax.nn.sigmoid(refs[0][...]) * refs[nb][...]
        for b in range(1, nb):
            acc = acc + jax.nn.sigmoid(refs[b][...]) * refs[nb + b][...]
        o_ref[...] = acc

    return pl.pallas_call(
        body, out_shape=jax.ShapeDtypeStruct((L, D), F32), grid=(L // tm,),
        in_specs=logit_specs + [row_spec] * nb, out_specs=row_spec,
        compiler_params=_params(("parallel",)), name="gate_merge_fwd",
    )(*([logits] * nb), *branches)


def _gate_bwd_call(logits, branches, dm):
    nb = len(branches)
    L, D = branches[0].shape
    tm, logit_specs, row_spec = _gate_specs(L, D, nb)

    def body(*refs):
        dm_v = refs[2 * nb][...]
        dl_ref = refs[2 * nb + 1]
        for b in range(nb):
            sig = jax.nn.sigmoid(refs[b][...])
            refs[2 * nb + 2 + b][...] = dm_v * sig
            dl_ref[:, b * D:(b + 1) * D] = dm_v * refs[nb + b][...] * (sig * (1.0 - sig))

    return pl.pallas_call(
        body,
        out_shape=[jax.ShapeDtypeStruct(logits.shape, F32)] + [jax.ShapeDtypeStruct((L, D), F32)] * nb,
        grid=(L // tm,),
        in_specs=logit_specs + [row_spec] * (nb + 1),
        out_specs=[pl.BlockSpec((tm, nb * D), lambda i: (i, 0))] + [row_spec] * nb,
        compiler_params=_params(("parallel",)), name="gate_merge_bwd",
    )(*([logits] * nb), *branches, dm)


@jax.custom_vjp
def gate_merge(logits, branches):
    return _gate_fwd_call(logits, branches)


def _gate_merge_fwd(logits, branches):
    return _gate_fwd_call(logits, branches), (logits, branches)


def _gate_merge_bwd(res, dm):
    logits, branches = res
    out = _gate_bwd_call(logits, branches, dm)
    return out[0], tuple(out[1:])


gate_merge.defvjp(_gate_merge_fwd, _gate_merge_bwd)


def _loss_fwd_call(y, t):
    L, D = y.shape
    tm = _rms_rows(L, D)

    def body(y_ref, t_ref, o_ref):
        @pl.when(pl.program_id(0) == 0)
        def _():
            o_ref[...] = jnp.zeros_like(o_ref)

        e = y_ref[...] - t_ref[...]
        part = jnp.sum(jnp.sum(e * e, axis=1, keepdims=True), axis=0, keepdims=True)
        o_ref[...] += part * (0.5 / D)

    out = pl.pallas_call(
        body,
        out_shape=jax.ShapeDtypeStruct((SUBLANES, LANES), F32),
        grid=(L // tm,),
        in_specs=[pl.BlockSpec((tm, D), lambda i: (i, 0)), pl.BlockSpec((tm, D), lambda i: (i, 0))],
        out_specs=pl.BlockSpec((SUBLANES, LANES), lambda i: (0, 0)),
        compiler_params=_params(("arbitrary",)),
        name="loss_fwd",
    )(y, t)
    return out[0, 0]


def _loss_bwd_call(y, t, g):
    L, D = y.shape
    tm = _rms_rows(L, D)

    def body(y_ref, t_ref, g_ref, o_ref):
        o_ref[...] = (y_ref[...] - t_ref[...]) * (g_ref[...] * (1.0 / D))

    return pl.pallas_call(
        body,
        out_shape=jax.ShapeDtypeStruct((L, D), F32),
        grid=(L // tm,),
        in_specs=[pl.BlockSpec((tm, D), lambda i: (i, 0)), pl.BlockSpec((tm, D), lambda i: (i, 0)),
                  pl.BlockSpec((1, 1), lambda i: (0, 0))],
        out_specs=pl.BlockSpec((tm, D), lambda i: (i, 0)),
        compiler_params=_params(("parallel",)),
        name="loss_bwd",
    )(y, t, g.reshape(1, 1))


@jax.custom_vjp
def loss_head(y, t):
    return _loss_fwd_call(y, t)


def _loss_head_fwd(y, t):
    return _loss_fwd_call(y, t), (y, t)


def _loss_head_bwd(res, g):
    y, t = res
    return _loss_bwd_call(y, t, g), jnp.zeros_like(t)


loss_head.defvjp(_loss_head_fwd, _loss_head_bwd)


def _rows_for(C, nbuf):
    return max(16, ((24 << 20) // (nbuf * 4 * C)) // 16 * 16)


def _pair_add_call(a, b, name):
    R, C = a.shape
    tm = _tile(R, _rows_for(C, 6), 16)

    def body(a_ref, b_ref, o_ref):
        o_ref[...] = (a_ref[...].astype(F32) + b_ref[...].astype(F32)).astype(o_ref.dtype)

    spec = pl.BlockSpec((tm, C), lambda i: (i, 0))
    return pl.pallas_call(
        body, out_shape=jax.ShapeDtypeStruct(a.shape, a.dtype), grid=(R // tm,),
        in_specs=[spec, spec], out_specs=spec, compiler_params=_params(("parallel",)), name=name,
    )(a, b)


def _adamw_call(parts, w, m, v, name):
    P, R, C = parts.shape
    tm = _tile(R, _rows_for(C, 2 * (P + 7)), 16)

    def body(p_ref, w_ref, m_ref, v_ref, g_ref, d_ref, mo_ref, vo_ref):
        g = p_ref[0].astype(F32)
        for s in range(1, P):
            g = g + p_ref[s].astype(F32)
        m2 = ADAM_B1 * m_ref[...] + (1.0 - ADAM_B1) * g
        v2 = ADAM_B2 * v_ref[...] + (1.0 - ADAM_B2) * jnp.square(g)
        m_hat = m2 / (1.0 - ADAM_B1 ** ADAM_STEP)
        v_hat = v2 / (1.0 - ADAM_B2 ** ADAM_STEP)
        g_ref[...] = g
        d_ref[...] = -ADAM_LR * (m_hat / (jnp.sqrt(v_hat) + ADAM_EPS) + ADAM_WD * w_ref[...])
        mo_ref[...] = m2
        vo_ref[...] = v2

    spec = pl.BlockSpec((tm, C), lambda i: (i, 0))
    out = jax.ShapeDtypeStruct((R, C), F32)
    return pl.pallas_call(
        body, out_shape=(out, out, out, out), grid=(R // tm,),
        in_specs=[pl.BlockSpec((P, tm, C), lambda i: (0, i, 0)), spec, spec, spec],
        out_specs=(spec, spec, spec, spec), compiler_params=_params(("parallel",)), name=name,
    )(parts, w, m, v)


_MESH = pl.DeviceIdType.MESH
FLIPS_CHIPS = ((1, 0, 0), (0, 1, 0), (1, 1, 0))
FLIPS_ALL = ((0, 0, 1), (1, 0, 0), (0, 1, 0), (1, 1, 0), (1, 0, 1), (0, 1, 1), (1, 1, 1))
FLIPS_SIBLING = ((0, 0, 1),)


def _exchange(arrays, flips, mode, name):
    n = len(arrays)
    nf = len(flips)
    if flips == FLIPS_SIBLING:
        n_slots, slot = 2, (lambda x, y, c: c)
    elif any(f[2] for f in flips):
        n_slots, slot = 8, (lambda x, y, c: 4 * x + 2 * y + c)
    else:
        n_slots, slot = 4, (lambda x, y, c: 2 * x + y)

    def body(*refs):
        ins, outs = refs[:n], refs[n:2 * n]
        send_sems, recv_sems = refs[2 * n:]
        x, y, c = lax.axis_index("x"), lax.axis_index("y"), lax.axis_index("c")
        me = slot(x, y, c)
        peers = [(x + f[0] - 2 * x * f[0], y + f[1] - 2 * y * f[1], c + f[2] - 2 * c * f[2]) for f in flips]

        def copy(i, k, sending):
            px, py, pc = peers[k]
            there = slot(px, py, pc)
            if mode == "swap":
                src, dst = ins[i], outs[i]
            elif mode == "swap_half":
                h = arrays[i].shape[1] // 2
                src, dst = ins[i].at[:, pl.ds(pc * h, h)], outs[i]
            elif mode == "gather":
                src, dst = ins[i], outs[i].at[me if sending else there]
            else:
                src, dst = ins[i].at[there if sending else me], outs[i].at[me if sending else there]
            return pltpu.make_async_remote_copy(src_ref=src, dst_ref=dst, send_sem=send_sems.at[i, k],
                                                recv_sem=recv_sems.at[i, k], device_id=(px, py, pc),
                                                device_id_type=_MESH)

        sends = [copy(i, k, True) for i in range(n) for k in range(nf)]
        for cp in sends:
            cp.start()
        for i in range(n):
            for k in range(nf):
                copy(i, k, False).wait_recv()
        for cp in sends:
            cp.wait_send()

    def with_own_slot(outs):
        if mode not in ("gather", "scatter"):
            return outs
        me = slot(lax.axis_index("x"), lax.axis_index("y"), lax.axis_index("c"))
        own = [a if mode == "gather" else lax.dynamic_index_in_dim(a, me, 0, keepdims=False) for a in arrays]
        return [lax.dynamic_update_index_in_dim(o, a, me, 0) for o, a in zip(outs, own)]

    if mode == "gather":
        out_shape = [jax.ShapeDtypeStruct((n_slots,) + a.shape, a.dtype) for a in arrays]
    elif mode == "swap_half":
        out_shape = [jax.ShapeDtypeStruct((a.shape[0], a.shape[1] // 2) + a.shape[2:], a.dtype) for a in arrays]
    else:
        out_shape = [jax.ShapeDtypeStruct(a.shape, a.dtype) for a in arrays]
    any_spec = pl.BlockSpec(memory_space=pl.ANY)
    return with_own_slot(pl.pallas_call(
        body,
        out_shape=out_shape,
        in_specs=[any_spec] * n,
        out_specs=[any_spec] * n,
        scratch_shapes=[pltpu.SemaphoreType.DMA((n, nf)), pltpu.SemaphoreType.DMA((n, nf))],
        compiler_params=pltpu.CompilerParams(has_side_effects=True),
        name=name,
    )(*arrays))


def _two_level_all_gather(arrays, name):
    n = len(arrays)
    nf = len(FLIPS_CHIPS)

    def body(*refs):
        ins, outs = refs[:n], refs[n:2 * n]
        send_sems, recv_sems, pass_send_sems, pass_recv_sems = refs[2 * n:]
        x, y, c = lax.axis_index("x"), lax.axis_index("y"), lax.axis_index("c")
        me = 2 * x + y
        chips = [(x + f[0] - 2 * x * f[0], y + f[1] - 2 * y * f[1]) for f in FLIPS_CHIPS]

        def rows(i, core):
            h = arrays[i].shape[0] // 2
            return pl.ds(core * h, h)

        def fetch(i, k, sending):
            px, py = chips[k]
            dst = outs[i].at[me if sending else 2 * px + py, rows(i, c)]
            return pltpu.make_async_remote_copy(src_ref=ins[i].at[rows(i, c)], dst_ref=dst,
                                                send_sem=send_sems.at[i, k], recv_sem=recv_sems.at[i, k],
                                                device_id=(px, py, c), device_id_type=_MESH)

        def hand_on(i, k, sending):
            px, py = chips[k]
            part = outs[i].at[2 * px + py, rows(i, c if sending else 1 - c)]
            return pltpu.make_async_remote_copy(src_ref=part, dst_ref=part, send_sem=pass_send_sems.at[i, k],
                                                recv_sem=pass_recv_sems.at[i, k], device_id=(x, y, 1 - c),
                                                device_id_type=_MESH)

        pairs = [(i, k) for i in range(n) for k in range(nf)]
        for i, k in pairs:
            fetch(i, k, True).start()
        for i, k in pairs:
            fetch(i, k, False).wait_recv()
            hand_on(i, k, True).start()
        for i, k in pairs:
            hand_on(i, k, False).wait_recv()
        for i, k in pairs:
            fetch(i, k, True).wait_send()
            hand_on(i, k, True).wait_send()

    any_spec = pl.BlockSpec(memory_space=pl.ANY)
    sems = pltpu.SemaphoreType.DMA((n, nf))
    gathered = pl.pallas_call(
        body,
        out_shape=[jax.ShapeDtypeStruct((4,) + a.shape, a.dtype) for a in arrays],
        in_specs=[any_spec] * n,
        out_specs=[any_spec] * n,
        scratch_shapes=[sems, sems, sems, sems],
        compiler_params=pltpu.CompilerParams(has_side_effects=True),
        name=name,
    )(*arrays)
    me = 2 * lax.axis_index("x") + lax.axis_index("y")
    return [lax.dynamic_update_index_in_dim(g, a, me, 0) for g, a in zip(gathered, arrays)]


def _sum_parts_call(parts, name):
    P, R, C = parts.shape
    tm = _tile(R, _rows_for(C, 2 * (P + 2)), 16)

    def body(p_ref, o_ref):
        g = p_ref[0].astype(F32)
        for s in range(1, P):
            g = g + p_ref[s].astype(F32)
        o_ref[...] = g

    return pl.pallas_call(
        body, out_shape=jax.ShapeDtypeStruct((R, C), F32), grid=(R // tm,),
        in_specs=[pl.BlockSpec((P, tm, C), lambda i: (0, i, 0))], out_specs=pl.BlockSpec((tm, C), lambda i: (i, 0)),
        compiler_params=_params(("parallel",)), name=name,
    )(parts)


WEIGHT_NAMES = ('w_in', 's5_lam_re', 's5_lam_im', 's5_log_step', 's5_b_re', 's5_b_im', 's5_c_re', 's5_c_im', 's5_d',
                's5_w_glu', 's5_b_glu', 'gdn_conv', 'gdn_a_log', 'gdn_dt_bias', 'gdn_o_gain', 'swa_sink', 't5_bias',
                'mla_q_gain', 'mla_kv_gain', 'mla_w_uq', 'mla_w_ukv', 'w_branch', 'w_out', 'mix_pre_gain',
                'mix_post_gain', 'mlp_pre_gain', 'mlp_post_gain', 'w_mlp_in', 'w_mlp_out')
SHARDED = {'w_in': (2, BF16), 's5_w_glu': (1, BF16), 'gdn_conv': (2, F32), 'mla_w_uq': (2, BF16),
           'mla_w_ukv': (2, BF16), 'w_branch': (3, BF16), 'w_out': (1, BF16), 'w_mlp_in': (2, BF16),
           'w_mlp_out': (1, BF16)}
REPLICATED = tuple(n for n in WEIGHT_NAMES if n not in SHARDED)
PACK_COLS = 1024

_IN_A = (('s5_u', 512), ('gdn_qkv', 1536), ('gdn_z', 512), ('gdn_beta', 8), ('gdn_decay', 8))
_IN_B = (('swa_q', 512), ('swa_kv', 256), ('mla_cq', 384), ('mla_ckv', 512), ('mla_kr', 64))
_IN_A_W = sum(w for _, w in _IN_A)
_IN_B_W = sum(w for _, w in _IN_B)
_IN_A_PAD = -_IN_A_W % LANES
_IN_B_PAD = -(_IN_A_W + _IN_A_PAD + _IN_B_W) % 512


def _assemble(g, axis):
    t = jnp.moveaxis(g, 0, axis)
    shape = t.shape[:axis] + (t.shape[axis] * t.shape[axis + 1],) + t.shape[axis + 2:]
    return t.reshape(shape)


def _s5_mixer(u, lam_re, lam_im, log_step, b_re, b_im, c_re, c_im, d_skip, w_glu, b_glu):
    nd, G, P = lam_re.shape
    Hg = b_re.shape[-1]
    lam_re = jnp.minimum(lam_re, -1e-4)
    dt = jnp.exp(log_step)[..., None]
    mag = jnp.exp(lam_re * dt)
    abar_r = mag * jnp.cos(lam_im * dt)
    abar_i = mag * jnp.sin(lam_im * dt)
    den = lam_re * lam_re + lam_im * lam_im
    xr = abar_r - 1.0
    xi = abar_i
    coef_r = (xr * lam_re + xi * lam_im) / den
    coef_i = (xi * lam_re - xr * lam_im) / den
    bbar_r = coef_r[..., None] * b_re - coef_i[..., None] * b_im
    bbar_i = coef_r[..., None] * b_im + coef_i[..., None] * b_re
    eye = jnp.eye(G, dtype=F32)

    def dense_b(bb):
        return jnp.einsum('dgph,gk->ghdkp', bb, eye).reshape(G * Hg, nd, G * P)

    def dense_c(cc):
        return jnp.einsum('dghp,gk->khdgp', cc, eye).reshape(G * Hg, nd, G * P)

    b_cat = _to_scan_cols(jnp.stack([dense_b(bbar_r), dense_b(bbar_i)], axis=2), nd)
    c_cat = _to_scan_cols(jnp.stack([dense_c(c_re), -dense_c(c_im)], axis=2), nd).T
    a_row = _to_scan_cols(jnp.stack([abar_r.reshape(nd, G * P), abar_i.reshape(nd, G * P)], axis=1), nd)[None]
    s = s5_scan(linear(u, b_cat, 's5_in'), a_row, nd)
    y = linear(s, c_cat, 's5_out') + d_skip * u
    y = jax.nn.gelu(y)
    return y * jax.nn.sigmoid(linear(y, w_glu, 's5_glu') + b_glu)


def _gdn_mixer(qkv, z, beta_logits, decay_logits, conv_w, a_log, dt_bias, o_gain):
    L = qkv.shape[0]
    Dh = GDN_HEAD_DIM
    H = z.shape[1] // Dh
    C = GDN_CHUNK
    NC = L // C
    xp = jnp.pad(qkv, ((GDN_CONV // 2, GDN_CONV - 1 - GDN_CONV // 2), (0, 0)))
    conv = xp[0:L] * conv_w[0]
    for j in range(1, GDN_CONV):
        conv = conv + xp[j:j + L] * conv_w[j]
    q, k, v = jnp.split(jax.nn.silu(conv), 3, axis=-1)

    def l2n(t):
        return t * lax.rsqrt(jnp.sum(t * t, axis=-1, keepdims=True) + 1e-6)

    q = (l2n(q.reshape(L, H, Dh)) * (Dh ** -0.5)).reshape(L, H * Dh)
    k = l2n(k.reshape(L, H, Dh)).reshape(L, H * Dh)
    beta = jax.nn.sigmoid(beta_logits).reshape(L, 2, H)
    g = -jnp.exp(a_log) * jax.nn.softplus(decay_logits.reshape(L, 2, H) + dt_bias)

    def per_chunk(t, d):
        return t[:, d].T.reshape(H, NC, C)

    g_with = jnp.cumsum(per_chunk(g, 0), axis=-1)
    g_against = jnp.cumsum(per_chunk(g, 1)[..., ::-1], axis=-1)[..., ::-1]
    gs = jnp.concatenate([g_with, g_against], axis=0)
    totals = jnp.concatenate([g_with[..., -1], g_against[..., 0]], axis=0)
    betas = jnp.concatenate([per_chunk(beta, 0), per_chunk(beta, 1)], axis=0)
    o_with, o_against = gated_delta_rule(q, k, v, gs[..., None], gs[:, :, None, :], totals[..., None, None],
                                         betas[..., None])
    o = o_with + o_against
    o = rms_norm(o.reshape(L * H, Dh), o_gain, 'gdn_onorm').reshape(L, H, Dh)
    o = o * jax.nn.silu(z.reshape(L, H, Dh))
    return o.reshape(L, H * Dh)


def _t5_bucket(rel):
    nb = T5_BUCKETS // 2
    max_exact = nb // 2
    ret = jnp.where(rel > 0, nb, 0)
    n = jnp.abs(rel)
    nf = jnp.maximum(n, 1).astype(F32)
    large = max_exact + (jnp.log(nf / max_exact) / math.log(T5_MAX_DISTANCE / max_exact)
                         * (nb - max_exact)).astype(jnp.int32)
    large = jnp.minimum(large, nb - 1)
    return ret + jnp.where(n < max_exact, n, large)


def _swa_mixer(q, kv, sink, t5_bias):
    L = q.shape[0]
    B = SWA_BLOCK
    HQ = q.shape[1] // SWA_HEAD_DIM
    HKV = HQ // SWA_KV_GROUP
    qh = q.reshape(L, HQ, SWA_HEAD_DIM).transpose(1, 0, 2)
    k, v = jnp.split(kv, 2, axis=-1)

    def heads_padded(t):
        return jnp.pad(t.reshape(L, HKV, SWA_HEAD_DIM).transpose(1, 0, 2), ((0, 0), (B, B), (0, 0)))

    qi = jnp.arange(B)[:, None]
    sj = jnp.arange(3 * B)[None, :]
    one_hot = (_t5_bucket(sj - B - qi)[..., None] == jnp.arange(T5_BUCKETS)).astype(F32)
    bias = jnp.einsum('qsb,bh->hqs', one_hot, t5_bias, precision=_HIGHEST)
    sink_rows = jnp.broadcast_to(sink[:, None, None], (HQ, B, 1))
    o = window_attention(qh, heads_padded(k), heads_padded(v), bias, sink_rows)
    return o.transpose(1, 0, 2).reshape(L, HQ * SWA_HEAD_DIM)


def _apply_rope(x, cos, sin):
    x1, x2 = jnp.split(x, 2, axis=-1)
    return jnp.concatenate([x1 * cos - x2 * sin, x2 * cos + x1 * sin], axis=-1)


def _mla_mixer(c_q, c_kv, k_rope, q_gain, kv_gain, w_uq, w_ukv):
    L = c_q.shape[0]
    H = w_uq.shape[1] // (MLA_NOPE + MLA_ROPE)
    q = linear(rms_norm(c_q, q_gain, 'mla_qnorm'), w_uq, 'mla_uq').reshape(L, H, MLA_NOPE + MLA_ROPE)
    kv = linear(rms_norm(c_kv, kv_gain, 'mla_kvnorm'), w_ukv, 'mla_ukv').reshape(L, H, MLA_NOPE + MLA_V)
    q_nope, q_pe = q[..., :MLA_NOPE], q[..., MLA_NOPE:]
    k_nope, v = kv[..., :MLA_NOPE], kv[..., MLA_NOPE:]
    pos = jnp.arange(L, dtype=F32)
    inv_freq = ROPE_THETA ** (-jnp.arange(0, MLA_ROPE, 2, dtype=F32) / MLA_ROPE)
    ang = pos[:, None] * inv_freq[None, :]
    cos, sin = jnp.cos(ang)[:, None, :], jnp.sin(ang)[:, None, :]
    q_pe = _apply_rope(q_pe, cos, sin)
    k_pe = _apply_rope(k_rope[:, None, :], cos, sin)
    qf = jnp.concatenate([q_nope, q_pe], axis=-1).transpose(1, 0, 2)
    kf = jnp.concatenate([k_nope, jnp.broadcast_to(k_pe, (L, H, MLA_ROPE))], axis=-1).transpose(1, 0, 2)
    o = full_attention(qf, kf, v.transpose(1, 0, 2), (MLA_NOPE + MLA_ROPE) ** -0.5)
    return o.transpose(1, 0, 2).reshape(L, H * MLA_V)


_IN_LAYOUT = _IN_A + ((None, _IN_A_PAD),) + _IN_B + ((None, _IN_B_PAD),)


@jax.custom_vjp
def split_proj(proj):
    out, start = [], 0
    for name, width in _IN_LAYOUT:
        if name is not None:
            out.append(proj[:, start:start + width])
        start += width
    return tuple(out)


def _split_proj_bwd(_, cts):
    it = iter(cts)
    rows = cts[0].shape[0]
    parts = [jnp.zeros((rows, width), F32) if name is None else next(it) for name, width in _IN_LAYOUT]
    return (jnp.concatenate(parts, axis=1),)


split_proj.defvjp(lambda proj: (split_proj(proj), None), _split_proj_bwd)


def _local_loss(weights, x, target):
    p = {n: (_assemble(weights[n], SHARDED[n][0]) if n in SHARDED else weights[n]) for n in WEIGHT_NAMES}
    L, D = x.shape
    depth = p['w_in'].shape[0]
    for l in range(depth):
        w_in = p['w_in'][l]
        zeros = lambda n: jnp.zeros((D, n), w_in.dtype)
        w_r = jnp.concatenate([w_in[:, :_IN_A_W], zeros(_IN_A_PAD), w_in[:, _IN_A_W:_IN_A_W + _IN_B_W],
                               zeros(_IN_B_PAD)], axis=1)
        w_g = w_in[:, _IN_A_W + _IN_B_W:]
        h = rms_norm(x, p['mix_pre_gain'][l], 'mix_pre')
        proj = linear(h, w_r, 'in_proj')
        gate_logits = linear(h, w_g, 'in_gate')
        seg = dict(zip([name for name, _ in _IN_A + _IN_B], split_proj(proj)))
        y_a = _s5_mixer(seg['s5_u'], p['s5_lam_re'][l], p['s5_lam_im'][l], p['s5_log_step'][l], p['s5_b_re'][l],
                        p['s5_b_im'][l], p['s5_c_re'][l], p['s5_c_im'][l], p['s5_d'][l], p['s5_w_glu'][l],
                        p['s5_b_glu'][l])
        y_b = _gdn_mixer(seg['gdn_qkv'], seg['gdn_z'], seg['gdn_beta'], seg['gdn_decay'], p['gdn_conv'][l],
                         p['gdn_a_log'][l], p['gdn_dt_bias'][l], p['gdn_o_gain'][l])
        y_c = _swa_mixer(seg['swa_q'], seg['swa_kv'], p['swa_sink'][l], p['t5_bias'])
        y_d = _mla_mixer(seg['mla_cq'], seg['mla_ckv'], seg['mla_kr'], p['mla_q_gain'][l], p['mla_kv_gain'][l],
                         p['mla_w_uq'][l], p['mla_w_ukv'][l])
        branches = tuple(linear(y, p['w_branch'][l, b], 'branch') for b, y in enumerate((y_a, y_b, y_c, y_d)))
        merged = gate_merge(gate_logits, branches)
        x = x + rms_norm(linear(merged, p['w_out'][l], 'mix_out'), p['mix_post_gain'][l], 'mix_post')
        h = rms_norm(x, p['mlp_pre_gain'][l], 'mlp_pre')
        f = relu2_linear(linear(h, p['w_mlp_in'][l], 'mlp_in'), p['w_mlp_out'][l])
        x = x + rms_norm(f, p['mlp_post_gain'][l], 'mlp_post')
    return loss_head(x, target)


def _two_d(t, lead):
    return t.reshape(t.shape[:lead] + (-1, t.shape[-1]))


def _pack(arrays):
    flat = jnp.concatenate([a.reshape(-1) for a in arrays])
    pad = -flat.shape[0] % (16 * PACK_COLS)
    return jnp.pad(flat, (0, pad)).reshape(-1, PACK_COLS)


def _unpack(packed, like):
    flat = packed.reshape(-1)
    out, pos = [], 0
    for a in like:
        out.append(flat[pos:pos + a.size].reshape(a.shape))
        pos += a.size
    return out


def kernel(x, w_in, s5_lam_re, s5_lam_im, s5_log_step, s5_b_re, s5_b_im, s5_c_re, s5_c_im, s5_d, s5_w_glu, s5_b_glu, gdn_conv, gdn_a_log, gdn_dt_bias, gdn_o_gain, swa_sink, t5_bias, mla_q_gain, mla_kv_gain, mla_w_uq, mla_w_ukv, w_branch, w_out, mix_pre_gain, mix_post_gain, mlp_pre_gain, mlp_post_gain, w_mlp_in, w_mlp_out, loss_target, m_w_in, m_s5_lam_re, m_s5_lam_im, m_s5_log_step, m_s5_b_re, m_s5_b_im, m_s5_c_re, m_s5_c_im, m_s5_d, m_s5_w_glu, m_s5_b_glu, m_gdn_conv, m_gdn_a_log, m_gdn_dt_bias, m_gdn_o_gain, m_swa_sink, m_t5_bias, m_mla_q_gain, m_mla_kv_gain, m_mla_w_uq, m_mla_w_ukv, m_w_branch, m_w_out, m_mix_pre_gain, m_mix_post_gain, m_mlp_pre_gain, m_mlp_post_gain, m_w_mlp_in, m_w_mlp_out, v_w_in, v_s5_lam_re, v_s5_lam_im, v_s5_log_step, v_s5_b_re, v_s5_b_im, v_s5_c_re, v_s5_c_im, v_s5_d, v_s5_w_glu, v_s5_b_glu, v_gdn_conv, v_gdn_a_log, v_gdn_dt_bias, v_gdn_o_gain, v_swa_sink, v_t5_bias, v_mla_q_gain, v_mla_kv_gain, v_mla_w_uq, v_mla_w_ukv, v_w_branch, v_w_out, v_mix_pre_gain, v_mix_post_gain, v_mlp_pre_gain, v_mlp_post_gain, v_w_mlp_in, v_w_mlp_out):
    w = dict(zip(WEIGHT_NAMES, (w_in, s5_lam_re, s5_lam_im, s5_log_step, s5_b_re, s5_b_im, s5_c_re, s5_c_im, s5_d, s5_w_glu, s5_b_glu, gdn_conv, gdn_a_log, gdn_dt_bias, gdn_o_gain, swa_sink, t5_bias, mla_q_gain, mla_kv_gain, mla_w_uq, mla_w_ukv, w_branch, w_out, mix_pre_gain, mix_post_gain, mlp_pre_gain, mlp_post_gain, w_mlp_in, w_mlp_out)))
    m = dict(zip(WEIGHT_NAMES, (m_w_in, m_s5_lam_re, m_s5_lam_im, m_s5_log_step, m_s5_b_re, m_s5_b_im, m_s5_c_re, m_s5_c_im, m_s5_d, m_s5_w_glu, m_s5_b_glu, m_gdn_conv, m_gdn_a_log, m_gdn_dt_bias, m_gdn_o_gain, m_swa_sink, m_t5_bias, m_mla_q_gain, m_mla_kv_gain, m_mla_w_uq, m_mla_w_ukv, m_w_branch, m_w_out, m_mix_pre_gain, m_mix_post_gain, m_mlp_pre_gain, m_mlp_post_gain, m_w_mlp_in, m_w_mlp_out)))
    v = dict(zip(WEIGHT_NAMES, (v_w_in, v_s5_lam_re, v_s5_lam_im, v_s5_log_step, v_s5_b_re, v_s5_b_im, v_s5_c_re, v_s5_c_im, v_s5_d, v_s5_w_glu, v_s5_b_glu, v_gdn_conv, v_gdn_a_log, v_gdn_dt_bias, v_gdn_o_gain, v_swa_sink, v_t5_bias, v_mla_q_gain, v_mla_kv_gain, v_mla_w_uq, v_mla_w_ukv, v_w_branch, v_w_out, v_mix_pre_gain, v_mix_post_gain, v_mlp_pre_gain, v_mlp_post_gain, v_w_mlp_in, v_w_mlp_out)))
    sharded = tuple(SHARDED)

    gathered = _two_level_all_gather([w[n].astype(SHARDED[n][1]) for n in sharded], 'weights_all_gather')
    weights = dict(zip(sharded, gathered))
    weights.update({n: w[n] for n in REPLICATED})

    loss, (grads, grad_x) = jax.value_and_grad(_local_loss, argnums=(0, 1))(weights, x[0], loss_target[0])
    loss = lax.psum(loss, ('x', 'y', 'c'))

    core = lax.axis_index('c')
    from_sibling = _exchange([grads[n] for n in sharded], FLIPS_SIBLING, 'swap_half', 'grads_core_swap')
    chip_sums = []
    for n, r in zip(sharded, from_sibling):
        mine = lax.dynamic_slice_in_dim(grads[n], core * r.shape[1], r.shape[1], axis=1)
        chip_sums.append(_pair_add_call(_two_d(mine, 0), _two_d(r, 0), 'grads_core_add').reshape(r.shape))
    per_chip = _exchange(chip_sums, FLIPS_CHIPS, 'scatter', 'grads_chip_scatter')
    halves = [_sum_parts_call(_two_d(parts, 1), 'grads_chip_sum').reshape(parts.shape[1:]) for parts in per_chip]
    both_halves = _exchange(halves, FLIPS_SIBLING, 'gather', 'grads_core_gather')
    out = {}
    for n, g in zip(sharded, both_halves):
        res = _adamw_call(_two_d(g, 0)[None], _two_d(w[n], 0), _two_d(m[n], 0), _two_d(v[n], 0), 'adamw_sharded')
        out[n] = tuple(r.reshape(w[n].shape) for r in res)

    small = [grads[n] for n in REPLICATED]
    all_parts = _exchange([_pack(small)], FLIPS_ALL, 'gather', 'grads_all_gather')[0]
    res = _adamw_call(all_parts, _pack([w[n] for n in REPLICATED]), _pack([m[n] for n in REPLICATED]),
                      _pack([v[n] for n in REPLICATED]), 'adamw_replicated')
    unpacked = [_unpack(r, small) for r in res]
    for i, n in enumerate(REPLICATED):
        out[n] = tuple(u[i] for u in unpacked)

    return (loss, grad_x[None]) + tuple(out[n][k] for k in range(4) for n in WEIGHT_NAMES)
```

```python
import functools
import math

import numpy as np
import jax
import jax.numpy as jnp
from jax import lax
from jax.experimental import pallas as pl
from jax.experimental.pallas import tpu as pltpu

F32 = jnp.float32
BF16 = jnp.bfloat16

VMEM_LIMIT_BYTES = 56 * 1024 * 1024
SUBLANES = 8
LANES = 128

NORM_EPS = 1e-6
DEPTH = 4
N_BRANCHES = 4
S5_GROUP = 16
S5_STATE = 64
GDN_HEAD_DIM = 128
GDN_CONV = 4
GDN_CHUNK = 64
SWA_HEAD_DIM = 64
SWA_KV_GROUP = 4
WINDOW = 128
SWA_BLOCK = 128
T5_BUCKETS = 32
T5_MAX_DISTANCE = 128
MLA_NOPE = 128
MLA_ROPE = 64
MLA_V = 128
ROPE_THETA = 10000.0

ADAM_LR = 0.001
ADAM_B1 = 0.9
ADAM_B2 = 0.999
ADAM_EPS = 1e-08
ADAM_WD = 0.01
ADAM_STEP = 10

_HIGHEST = lax.Precision.HIGHEST


def _params(sem, vmem=VMEM_LIMIT_BYTES):
    return pltpu.CompilerParams(dimension_semantics=sem, vmem_limit_bytes=vmem)


def _tile(dim, target, align):
    if dim <= target:
        return dim
    t = (target // align) * align
    while t >= align:
        if dim % t == 0:
            return t
        t -= align
    return dim


_DOT_DIMS = {"nn": ((1,), (0,)), "nt": ((1,), (1,)), "tn": ((0,), (0,))}


MM_OPERAND_TILE_BYTES = 12 * 1024 * 1024


def _mm(a, b, form="nn", out_dtype=F32, tm=1024, tn=1024, tk=None, relu2_a=False, relu2_grad_of=None, name="mm"):
    if form == "nn":
        (M, K), (K2, N) = a.shape, b.shape
    elif form == "nt":
        (M, K), (N, K2) = a.shape, b.shape
    else:
        (K, M), (K2, N) = a.shape, b.shape
    assert K == K2, (a.shape, b.shape, form)
    tm = _tile(M, tm, LANES if form == "tn" else 16)
    tn = _tile(N, tn, LANES)
    if tk is None:
        tk = MM_OPERAND_TILE_BYTES // (tm * a.dtype.itemsize + tn * b.dtype.itemsize)
    tk = _tile(K, max(LANES, tk // LANES * LANES), LANES)
    nk = K // tk
    dims = (_DOT_DIMS[form], ((), ()))
    n_in = 2 if relu2_grad_of is None else 3

    def dot(a_ref, b_ref):
        av = a_ref[...]
        if relu2_a:
            av = jnp.square(jnp.maximum(av, 0.0))
        return lax.dot_general(av.astype(BF16), b_ref[...].astype(BF16), dims, preferred_element_type=F32)

    def finish(acc, refs):
        if relu2_grad_of is not None:
            acc = acc * (2.0 * jnp.maximum(refs[2][...], 0.0))
        refs[n_in][...] = acc.astype(refs[n_in].dtype)

    def body_single(*refs):
        finish(dot(refs[0], refs[1]), refs)

    def body_acc(*refs):
        acc_ref = refs[n_in + 1]
        k = pl.program_id(2)

        @pl.when(k == 0)
        def _():
            acc_ref[...] = jnp.zeros_like(acc_ref)

        acc_ref[...] += dot(refs[0], refs[1])

        @pl.when(k == nk - 1)
        def _():
            finish(acc_ref[...], refs)

    body = body_single if nk == 1 else body_acc
    extra = [] if relu2_grad_of is None else [relu2_grad_of]
    extra_specs = [pl.BlockSpec((tm, tn), lambda i, j, k: (i, j))] * len(extra)

    if form == "nn":
        a_spec = pl.BlockSpec((tm, tk), lambda i, j, k: (i, k))
        b_spec = pl.BlockSpec((tk, tn), lambda i, j, k: (k, j))
    elif form == "nt":
        a_spec = pl.BlockSpec((tm, tk), lambda i, j, k: (i, k))
        b_spec = pl.BlockSpec((tn, tk), lambda i, j, k: (j, k))
    else:
        a_spec = pl.BlockSpec((tk, tm), lambda i, j, k: (k, i))
        b_spec = pl.BlockSpec((tk, tn), lambda i, j, k: (k, j))
    return pl.pallas_call(
        body,
        out_shape=jax.ShapeDtypeStruct((M, N), out_dtype),
        grid=(M // tm, N // tn, nk),
        in_specs=[a_spec, b_spec] + extra_specs,
        out_specs=pl.BlockSpec((tm, tn), lambda i, j, k: (i, j)),
        scratch_shapes=[] if nk == 1 else [pltpu.VMEM((tm, tn), F32)],
        compiler_params=_params(("parallel", "parallel", "arbitrary")),
        name=name,
    )(a, b, *extra)


@jax.custom_vjp
def relu2_linear(x, w):
    return _mm(x, w, "nn", relu2_a=True, name="mlp_out_fwd")


def _relu2_linear_fwd(x, w):
    return _mm(x, w, "nn", relu2_a=True, name="mlp_out_fwd"), (x, w)


def _relu2_linear_bwd(res, dy):
    x, w = res
    dx = _mm(dy, w, "nt", relu2_grad_of=x, name="mlp_out_dx")
    dw = _mm(x, dy, "tn", out_dtype=w.dtype, relu2_a=True, name="mlp_out_dw")
    return dx, dw


relu2_linear.defvjp(_relu2_linear_fwd, _relu2_linear_bwd)


@functools.partial(jax.custom_vjp, nondiff_argnums=(2,))
def linear(x, w, name):
    return _mm(x, w, "nn", name=name + "_fwd")


def _linear_fwd(x, w, name):
    return _mm(x, w, "nn", name=name + "_fwd"), (x, w)


def _linear_bwd(name, res, dy):
    x, w = res
    dx = _mm(dy, w, "nt", out_dtype=x.dtype, name=name + "_dx")
    dw = _mm(x, dy, "tn", out_dtype=w.dtype, name=name + "_dw")
    return dx, dw


linear.defvjp(_linear_fwd, _linear_bwd)


def _rms_rows(L, D):
    return _tile(L, max(SUBLANES, (1 << 20) // D), SUBLANES)


def _rms_fwd_call(x, gain, name):
    L, D = x.shape
    tm = _rms_rows(L, D)

    def body(x_ref, g_ref, o_ref):
        xv = x_ref[...]
        r = lax.rsqrt(jnp.mean(xv * xv, axis=-1, keepdims=True) + NORM_EPS)
        o_ref[...] = xv * r * g_ref[...]

    return pl.pallas_call(
        body,
        out_shape=jax.ShapeDtypeStruct((L, D), F32),
        grid=(L // tm,),
        in_specs=[pl.BlockSpec((tm, D), lambda i: (i, 0)), pl.BlockSpec((1, D), lambda i: (0, 0))],
        out_specs=pl.BlockSpec((tm, D), lambda i: (i, 0)),
        compiler_params=_params(("parallel",)),
        name=name,
    )(x, gain.reshape(1, D))


def _rms_bwd_call(x, gain, dy, name):
    L, D = x.shape
    tm = _rms_rows(L, D)

    def body(x_ref, g_ref, dy_ref, dx_ref, dg_ref):
        xv = x_ref[...]
        dyv = dy_ref[...]
        r = lax.rsqrt(jnp.mean(xv * xv, axis=-1, keepdims=True) + NORM_EPS)
        xh = xv * r
        dyg = dyv * g_ref[...]
        dx_ref[...] = r * (dyg - xh * jnp.mean(dyg * xh, axis=-1, keepdims=True))

        @pl.when(pl.program_id(0) == 0)
        def _():
            dg_ref[...] = jnp.zeros_like(dg_ref)

        dg_ref[...] += jnp.sum(dyv * xh, axis=0, keepdims=True)

    return pl.pallas_call(
        body,
        out_shape=(jax.ShapeDtypeStruct((L, D), F32), jax.ShapeDtypeStruct((1, D), F32)),
        grid=(L // tm,),
        in_specs=[pl.BlockSpec((tm, D), lambda i: (i, 0)), pl.BlockSpec((1, D), lambda i: (0, 0)),
                  pl.BlockSpec((tm, D), lambda i: (i, 0))],
        out_specs=(pl.BlockSpec((tm, D), lambda i: (i, 0)), pl.BlockSpec((1, D), lambda i: (0, 0))),
        compiler_params=_params(("arbitrary",)),
        name=name,
    )(x, gain.reshape(1, D), dy)


@functools.partial(jax.custom_vjp, nondiff_argnums=(2,))
def rms_norm(x, gain, name):
    return _rms_fwd_call(x, gain, name + "_fwd")


def _rms_norm_fwd(x, gain, name):
    return _rms_fwd_call(x, gain, name + "_fwd"), (x, gain)


def _rms_norm_bwd(name, res, dy):
    x, gain = res
    dx, dg = _rms_bwd_call(x, gain, dy, name + "_bwd")
    return dx, dg.reshape(gain.shape)


rms_norm.defvjp(_rms_norm_fwd, _rms_norm_bwd)


SCAN_COLS = 512
SCAN_ROWS = 256
SCAN_UNROLL = 8


def _to_scan_cols(t, nd):
    P = t.shape[-1]
    pc = _tile(P, SCAN_COLS, LANES)
    lead = t.shape[:-3]
    t = t.reshape(lead + (nd, 2, P // pc, pc))
    t = jnp.swapaxes(t, -3, -2)
    return t.reshape(lead + (nd * 2 * P,))


def _scan_call(b, a, nd, reverse_dir0, name):
    L, NC = b.shape
    P = NC // (2 * nd)
    pc = _tile(P, SCAN_COLS, LANES)
    ncb = P // pc
    T = _tile(L, SCAN_ROWS, SUBLANES)
    nt = L // T
    rev0 = 1 if reverse_dir0 else 0
    G = SUBLANES

    def rev_of(d):
        return d + rev0 - 2 * d * rev0

    a3 = a.reshape(nd * ncb, 2, pc)
    cmul = lambda x, y: (x[0] * y[0] - x[1] * y[1], x[0] * y[1] + x[1] * y[0])
    pows = [(a3[:, 0], a3[:, 1])]
    for _ in range(G - 1):
        pows.append(cmul(pows[-1], pows[0]))
    cols = lambda t: jnp.stack(t, axis=1).reshape(nd, NC // nd)
    steps = jnp.stack([cols(pows[0]), cols(pows[1]), cols(pows[3])], axis=0).reshape(3, NC)
    with_time = jnp.stack([cols(p) for p in pows], axis=0)
    carry_pows = jnp.concatenate([with_time[::-1, d] if rev_of(d) else with_time[:, d] for d in range(nd)], axis=1)

    def body(b_ref, step_ref, pow_ref, o_ref, carry_ref):
        rv = rev_of(pl.program_id(0))

        @pl.when(pl.program_id(2) == 0)
        def _():
            carry_ref[...] = jnp.zeros_like(carry_ref)

        row = lax.broadcasted_iota(jnp.int32, (G, pc), 0)
        mults = [(step_ref[j:j + 1, :pc], step_ref[j:j + 1, pc:]) for j in range(3)]
        pr, pi = pow_ref[:, :pc], pow_ref[:, pc:]

        def run(backwards):
            def shifted(x, k):
                if backwards:
                    return jnp.where(row < G - k, pltpu.roll(x, G - k, 0), 0.0)
                return jnp.where(row >= k, pltpu.roll(x, k, 0), 0.0)

            def group(g, carry):
                cr, ci = carry
                r0 = pl.multiple_of((T // G - 1 - g if backwards else g) * G, G)
                x = b_ref[pl.ds(r0, G), :]
                xr, xi = x[:, :pc], x[:, pc:]
                for j, (mr, mi) in enumerate(mults):
                    sr, si = shifted(xr, 1 << j), shifted(xi, 1 << j)
                    xr, xi = xr + (mr * sr - mi * si), xi + (mr * si + mi * sr)
                o_ref[pl.ds(r0, G), :] = jnp.concatenate([xr + (pr * cr - pi * ci), xi + (pr * ci + pi * cr)], axis=1)
                last = o_ref[pl.ds(r0 + (0 if backwards else G - 1), 1), :]
                return last[:, :pc], last[:, pc:]

            cr, ci = lax.fori_loop(0, T // G, group, (carry_ref[:, :pc], carry_ref[:, pc:]), unroll=SCAN_UNROLL)
            carry_ref[...] = jnp.concatenate([cr, ci], axis=1)

        pl.when(rv == 0)(functools.partial(run, False))
        pl.when(rv == 1)(functools.partial(run, True))

    def t_idx(d, i):
        return i + rev_of(d) * (nt - 1 - 2 * i)

    spec = pl.BlockSpec((T, 2 * pc), lambda d, c, i: (t_idx(d, i), d * ncb + c))
    return pl.pallas_call(
        body,
        out_shape=jax.ShapeDtypeStruct(b.shape, F32),
        grid=(nd, ncb, nt),
        in_specs=[spec, pl.BlockSpec((3, 2 * pc), lambda d, c, i: (0, d * ncb + c)),
                  pl.BlockSpec((G, 2 * pc), lambda d, c, i: (0, d * ncb + c))],
        out_specs=spec,
        scratch_shapes=[pltpu.VMEM((1, 2 * pc), F32)],
        compiler_params=_params(("arbitrary", "arbitrary", "arbitrary")),
        name=name,
    )(b, steps, carry_pows)


def _scan_da_call(g, sp, nd, name):
    L, NC = g.shape
    P = NC // (2 * nd)
    pc = _tile(P, SCAN_COLS, LANES)
    T = _tile(L, SCAN_ROWS * 2, SUBLANES)

    def body(g_ref, s_ref, o_ref):
        @pl.when(pl.program_id(1) == 0)
        def _():
            o_ref[...] = jnp.zeros_like(o_ref)

        gr, gi = g_ref[:, :pc], g_ref[:, pc:]
        sr, si = s_ref[:, :pc], s_ref[:, pc:]
        dar = jnp.sum(gr * sr + gi * si, axis=0, keepdims=True)
        dai = jnp.sum(gi * sr - gr * si, axis=0, keepdims=True)
        o_ref[...] += jnp.concatenate([dar, dai], axis=1)

    spec = pl.BlockSpec((T, 2 * pc), lambda c, i: (i, c))
    return pl.pallas_call(
        body,
        out_shape=jax.ShapeDtypeStruct((1, NC), F32),
        grid=(NC // (2 * pc), L // T),
        in_specs=[spec, spec],
        out_specs=pl.BlockSpec((1, 2 * pc), lambda c, i: (0, c)),
        compiler_params=_params(("arbitrary", "arbitrary")),
        name=name,
    )(g, sp)


def _conj_cols(a, nd):
    P = a.shape[-1] // (2 * nd)
    pc = _tile(P, SCAN_COLS, LANES)
    sign = jnp.tile(jnp.concatenate([jnp.ones((pc,), F32), -jnp.ones((pc,), F32)]), a.shape[-1] // (2 * pc))
    return a * sign


@functools.partial(jax.custom_vjp, nondiff_argnums=(2,))
def s5_scan(b, a, nd):
    return _scan_call(b, a, nd, False, "s5_scan_fwd")


def _s5_scan_fwd(b, a, nd):
    s = _scan_call(b, a, nd, False, "s5_scan_fwd")
    return s, (s, a)


def _s5_scan_bwd(nd, res, ds):
    s, a = res
    g = _scan_call(ds, _conj_cols(a, nd), nd, True, "s5_scan_adj")
    L, NC = s.shape
    half = NC // nd
    zero = jnp.zeros((1, half), F32)
    prev = [jnp.concatenate([zero, s[:-1, :half]], axis=0)]
    if nd == 2:
        prev.append(jnp.concatenate([s[1:, half:], zero], axis=0))
    sp = jnp.concatenate(prev, axis=1)
    da = _scan_da_call(g, sp, nd, "s5_scan_da")
    return g, da


s5_scan.defvjp(_s5_scan_fwd, _s5_scan_bwd)


LOG2E = math.log2(math.e)
ATTN_TQ = 512
ATTN_TK = 4096
ATTN_BWD_TQ = 512
ATTN_BWD_TK = 4096
ATTN_SUB = 2


def _attn_fwd_call(q, k, v, scale):
    H, L, DQ = q.shape
    DV = v.shape[-1]
    tq = _tile(L, ATTN_TQ, LANES)
    tk = _tile(L, ATTN_TK, LANES)
    nk = L // tk

    def body(q_ref, k_ref, v_ref, o_ref, lse_ref, m_s, l_s, acc_s):
        j = pl.program_id(2)

        @pl.when(j == 0)
        def _():
            m_s[...] = jnp.full_like(m_s, -jnp.inf)
            l_s[...] = jnp.zeros_like(l_s)
            acc_s[...] = jnp.zeros_like(acc_s)

        s = lax.dot_general(q_ref[0].astype(BF16), k_ref[0].astype(BF16), (((1,), (1,)), ((), ())),
                            preferred_element_type=F32) * (scale * LOG2E)
        m_old = m_s[...]
        m_new = jnp.maximum(m_old, jnp.max(s, axis=1, keepdims=True))
        alpha = jnp.exp2(m_old - m_new)
        p = jnp.exp2(s - m_new)
        l_s[...] = alpha * l_s[...] + jnp.sum(p, axis=1, keepdims=True)
        acc_s[...] = alpha * acc_s[...] + jnp.dot(p.astype(BF16), v_ref[0].astype(BF16), preferred_element_type=F32)
        m_s[...] = m_new

        @pl.when(j == nk - 1)
        def _():
            o_ref[0] = acc_s[...] / l_s[...]
            lse_ref[0] = m_s[...] + jnp.log(l_s[...]) * LOG2E

    return pl.pallas_call(
        body,
        out_shape=(jax.ShapeDtypeStruct((H, L, DV), F32), jax.ShapeDtypeStruct((H, L, 1), F32)),
        grid=(H, L // tq, nk),
        in_specs=[pl.BlockSpec((1, tq, DQ), lambda h, i, j: (h, i, 0)),
                  pl.BlockSpec((1, tk, DQ), lambda h, i, j: (h, j, 0)),
                  pl.BlockSpec((1, tk, DV), lambda h, i, j: (h, j, 0))],
        out_specs=(pl.BlockSpec((1, tq, DV), lambda h, i, j: (h, i, 0)),
                   pl.BlockSpec((1, tq, 1), lambda h, i, j: (h, i, 0))),
        scratch_shapes=[pltpu.VMEM((tq, 1), F32), pltpu.VMEM((tq, 1), F32), pltpu.VMEM((tq, DV), F32)],
        compiler_params=_params(("parallel", "parallel", "arbitrary")),
        name="mla_attn_fwd",
    )(q, k, v)


def _attn_bwd_call(q, k, v, do, lse, delta, scale):
    H, L, DQ = q.shape
    DV = v.shape[-1]
    tq = _tile(L, ATTN_BWD_TQ, LANES)
    tk = _tile(L, ATTN_BWD_TK, LANES)
    nq = L // tq

    def body(q_ref, k_ref, v_ref, do_ref, lse_ref, dl_ref, dq_ref, dk_ref, dv_ref, dk_s, dv_s):
        j = pl.program_id(1)
        i = pl.program_id(2)

        @pl.when(jnp.logical_and(j == 0, i == 0))
        def _():
            dq_ref[...] = jnp.zeros_like(dq_ref)

        @pl.when(i == 0)
        def _():
            dk_s[...] = jnp.zeros_like(dk_s)
            dv_s[...] = jnp.zeros_like(dv_s)

        kb = k_ref[0].astype(BF16)
        vb = v_ref[0].astype(BF16)
        w = tq // ATTN_SUB
        R = range(ATTN_SUB)
        sub = [pl.ds(a * w, w) for a in R]
        nt, tn = (((1,), (1,)), ((), ())), (((0,), (0,)), ((), ()))
        qb = [q_ref[0, sub[a], :].astype(BF16) for a in R]
        dob = [do_ref[0, sub[a], :].astype(BF16) for a in R]
        s = [lax.dot_general(qb[a], kb, nt, preferred_element_type=F32) * (scale * LOG2E) for a in R]
        dp = [lax.dot_general(dob[a], vb, nt, preferred_element_type=F32) for a in R]
        p = [jnp.exp2(s[a] - lse_ref[0, sub[a], :]) for a in R]
        ds = [(p[a] * (dp[a] - dl_ref[0, sub[a], :]) * scale).astype(BF16) for a in R]
        dv = dv_s[...]
        dk = dk_s[...]
        for a in R:
            dv = dv + lax.dot_general(p[a].astype(BF16), dob[a], tn, preferred_element_type=F32)
            dk = dk + lax.dot_general(ds[a], qb[a], tn, preferred_element_type=F32)
        dv_s[...] = dv
        dk_s[...] = dk
        for a in R:
            rows = pl.ds(pl.multiple_of(i * tq + a * w, w), w)
            dq_ref[0, rows, :] += jnp.dot(ds[a], kb, preferred_element_type=F32)

        @pl.when(i == nq - 1)
        def _():
            dk_ref[0] = dk_s[...]
            dv_ref[0] = dv_s[...]

    return pl.pallas_call(
        body,
        out_shape=(jax.ShapeDtypeStruct((H, L, DQ), F32), jax.ShapeDtypeStruct((H, L, DQ), F32),
                   jax.ShapeDtypeStruct((H, L, DV), F32)),
        grid=(H, L // tk, nq),
        in_specs=[pl.BlockSpec((1, tq, DQ), lambda h, j, i: (h, i, 0)),
                  pl.BlockSpec((1, tk, DQ), lambda h, j, i: (h, j, 0)),
                  pl.BlockSpec((1, tk, DV), lambda h, j, i: (h, j, 0)),
                  pl.BlockSpec((1, tq, DV), lambda h, j, i: (h, i, 0)),
                  pl.BlockSpec((1, tq, 1), lambda h, j, i: (h, i, 0)),
                  pl.BlockSpec((1, tq, 1), lambda h, j, i: (h, i, 0))],
        out_specs=(pl.BlockSpec((1, L, DQ), lambda h, j, i: (h, 0, 0)),
                   pl.BlockSpec((1, tk, DQ), lambda h, j, i: (h, j, 0)),
                   pl.BlockSpec((1, tk, DV), lambda h, j, i: (h, j, 0))),
        scratch_shapes=[pltpu.VMEM((tk, DQ), F32), pltpu.VMEM((tk, DV), F32)],
        compiler_params=_params(("arbitrary", "arbitrary", "arbitrary")),
        name="mla_attn_bwd",
    )(q, k, v, do, lse, delta)


@functools.partial(jax.custom_vjp, nondiff_argnums=(3,))
def full_attention(q, k, v, scale):
    return _full_attention_fwd(q, k, v, scale)[0]


def _full_attention_fwd(q, k, v, scale):
    q, k, v = q.astype(BF16), k.astype(BF16), v.astype(BF16)
    o, lse = _attn_fwd_call(q, k, v, scale)
    return o, (q, k, v, o, lse)


def _full_attention_bwd(scale, res, do):
    q, k, v, o, lse = res
    delta = jnp.sum(do * o, axis=-1, keepdims=True)
    return _attn_bwd_call(q, k, v, do.astype(BF16), lse, delta, scale)


full_attention.defvjp(_full_attention_fwd, _full_attention_bwd)


_NN = ((1,), (0,))
_NT = ((1,), (1,))
_TN = ((0,), (0,))
_DOT_VJP = {_NN: (("g", "b", _NT), ("a", "g", _TN)),
            _NT: (("g", "b", _NN), ("g", "a", _TN)),
            _TN: (("b", "g", _NT), ("a", "g", _NN))}


def _dot1(a, b, dims):
    return lax.dot_general(a.astype(BF16), b.astype(BF16), (dims, ((), ())), preferred_element_type=F32)


def _split_bf16(t):
    hi = t.astype(BF16)
    return hi, (t - hi.astype(F32)).astype(BF16)


def _dot3_raw(a, b, dims):
    ah, al = _split_bf16(a)
    bh, bl = _split_bf16(b)
    d = lambda p, q: lax.dot_general(p, q, (dims, ((), ())), preferred_element_type=F32)
    return d(ah, bh) + (d(ah, bl) + d(al, bh))


def _with_f32_cotangents(raw):
    @functools.partial(jax.custom_vjp, nondiff_argnums=(2,))
    def dot(a, b, dims):
        return raw(a, b, dims)

    def fwd(a, b, dims):
        return raw(a, b, dims), (a, b)

    def bwd(dims, res, g):
        a, b = res
        ops = {"a": a, "b": b, "g": g}
        (p, q, dp), (r, s, dr) = _DOT_VJP[dims]
        return raw(ops[p], ops[q], dp), raw(ops[r], ops[s], dr)

    dot.defvjp(fwd, bwd)
    return dot


def _halves(t, axis):
    n = t.shape[axis] // 2
    return lax.slice_in_dim(t, 0, n, axis=axis), lax.slice_in_dim(t, n, 2 * n, axis=axis)


@functools.partial(jax.custom_vjp, nondiff_argnums=(2,))
def _stack(a, b, axis):
    return jnp.concatenate([a, b], axis=axis)


_stack.defvjp(lambda a, b, axis: (jnp.concatenate([a, b], axis=axis), None),
              lambda axis, _, g: _halves(g, axis))


@functools.partial(jax.custom_vjp, nondiff_argnums=(1,))
def _unstack(t, axis):
    return _halves(t, axis)


_unstack.defvjp(lambda t, axis: (_halves(t, axis), None),
                lambda axis, _, g: (jnp.concatenate(list(g), axis=axis),))


_bdot = _with_f32_cotangents(_dot1)
_dot3 = _with_f32_cotangents(_dot3_raw)


def _swa_heads(q, k, v, bias, sink, valid):
    R = range(len(q))
    kv = [h // SWA_KV_GROUP for h in R]
    s = [_bdot(q[h], k[kv[h]], _NT) * (SWA_HEAD_DIM ** -0.5) + bias[h] for h in R]
    s = [jnp.where(valid, x, -1e30) for x in s]
    m = [lax.stop_gradient(jnp.maximum(jnp.max(s[h], axis=1, keepdims=True), sink[h])) for h in R]
    p = [jnp.exp(s[h] - m[h]) for h in R]
    den = [jnp.sum(p[h], axis=1, keepdims=True) + jnp.exp(sink[h] - m[h]) for h in R]
    return [_bdot(p[h] / den[h], v[kv[h]], _NN) for h in R]


def _swa_valid(n, L):
    qi = lax.broadcasted_iota(jnp.int32, (SWA_BLOCK, 3 * SWA_BLOCK), 0)
    sj = lax.broadcasted_iota(jnp.int32, (SWA_BLOCK, 3 * SWA_BLOCK), 1)
    rel = sj - SWA_BLOCK - qi
    kpos = n * SWA_BLOCK + sj - SWA_BLOCK
    return (jnp.abs(rel) <= WINDOW) & (kpos >= 0) & (kpos < L)


def _swa_specs(HQ, HKV, L):
    B = SWA_BLOCK
    q_spec = pl.BlockSpec((HQ, B, SWA_HEAD_DIM), lambda n: (0, n, 0))
    kv_spec = pl.BlockSpec((HKV, L + 2 * B, SWA_HEAD_DIM), lambda n: (0, 0, 0))
    bias_spec = pl.BlockSpec((HQ, B, 3 * B), lambda n: (0, 0, 0))
    sink_spec = pl.BlockSpec((HQ, B, 1), lambda n: (0, 0, 0))
    return q_spec, kv_spec, bias_spec, sink_spec


def _swa_fwd_call(q, kpad, vpad, bias, sink):
    HQ, L, _ = q.shape
    HKV = kpad.shape[0]
    B = SWA_BLOCK

    def body(q_ref, k_ref, v_ref, b_ref, s_ref, o_ref):
        n = pl.program_id(0)
        rows = pl.ds(pl.multiple_of(n * B, B), 3 * B)
        out = _swa_heads([q_ref[h] for h in range(HQ)], [k_ref[g, rows, :] for g in range(HKV)],
                         [v_ref[g, rows, :] for g in range(HKV)], [b_ref[h] for h in range(HQ)],
                         [s_ref[h] for h in range(HQ)], _swa_valid(n, L))
        for h in range(HQ):
            o_ref[h] = out[h]

    q_spec, kv_spec, bias_spec, sink_spec = _swa_specs(HQ, HKV, L)
    return pl.pallas_call(
        body,
        out_shape=jax.ShapeDtypeStruct(q.shape, F32),
        grid=(L // B,),
        in_specs=[q_spec, kv_spec, kv_spec, bias_spec, sink_spec],
        out_specs=q_spec,
        compiler_params=_params(("parallel",)),
        name="swa_fwd",
    )(q, kpad, vpad, bias, sink)


def _swa_bwd_call(q, kpad, vpad, bias, sink, do):
    HQ, L, _ = q.shape
    HKV = kpad.shape[0]
    B = SWA_BLOCK

    def body(q_ref, k_ref, v_ref, b_ref, s_ref, do_ref, dq_ref, dk_ref, dv_ref, db_ref, ds_ref):
        n = pl.program_id(0)

        @pl.when(n == 0)
        def _():
            dk_ref[...] = jnp.zeros_like(dk_ref)
            dv_ref[...] = jnp.zeros_like(dv_ref)
            db_ref[...] = jnp.zeros_like(db_ref)
            ds_ref[...] = jnp.zeros_like(ds_ref)

        rows = pl.ds(pl.multiple_of(n * B, B), 3 * B)
        _, vjp = jax.vjp(functools.partial(_swa_heads, valid=_swa_valid(n, L)),
                         [q_ref[h] for h in range(HQ)], [k_ref[g, rows, :] for g in range(HKV)],
                         [v_ref[g, rows, :] for g in range(HKV)], [b_ref[h] for h in range(HQ)],
                         [s_ref[h] for h in range(HQ)])
        dq, dk, dv, db, dsk = vjp([do_ref[h] for h in range(HQ)])
        for h in range(HQ):
            dq_ref[h] = dq[h]
            db_ref[h] += db[h]
            ds_ref[h] += dsk[h]
        for g in range(HKV):
            dk_ref[g, rows, :] += dk[g]
            dv_ref[g, rows, :] += dv[g]

    q_spec, kv_spec, bias_spec, sink_spec = _swa_specs(HQ, HKV, L)
    return pl.pallas_call(
        body,
        out_shape=(jax.ShapeDtypeStruct(q.shape, F32), jax.ShapeDtypeStruct(kpad.shape, F32),
                   jax.ShapeDtypeStruct(vpad.shape, F32), jax.ShapeDtypeStruct(bias.shape, F32),
                   jax.ShapeDtypeStruct(sink.shape, F32)),
        grid=(L // B,),
        in_specs=[q_spec, kv_spec, kv_spec, bias_spec, sink_spec, q_spec],
        out_specs=(q_spec, kv_spec, kv_spec, bias_spec, sink_spec),
        compiler_params=_params(("arbitrary",)),
        name="swa_bwd",
    )(q, kpad, vpad, bias, sink, do)


@jax.custom_vjp
def window_attention(q, kpad, vpad, bias, sink):
    return _swa_fwd_call(q, kpad, vpad, bias, sink)


def _window_attention_fwd(q, kpad, vpad, bias, sink):
    return _swa_fwd_call(q, kpad, vpad, bias, sink), (q, kpad, vpad, bias, sink)


def _window_attention_bwd(res, do):
    return _swa_bwd_call(*res, do)


window_attention.defvjp(_window_attention_fwd, _window_attention_bwd)


def _unit_tri_inverses(a):
    C = a[0].shape[0]
    ii = lax.broadcasted_iota(jnp.int32, (C, C), 0)
    jj = lax.broadcasted_iota(jnp.int32, (C, C), 1)
    pw = [-x for x in a]
    t = [jnp.where(ii == jj, 1.0, 0.0)] * len(a)
    for _ in range(int(math.log2(C))):
        both = [_dot3_raw(jnp.concatenate([ts, ps], axis=0), ps, _NN) for ts, ps in zip(t, pw)]
        t = [ts + b[:C] for ts, b in zip(t, both)]
        pw = [b[C:] for b in both]
    return t


@jax.custom_vjp
def _known_inverse(a, t):
    return t


def _known_inverse_fwd(a, t):
    return t, t


def _known_inverse_bwd(t, dt):
    da = -_dot3_raw(_dot3_raw(t, dt, _TN), t, _NT)
    return da, jnp.zeros_like(t)


_known_inverse.defvjp(_known_inverse_fwd, _known_inverse_bwd)


def _gdn_chunks(S, q, k, v, gc, gr, gl, beta, t_known, backwards):
    R = range(len(q))
    C = q[0].shape[0]
    ii = lax.broadcasted_iota(jnp.int32, (C, C), 0)
    jj = lax.broadcasted_iota(jnp.int32, (C, C), 1)
    causal = [(ii <= jj) if backwards[s] else (ii >= jj) for s in R]
    strict = [(ii < jj) if backwards[s] else (ii > jj) for s in R]
    decay = [jnp.where(causal[s], jnp.exp(jnp.where(causal[s], gc[s] - gr[s], 0.0)), 0.0) for s in R]
    kb = [k[s] * beta[s] for s in R]
    kk_qk = [_unstack(_bdot(_stack(kb[s], q[s], 0), k[s], _NT), 0) for s in R]
    a = [jnp.where(strict[s], kk_qk[s][0] * decay[s], 0.0) for s in R]
    t = _unit_tri_inverses(a) if t_known is None else [_known_inverse(a[s], t_known[s]) for s in R]
    u_w = [_unstack(_dot3(t[s], _stack(v[s] * beta[s], kb[s] * jnp.exp(gc[s]), 1), _NN), 1) for s in R]
    ws_qs = [_unstack(_bdot(_stack(u_w[s][1], q[s] * jnp.exp(gc[s]), 0), S[s], _NN), 0) for s in R]
    v_new = [u_w[s][0] - ws_qs[s][0] for s in R]
    o = [ws_qs[s][1] + _bdot(kk_qk[s][1] * decay[s], v_new[s], _NN) for s in R]
    s_new = [S[s] * jnp.exp(gl[s]) + _bdot(k[s] * jnp.exp(gl[s] - gc[s]), v_new[s], _TN) for s in R]
    return s_new, o, t


def _gdn_specs(H, L, C, Dh, flip):
    NC = L // C
    idx = (lambda c: NC - 1 - c) if flip else (lambda c: c)
    seq = pl.BlockSpec((C, H * Dh), lambda c: (idx(c), 0))

    def scalar(rows, cols, group):
        return pl.BlockSpec((H, 1, rows, cols), lambda c: (group, idx(c), 0, 0))

    return seq, scalar


def _gdn_fwd_call(q, k, v, gc, gr, gl, beta):
    L, HD = q.shape
    C = gc.shape[2]
    NC = L // C
    H = gc.shape[0] // 2
    Dh = HD // H

    def body(qf, kf, vf, qb, kb, vb, gcf, grf, glf, bf, gcb, grb, glb, bb,
             of_ref, ob_ref, stf_ref, stb_ref, tf_ref, tb_ref, s_scr):
        @pl.when(pl.program_id(0) == 0)
        def _():
            s_scr[...] = jnp.zeros_like(s_scr)

        groups = ((qf, kf, vf, gcf, grf, glf, bf), (qb, kb, vb, gcb, grb, glb, bb))
        outs = ((of_ref, stf_ref, tf_ref), (ob_ref, stb_ref, tb_ref))
        seqs = [(d, h) for d in range(2) for h in range(H)]
        cols = [slice(h * Dh, (h + 1) * Dh) for h in range(H)]
        s0 = [s_scr[d * H + h] for d, h in seqs]
        for (d, h), s in zip(seqs, s0):
            outs[d][1][h, 0] = s
        seq_in = lambda j: [groups[d][j][:, cols[h]] for d, h in seqs]
        scal_in = lambda j: [groups[d][j][h, 0] for d, h in seqs]
        s1, o, t = _gdn_chunks(s0, seq_in(0), seq_in(1), seq_in(2), scal_in(3), scal_in(4), scal_in(5), scal_in(6),
                               None, [d == 1 for d, _ in seqs])
        for i, (d, h) in enumerate(seqs):
            s_scr[d * H + h] = s1[i]
            outs[d][0][:, cols[h]] = o[i]
            outs[d][2][h, 0] = t[i]

    seq_f, sc_f = _gdn_specs(H, L, C, Dh, False)
    seq_b, sc_b = _gdn_specs(H, L, C, Dh, True)
    sds = jax.ShapeDtypeStruct
    return pl.pallas_call(
        body,
        out_shape=(sds((L, HD), F32), sds((L, HD), F32), sds((H, NC, Dh, Dh), F32), sds((H, NC, Dh, Dh), F32),
                   sds((H, NC, C, C), F32), sds((H, NC, C, C), F32)),
        grid=(NC,),
        in_specs=[seq_f, seq_f, seq_f, seq_b, seq_b, seq_b,
                  sc_f(C, 1, 0), sc_f(1, C, 0), sc_f(1, 1, 0), sc_f(C, 1, 0),
                  sc_b(C, 1, 1), sc_b(1, C, 1), sc_b(1, 1, 1), sc_b(C, 1, 1)],
        out_specs=(seq_f, seq_b, sc_f(Dh, Dh, 0), sc_b(Dh, Dh, 0), sc_f(C, C, 0), sc_b(C, C, 0)),
        scratch_shapes=[pltpu.VMEM((2 * H, Dh, Dh), F32)],
        compiler_params=_params(("arbitrary",)),
        name="gdn_fwd",
    )(q, k, v, q, k, v, gc, gr, gl, beta, gc, gr, gl, beta)


def _gdn_bwd_call(q, k, v, gc, gr, gl, beta, st_f, st_b, t_f, t_b, do_f, do_b):
    L, HD = q.shape
    C = gc.shape[2]
    NC = L // C
    H = gc.shape[0] // 2
    Dh = HD // H

    def body(qf, kf, vf, qb, kb, vb, gcf, grf, glf, bf, gcb, grb, glb, bb, stf, stb, tf, tb, dof, dob,
             dqf, dkf, dvf, dqb, dkb, dvb, dgcf, dgrf, dglf, dbf, dgcb, dgrb, dglb, dbb, ds_scr):
        @pl.when(pl.program_id(0) == 0)
        def _():
            ds_scr[...] = jnp.zeros_like(ds_scr)

        groups = ((qf, kf, vf, gcf, grf, glf, bf, stf, tf, dof), (qb, kb, vb, gcb, grb, glb, bb, stb, tb, dob))
        outs = ((dqf, dkf, dvf, dgcf, dgrf, dglf, dbf), (dqb, dkb, dvb, dgcb, dgrb, dglb, dbb))
        seqs = [(d, h) for d in range(2) for h in range(H)]
        cols = [slice(h * Dh, (h + 1) * Dh) for h in range(H)]
        seq_in = lambda j: [groups[d][j][:, cols[h]] for d, h in seqs]
        scal_in = lambda j: [groups[d][j][h, 0] for d, h in seqs]
        t_known = scal_in(8)
        backwards = [d == 1 for d, _ in seqs]

        def chunks(*args):
            return _gdn_chunks(*args, t_known, backwards)[:2]

        _, vjp = jax.vjp(chunks, scal_in(7), seq_in(0), seq_in(1), seq_in(2), scal_in(3), scal_in(4), scal_in(5),
                         scal_in(6))
        grads = vjp(([ds_scr[d * H + h] for d, h in seqs], seq_in(9)))
        for i, (d, h) in enumerate(seqs):
            ds_scr[d * H + h] = grads[0][i]
            for j in range(3):
                outs[d][j][:, cols[h]] = grads[1 + j][i]
            for j in range(3, 7):
                outs[d][j][h, 0] = grads[1 + j][i]

    seq_f, sc_f = _gdn_specs(H, L, C, Dh, True)
    seq_b, sc_b = _gdn_specs(H, L, C, Dh, False)
    sds = jax.ShapeDtypeStruct
    seq_out = sds((L, HD), F32)
    half = lambda t: sds((H,) + t.shape[1:], F32)
    scal_f = [sc_f(C, 1, 0), sc_f(1, C, 0), sc_f(1, 1, 0), sc_f(C, 1, 0)]
    scal_b = [sc_b(C, 1, 1), sc_b(1, C, 1), sc_b(1, 1, 1), sc_b(C, 1, 1)]
    scal_b_out = [sc_b(C, 1, 0), sc_b(1, C, 0), sc_b(1, 1, 0), sc_b(C, 1, 0)]
    return pl.pallas_call(
        body,
        out_shape=(seq_out,) * 6 + (half(gc), half(gr), half(gl), half(beta)) * 2,
        grid=(NC,),
        in_specs=[seq_f, seq_f, seq_f, seq_b, seq_b, seq_b] + scal_f + scal_b
                 + [sc_f(Dh, Dh, 0), sc_b(Dh, Dh, 0), sc_f(C, C, 0), sc_b(C, C, 0), seq_f, seq_b],
        out_specs=[seq_f, seq_f, seq_f, seq_b, seq_b, seq_b] + scal_f + scal_b_out,
        scratch_shapes=[pltpu.VMEM((2 * H, Dh, Dh), F32)],
        compiler_params=_params(("arbitrary",)),
        name="gdn_bwd",
    )(q, k, v, q, k, v, gc, gr, gl, beta, gc, gr, gl, beta, st_f, st_b, t_f, t_b, do_f, do_b)


@jax.custom_vjp
def gated_delta_rule(q, k, v, gc, gr, gl, beta):
    return _gdn_fwd_call(q, k, v, gc, gr, gl, beta)[:2]


def _gated_delta_rule_fwd(q, k, v, gc, gr, gl, beta):
    o_f, o_b, st_f, st_b, t_f, t_b = _gdn_fwd_call(q, k, v, gc, gr, gl, beta)
    return (o_f, o_b), (q, k, v, gc, gr, gl, beta, st_f, st_b, t_f, t_b)


def _gated_delta_rule_bwd(res, do):
    (dqf, dkf, dvf, dqb, dkb, dvb, dgcf, dgrf, dglf, dbf, dgcb, dgrb, dglb, dbb) = _gdn_bwd_call(*res, *do)
    cat = lambda a, b: jnp.concatenate([a, b], axis=0)
    return dqf + dqb, dkf + dkb, dvf + dvb, cat(dgcf, dgcb), cat(dgrf, dgrb), cat(dglf, dglb), cat(dbf, dbb)


gated_delta_rule.defvjp(_gated_delta_rule_fwd, _gated_delta_rule_bwd)


GATE_ROWS = 128


def _gate_specs(L, D, nb):
    tm = _tile(L, GATE_ROWS, SUBLANES)
    logit_specs = [pl.BlockSpec((tm, D), functools.partial(lambda i, b: (i, b), b=b)) for b in range(nb)]
    row_spec = pl.BlockSpec((tm, D), lambda i: (i, 0))
    return tm, logit_specs, row_spec


def _gate_fwd_call(logits, branches):
    nb = len(branches)
    L, D = branches[0].shape
    tm, logit_specs, row_spec = _gate_specs(L, D, nb)

    def body(*refs):
        o_ref = refs[2 * nb]
        acc = jax.nn.sigmoid(refs[0][...]) * refs[nb][...]
        for b in range(1, nb):
            acc = acc + jax.nn.sigmoid(refs[b][...]) * refs[nb + b][...]
        o_ref[...] = acc

    return pl.pallas_call(
        body, out_shape=jax.ShapeDtypeStruct((L, D), F32), grid=(L // tm,),
        in_specs=logit_specs + [row_spec] * nb, out_specs=row_spec,
        compiler_params=_params(("parallel",)), name="gate_merge_fwd",
    )(*([logits] * nb), *branches)


def _gate_bwd_call(logits, branches, dm):
    nb = len(branches)
    L, D = branches[0].shape
    tm, logit_specs, row_spec = _gate_specs(L, D, nb)

    def body(*refs):
        dm_v = refs[2 * nb][...]
        dl_ref = refs[2 * nb + 1]
        for b in range(nb):
            sig = jax.nn.sigmoid(refs[b][...])
            refs[2 * nb + 2 + b][...] = dm_v * sig
            dl_ref[:, b * D:(b + 1) * D] = dm_v * refs[nb + b][...] * (sig * (1.0 - sig))

    return pl.pallas_call(
        body,
        out_shape=[jax.ShapeDtypeStruct(logits.shape, F32)] + [jax.ShapeDtypeStruct((L, D), F32)] * nb,
        grid=(L // tm,),
        in_specs=logit_specs + [row_spec] * (nb + 1),
        out_specs=[pl.BlockSpec((tm, nb * D), lambda i: (i, 0))] + [row_spec] * nb,
        compiler_params=_params(("parallel",)), name="gate_merge_bwd",
    )(*([logits] * nb), *branches, dm)


@jax.custom_vjp
def gate_merge(logits, branches):
    return _gate_fwd_call(logits, branches)


def _gate_merge_fwd(logits, branches):
    return _gate_fwd_call(logits, branches), (logits, branches)


def _gate_merge_bwd(res, dm):
    logits, branches = res
    out = _gate_bwd_call(logits, branches, dm)
    return out[0], tuple(out[1:])


gate_merge.defvjp(_gate_merge_fwd, _gate_merge_bwd)


def _loss_fwd_call(y, t):
    L, D = y.shape
    tm = _rms_rows(L, D)

    def body(y_ref, t_ref, o_ref):
        @pl.when(pl.program_id(0) == 0)
        def _():
            o_ref[...] = jnp.zeros_like(o_ref)

        e = y_ref[...] - t_ref[...]
        part = jnp.sum(jnp.sum(e * e, axis=1, keepdims=True), axis=0, keepdims=True)
        o_ref[...] += part * (0.5 / D)

    out = pl.pallas_call(
        body,
        out_shape=jax.ShapeDtypeStruct((SUBLANES, LANES), F32),
        grid=(L // tm,),
        in_specs=[pl.BlockSpec((tm, D), lambda i: (i, 0)), pl.BlockSpec((tm, D), lambda i: (i, 0))],
        out_specs=pl.BlockSpec((SUBLANES, LANES), lambda i: (0, 0)),
        compiler_params=_params(("arbitrary",)),
        name="loss_fwd",
    )(y, t)
    return out[0, 0]


def _loss_bwd_call(y, t, g):
    L, D = y.shape
    tm = _rms_rows(L, D)

    def body(y_ref, t_ref, g_ref, o_ref):
        o_ref[...] = (y_ref[...] - t_ref[...]) * (g_ref[...] * (1.0 / D))

    return pl.pallas_call(
        body,
        out_shape=jax.ShapeDtypeStruct((L, D), F32),
        grid=(L // tm,),
        in_specs=[pl.BlockSpec((tm, D), lambda i: (i, 0)), pl.BlockSpec((tm, D), lambda i: (i, 0)),
                  pl.BlockSpec((1, 1), lambda i: (0, 0))],
        out_specs=pl.BlockSpec((tm, D), lambda i: (i, 0)),
        compiler_params=_params(("parallel",)),
        name="loss_bwd",
    )(y, t, g.reshape(1, 1))


@jax.custom_vjp
def loss_head(y, t):
    return _loss_fwd_call(y, t)


def _loss_head_fwd(y, t):
    return _loss_fwd_call(y, t), (y, t)


def _loss_head_bwd(res, g):
    y, t = res
    return _loss_bwd_call(y, t, g), jnp.zeros_like(t)


loss_head.defvjp(_loss_head_fwd, _loss_head_bwd)


def _rows_for(C, nbuf):
    return max(16, ((24 << 20) // (nbuf * 4 * C)) // 16 * 16)


def _pair_add_call(a, b, name):
    R, C = a.shape
    tm = _tile(R, _rows_for(C, 6), 16)

    def body(a_ref, b_ref, o_ref):
        o_ref[...] = (a_ref[...].astype(F32) + b_ref[...].astype(F32)).astype(o_ref.dtype)

    spec = pl.BlockSpec((tm, C), lambda i: (i, 0))
    return pl.pallas_call(
        body, out_shape=jax.ShapeDtypeStruct(a.shape, a.dtype), grid=(R // tm,),
        in_specs=[spec, spec], out_specs=spec, compiler_params=_params(("parallel",)), name=name,
    )(a, b)


def _adamw_call(parts, w, m, v, name):
    P, R, C = parts.shape
    tm = _tile(R, _rows_for(C, 2 * (P + 7)), 16)

    def body(p_ref, w_ref, m_ref, v_ref, g_ref, d_ref, mo_ref, vo_ref):
        g = p_ref[0].astype(F32)
        for s in range(1, P):
            g = g + p_ref[s].astype(F32)
        m2 = ADAM_B1 * m_ref[...] + (1.0 - ADAM_B1) * g
        v2 = ADAM_B2 * v_ref[...] + (1.0 - ADAM_B2) * jnp.square(g)
        m_hat = m2 / (1.0 - ADAM_B1 ** ADAM_STEP)
        v_hat = v2 / (1.0 - ADAM_B2 ** ADAM_STEP)
        g_ref[...] = g
        d_ref[...] = -ADAM_LR * (m_hat / (jnp.sqrt(v_hat) + ADAM_EPS) + ADAM_WD * w_ref[...])
        mo_ref[...] = m2
        vo_ref[...] = v2

    spec = pl.BlockSpec((tm, C), lambda i: (i, 0))
    out = jax.ShapeDtypeStruct((R, C), F32)
    return pl.pallas_call(
        body, out_shape=(out, out, out, out), grid=(R // tm,),
        in_specs=[pl.BlockSpec((P, tm, C), lambda i: (0, i, 0)), spec, spec, spec],
        out_specs=(spec, spec, spec, spec), compiler_params=_params(("parallel",)), name=name,
    )(parts, w, m, v)


_MESH = pl.DeviceIdType.MESH
FLIPS_CHIPS = ((1, 0, 0), (0, 1, 0), (1, 1, 0))
FLIPS_ALL = ((0, 0, 1), (1, 0, 0), (0, 1, 0), (1, 1, 0), (1, 0, 1), (0, 1, 1), (1, 1, 1))
FLIPS_SIBLING = ((0, 0, 1),)


def _exchange(arrays, flips, mode, name):
    n = len(arrays)
    nf = len(flips)
    if flips == FLIPS_SIBLING:
        n_slots, slot = 2, (lambda x, y, c: c)
    elif any(f[2] for f in flips):
        n_slots, slot = 8, (lambda x, y, c: 4 * x + 2 * y + c)
    else:
        n_slots, slot = 4, (lambda x, y, c: 2 * x + y)

    def body(*refs):
        ins, outs = refs[:n], refs[n:2 * n]
        send_sems, recv_sems = refs[2 * n:]
        x, y, c = lax.axis_index("x"), lax.axis_index("y"), lax.axis_index("c")
        me = slot(x, y, c)
        peers = [(x + f[0] - 2 * x * f[0], y + f[1] - 2 * y * f[1], c + f[2] - 2 * c * f[2]) for f in flips]

        def copy(i, k, sending):
            px, py, pc = peers[k]
            there = slot(px, py, pc)
            if mode == "swap":
                src, dst = ins[i], outs[i]
            elif mode == "swap_half":
                h = arrays[i].shape[1] // 2
                src, dst = ins[i].at[:, pl.ds(pc * h, h)], outs[i]
            elif mode == "gather":
                src, dst = ins[i], outs[i].at[me if sending else there]
            else:
                src, dst = ins[i].at[there if sending else me], outs[i].at[me if sending else there]
            return pltpu.make_async_remote_copy(src_ref=src, dst_ref=dst, send_sem=send_sems.at[i, k],
                                                recv_sem=recv_sems.at[i, k], device_id=(px, py, pc),
                                                device_id_type=_MESH)

        sends = [copy(i, k, True) for i in range(n) for k in range(nf)]
        for cp in sends:
            cp.start()
        for i in range(n):
            for k in range(nf):
                copy(i, k, False).wait_recv()
        for cp in sends:
            cp.wait_send()

    def with_own_slot(outs):
        if mode not in ("gather", "scatter"):
            return outs
        me = slot(lax.axis_index("x"), lax.axis_index("y"), lax.axis_index("c"))
        own = [a if mode == "gather" else lax.dynamic_index_in_dim(a, me, 0, keepdims=False) for a in arrays]
        return [lax.dynamic_update_index_in_dim(o, a, me, 0) for o, a in zip(outs, own)]

    if mode == "gather":
        out_shape = [jax.ShapeDtypeStruct((n_slots,) + a.shape, a.dtype) for a in arrays]
    elif mode == "swap_half":
        out_shape = [jax.ShapeDtypeStruct((a.shape[0], a.shape[1] // 2) + a.shape[2:], a.dtype) for a in arrays]
    else:
        out_shape = [jax.ShapeDtypeStruct(a.shape, a.dtype) for a in arrays]
    any_spec = pl.BlockSpec(memory_space=pl.ANY)
    return with_own_slot(pl.pallas_call(
        body,
        out_shape=out_shape,
        in_specs=[any_spec] * n,
        out_specs=[any_spec] * n,
        scratch_shapes=[pltpu.SemaphoreType.DMA((n, nf)), pltpu.SemaphoreType.DMA((n, nf))],
        compiler_params=pltpu.CompilerParams(has_side_effects=True),
        name=name,
    )(*arrays))


def _two_level_all_gather(arrays, name):
    n = len(arrays)
    nf = len(FLIPS_CHIPS)

    def body(*refs):
        ins, outs = refs[:n], refs[n:2 * n]
        send_sems, recv_sems, pass_send_sems, pass_recv_sems = refs[2 * n:]
        x, y, c = lax.axis_index("x"), lax.axis_index("y"), lax.axis_index("c")
        me = 2 * x + y
        chips = [(x + f[0] - 2 * x * f[0], y + f[1] - 2 * y * f[1]) for f in FLIPS_CHIPS]

        def rows(i, core):
            h = arrays[i].shape[0] // 2
            return pl.ds(core * h, h)

        def fetch(i, k, sending):
            px, py = chips[k]
            dst = outs[i].at[me if sending else 2 * px + py, rows(i, c)]
            return pltpu.make_async_remote_copy(src_ref=ins[i].at[rows(i, c)], dst_ref=dst,
                                                send_sem=send_sems.at[i, k], recv_sem=recv_sems.at[i, k],
                                                device_id=(px, py, c), device_id_type=_MESH)

        def hand_on(i, k, sending):
            px, py = chips[k]
            part = outs[i].at[2 * px + py, rows(i, c if sending else 1 - c)]
            return pltpu.make_async_remote_copy(src_ref=part, dst_ref=part, send_sem=pass_send_sems.at[i, k],
                                                recv_sem=pass_recv_sems.at[i, k], device_id=(x, y, 1 - c),
                                                device_id_type=_MESH)

        pairs = [(i, k) for i in range(n) for k in range(nf)]
        for i, k in pairs:
            fetch(i, k, True).start()
        for i, k in pairs:
            fetch(i, k, False).wait_recv()
            hand_on(i, k, True).start()
        for i, k in pairs:
            hand_on(i, k, False).wait_recv()
        for i, k in pairs:
            fetch(i, k, True).wait_send()
            hand_on(i, k, True).wait_send()

    any_spec = pl.BlockSpec(memory_space=pl.ANY)
    sems = pltpu.SemaphoreType.DMA((n, nf))
    gathered = pl.pallas_call(
        body,
        out_shape=[jax.ShapeDtypeStruct((4,) + a.shape, a.dtype) for a in arrays],
        in_specs=[any_spec] * n,
        out_specs=[any_spec] * n,
        scratch_shapes=[sems, sems, sems, sems],
        compiler_params=pltpu.CompilerParams(has_side_effects=True),
        name=name,
    )(*arrays)
    me = 2 * lax.axis_index("x") + lax.axis_index("y")
    return [lax.dynamic_update_index_in_dim(g, a, me, 0) for g, a in zip(gathered, arrays)]


def _sum_parts_call(parts, name):
    P, R, C = parts.shape
    tm = _tile(R, _rows_for(C, 2 * (P + 2)), 16)

    def body(p_ref, o_ref):
        g = p_ref[0].astype(F32)
        for s in range(1, P):
            g = g + p_ref[s].astype(F32)
        o_ref[...] = g

    return pl.pallas_call(
        body, out_shape=jax.ShapeDtypeStruct((R, C), F32), grid=(R // tm,),
        in_specs=[pl.BlockSpec((P, tm, C), lambda i: (0, i, 0))], out_specs=pl.BlockSpec((tm, C), lambda i: (i, 0)),
        compiler_params=_params(("parallel",)), name=name,
    )(parts)


WEIGHT_NAMES = ('w_in', 's5_lam_re', 's5_lam_im', 's5_log_step', 's5_b_re', 's5_b_im', 's5_c_re', 's5_c_im', 's5_d',
                's5_w_glu', 's5_b_glu', 'gdn_conv', 'gdn_a_log', 'gdn_dt_bias', 'gdn_o_gain', 'swa_sink', 't5_bias',
                'mla_q_gain', 'mla_kv_gain', 'mla_w_uq', 'mla_w_ukv', 'w_branch', 'w_out', 'mix_pre_gain',
                'mix_post_gain', 'mlp_pre_gain', 'mlp_post_gain', 'w_mlp_in', 'w_mlp_out')
SHARDED = {'w_in': (2, BF16), 's5_w_glu': (1, BF16), 'gdn_conv': (2, F32), 'mla_w_uq': (2, BF16),
           'mla_w_ukv': (2, BF16), 'w_branch': (3, BF16), 'w_out': (1, BF16), 'w_mlp_in': (2, BF16),
           'w_mlp_out': (1, BF16)}
REPLICATED = tuple(n for n in WEIGHT_NAMES if n not in SHARDED)
PACK_COLS = 1024

_IN_A = (('s5_u', 512), ('gdn_qkv', 1536), ('gdn_z', 512), ('gdn_beta', 8), ('gdn_decay', 8))
_IN_B = (('swa_q', 512), ('swa_kv', 256), ('mla_cq', 384), ('mla_ckv', 512), ('mla_kr', 64))
_IN_A_W = sum(w for _, w in _IN_A)
_IN_B_W = sum(w for _, w in _IN_B)
_IN_A_PAD = -_IN_A_W % LANES
_IN_B_PAD = -(_IN_A_W + _IN_A_PAD + _IN_B_W) % 512


def _assemble(g, axis):
    t = jnp.moveaxis(g, 0, axis)
    shape = t.shape[:axis] + (t.shape[axis] * t.shape[axis + 1],) + t.shape[axis + 2:]
    return t.reshape(shape)


def _s5_mixer(u, lam_re, lam_im, log_step, b_re, b_im, c_re, c_im, d_skip, w_glu, b_glu):
    nd, G, P = lam_re.shape
    Hg = b_re.shape[-1]
    lam_re = jnp.minimum(lam_re, -1e-4)
    dt = jnp.exp(log_step)[..., None]
    mag = jnp.exp(lam_re * dt)
    abar_r = mag * jnp.cos(lam_im * dt)
    abar_i = mag * jnp.sin(lam_im * dt)
    den = lam_re * lam_re + lam_im * lam_im
    xr = abar_r - 1.0
    xi = abar_i
    coef_r = (xr * lam_re + xi * lam_im) / den
    coef_i = (xi * lam_re - xr * lam_im) / den
    bbar_r = coef_r[..., None] * b_re - coef_i[..., None] * b_im
    bbar_i = coef_r[..., None] * b_im + coef_i[..., None] * b_re
    eye = jnp.eye(G, dtype=F32)

    def dense_b(bb):
        return jnp.einsum('dgph,gk->ghdkp', bb, eye).reshape(G * Hg, nd, G * P)

    def dense_c(cc):
        return jnp.einsum('dghp,gk->khdgp', cc, eye).reshape(G * Hg, nd, G * P)

    b_cat = _to_scan_cols(jnp.stack([dense_b(bbar_r), dense_b(bbar_i)], axis=2), nd)
    c_cat = _to_scan_cols(jnp.stack([dense_c(c_re), -dense_c(c_im)], axis=2), nd).T
    a_row = _to_scan_cols(jnp.stack([abar_r.reshape(nd, G * P), abar_i.reshape(nd, G * P)], axis=1), nd)[None]
    s = s5_scan(linear(u, b_cat, 's5_in'), a_row, nd)
    y = linear(s, c_cat, 's5_out') + d_skip * u
    y = jax.nn.gelu(y)
    return y * jax.nn.sigmoid(linear(y, w_glu, 's5_glu') + b_glu)


def _gdn_mixer(qkv, z, beta_logits, decay_logits, conv_w, a_log, dt_bias, o_gain):
    L = qkv.shape[0]
    Dh = GDN_HEAD_DIM
    H = z.shape[1] // Dh
    C = GDN_CHUNK
    NC = L // C
    xp = jnp.pad(qkv, ((GDN_CONV // 2, GDN_CONV - 1 - GDN_CONV // 2), (0, 0)))
    conv = xp[0:L] * conv_w[0]
    for j in range(1, GDN_CONV):
        conv = conv + xp[j:j + L] * conv_w[j]
    q, k, v = jnp.split(jax.nn.silu(conv), 3, axis=-1)

    def l2n(t):
        return t * lax.rsqrt(jnp.sum(t * t, axis=-1, keepdims=True) + 1e-6)

    q = (l2n(q.reshape(L, H, Dh)) * (Dh ** -0.5)).reshape(L, H * Dh)
    k = l2n(k.reshape(L, H, Dh)).reshape(L, H * Dh)
    beta = jax.nn.sigmoid(beta_logits).reshape(L, 2, H)
    g = -jnp.exp(a_log) * jax.nn.softplus(decay_logits.reshape(L, 2, H) + dt_bias)

    def per_chunk(t, d):
        return t[:, d].T.reshape(H, NC, C)

    g_with = jnp.cumsum(per_chunk(g, 0), axis=-1)
    g_against = jnp.cumsum(per_chunk(g, 1)[..., ::-1], axis=-1)[..., ::-1]
    gs = jnp.concatenate([g_with, g_against], axis=0)
    totals = jnp.concatenate([g_with[..., -1], g_against[..., 0]], axis=0)
    betas = jnp.concatenate([per_chunk(beta, 0), per_chunk(beta, 1)], axis=0)
    o_with, o_against = gated_delta_rule(q, k, v, gs[..., None], gs[:, :, None, :], totals[..., None, None],
                                         betas[..., None])
    o = o_with + o_against
    o = rms_norm(o.reshape(L * H, Dh), o_gain, 'gdn_onorm').reshape(L, H, Dh)
    o = o * jax.nn.silu(z.reshape(L, H, Dh))
    return o.reshape(L, H * Dh)


def _t5_bucket(rel):
    nb = T5_BUCKETS // 2
    max_exact = nb // 2
    ret = jnp.where(rel > 0, nb, 0)
    n = jnp.abs(rel)
    nf = jnp.maximum(n, 1).astype(F32)
    large = max_exact + (jnp.log(nf / max_exact) / math.log(T5_MAX_DISTANCE / max_exact)
                         * (nb - max_exact)).astype(jnp.int32)
    large = jnp.minimum(large, nb - 1)
    return ret + jnp.where(n < max_exact, n, large)


def _swa_mixer(q, kv, sink, t5_bias):
    L = q.shape[0]
    B = SWA_BLOCK
    HQ = q.shape[1] // SWA_HEAD_DIM
    HKV = HQ // SWA_KV_GROUP
    qh = q.reshape(L, HQ, SWA_HEAD_DIM).transpose(1, 0, 2)
    k, v = jnp.split(kv, 2, axis=-1)

    def heads_padded(t):
        return jnp.pad(t.reshape(L, HKV, SWA_HEAD_DIM).transpose(1, 0, 2), ((0, 0), (B, B), (0, 0)))

    qi = jnp.arange(B)[:, None]
    sj = jnp.arange(3 * B)[None, :]
    one_hot = (_t5_bucket(sj - B - qi)[..., None] == jnp.arange(T5_BUCKETS)).astype(F32)
    bias = jnp.einsum('qsb,bh->hqs', one_hot, t5_bias, precision=_HIGHEST)
    sink_rows = jnp.broadcast_to(sink[:, None, None], (HQ, B, 1))
    o = window_attention(qh, heads_padded(k), heads_padded(v), bias, sink_rows)
    return o.transpose(1, 0, 2).reshape(L, HQ * SWA_HEAD_DIM)


def _apply_rope(x, cos, sin):
    x1, x2 = jnp.split(x, 2, axis=-1)
    return jnp.concatenate([x1 * cos - x2 * sin, x2 * cos + x1 * sin], axis=-1)


def _mla_mixer(c_q, c_kv, k_rope, q_gain, kv_gain, w_uq, w_ukv):
    L = c_q.shape[0]
    H = w_uq.shape[1] // (MLA_NOPE + MLA_ROPE)
    q = linear(rms_norm(c_q, q_gain, 'mla_qnorm'), w_uq, 'mla_uq').reshape(L, H, MLA_NOPE + MLA_ROPE)
    kv = linear(rms_norm(c_kv, kv_gain, 'mla_kvnorm'), w_ukv, 'mla_ukv').reshape(L, H, MLA_NOPE + MLA_V)
    q_nope, q_pe = q[..., :MLA_NOPE], q[..., MLA_NOPE:]
    k_nope, v = kv[..., :MLA_NOPE], kv[..., MLA_NOPE:]
    pos = jnp.arange(L, dtype=F32)
    inv_freq = ROPE_THETA ** (-jnp.arange(0, MLA_ROPE, 2, dtype=F32) / MLA_ROPE)
    ang = pos[:, None] * inv_freq[None, :]
    cos, sin = jnp.cos(ang)[:, None, :], jnp.sin(ang)[:, None, :]
    q_pe = _apply_rope(q_pe, cos, sin)
    k_pe = _apply_rope(k_rope[:, None, :], cos, sin)
    qf = jnp.concatenate([q_nope, q_pe], axis=-1).transpose(1, 0, 2)
    kf = jnp.concatenate([k_nope, jnp.broadcast_to(k_pe, (L, H, MLA_ROPE))], axis=-1).transpose(1, 0, 2)
    o = full_attention(qf, kf, v.transpose(1, 0, 2), (MLA_NOPE + MLA_ROPE) ** -0.5)
    return o.transpose(1, 0, 2).reshape(L, H * MLA_V)


_IN_LAYOUT = _IN_A + ((None, _IN_A_PAD),) + _IN_B + ((None, _IN_B_PAD),)


@jax.custom_vjp
def split_proj(proj):
    out, start = [], 0
    for name, width in _IN_LAYOUT:
        if name is not None:
            out.append(proj[:, start:start + width])
        start += width
    return tuple(out)


def _split_proj_bwd(_, cts):
    it = iter(cts)
    rows = cts[0].shape[0]
    parts = [jnp.zeros((rows, width), F32) if name is None else next(it) for name, width in _IN_LAYOUT]
    return (jnp.concatenate(parts, axis=1),)


split_proj.defvjp(lambda proj: (split_proj(proj), None), _split_proj_bwd)


def _local_loss(weights, x, target):
    p = {n: (_assemble(weights[n], SHARDED[n][0]) if n in SHARDED else weights[n]) for n in WEIGHT_NAMES}
    L, D = x.shape
    depth = p['w_in'].shape[0]
    for l in range(depth):
        w_in = p['w_in'][l]
        zeros = lambda n: jnp.zeros((D, n), w_in.dtype)
        w_r = jnp.concatenate([w_in[:, :_IN_A_W], zeros(_IN_A_PAD), w_in[:, _IN_A_W:_IN_A_W + _IN_B_W],
                               zeros(_IN_B_PAD)], axis=1)
        w_g = w_in[:, _IN_A_W + _IN_B_W:]
        h = rms_norm(x, p['mix_pre_gain'][l], 'mix_pre')
        proj = linear(h, w_r, 'in_proj')
        gate_logits = linear(h, w_g, 'in_gate')
        seg = dict(zip([name for name, _ in _IN_A + _IN_B], split_proj(proj)))
        y_a = _s5_mixer(seg['s5_u'], p['s5_lam_re'][l], p['s5_lam_im'][l], p['s5_log_step'][l], p['s5_b_re'][l],
                        p['s5_b_im'][l], p['s5_c_re'][l], p['s5_c_im'][l], p['s5_d'][l], p['s5_w_glu'][l],
                        p['s5_b_glu'][l])
        y_b = _gdn_mixer(seg['gdn_qkv'], seg['gdn_z'], seg['gdn_beta'], seg['gdn_decay'], p['gdn_conv'][l],
                         p['gdn_a_log'][l], p['gdn_dt_bias'][l], p['gdn_o_gain'][l])
        y_c = _swa_mixer(seg['swa_q'], seg['swa_kv'], p['swa_sink'][l], p['t5_bias'])
        y_d = _mla_mixer(seg['mla_cq'], seg['mla_ckv'], seg['mla_kr'], p['mla_q_gain'][l], p['mla_kv_gain'][l],
                         p['mla_w_uq'][l], p['mla_w_ukv'][l])
        branches = tuple(linear(y, p['w_branch'][l, b], 'branch') for b, y in enumerate((y_a, y_b, y_c, y_d)))
        merged = gate_merge(gate_logits, branches)
        x = x + rms_norm(linear(merged, p['w_out'][l], 'mix_out'), p['mix_post_gain'][l], 'mix_post')
        h = rms_norm(x, p['mlp_pre_gain'][l], 'mlp_pre')
        f = relu2_linear(linear(h, p['w_mlp_in'][l], 'mlp_in'), p['w_mlp_out'][l])
        x = x + rms_norm(f, p['mlp_post_gain'][l], 'mlp_post')
    return loss_head(x, target)


def _two_d(t, lead):
    return t.reshape(t.shape[:lead] + (-1, t.shape[-1]))


def _pack(arrays):
    flat = jnp.concatenate([a.reshape(-1) for a in arrays])
    pad = -flat.shape[0] % (16 * PACK_COLS)
    return jnp.pad(flat, (0, pad)).reshape(-1, PACK_COLS)


def _unpack(packed, like):
    flat = packed.reshape(-1)
    out, pos = [], 0
    for a in like:
        out.append(flat[pos:pos + a.size].reshape(a.shape))
        pos += a.size
    return out


def kernel(x, w_in, s5_lam_re, s5_lam_im, s5_log_step, s5_b_re, s5_b_im, s5_c_re, s5_c_im, s5_d, s5_w_glu, s5_b_glu, gdn_conv, gdn_a_log, gdn_dt_bias, gdn_o_gain, swa_sink, t5_bias, mla_q_gain, mla_kv_gain, mla_w_uq, mla_w_ukv, w_branch, w_out, mix_pre_gain, mix_post_gain, mlp_pre_gain, mlp_post_gain, w_mlp_in, w_mlp_out, loss_target, m_w_in, m_s5_lam_re, m_s5_lam_im, m_s5_log_step, m_s5_b_re, m_s5_b_im, m_s5_c_re, m_s5_c_im, m_s5_d, m_s5_w_glu, m_s5_b_glu, m_gdn_conv, m_gdn_a_log, m_gdn_dt_bias, m_gdn_o_gain, m_swa_sink, m_t5_bias, m_mla_q_gain, m_mla_kv_gain, m_mla_w_uq, m_mla_w_ukv, m_w_branch, m_w_out, m_mix_pre_gain, m_mix_post_gain, m_mlp_pre_gain, m_mlp_post_gain, m_w_mlp_in, m_w_mlp_out, v_w_in, v_s5_lam_re, v_s5_lam_im, v_s5_log_step, v_s5_b_re, v_s5_b_im, v_s5_c_re, v_s5_c_im, v_s5_d, v_s5_w_glu, v_s5_b_glu, v_gdn_conv, v_gdn_a_log, v_gdn_dt_bias, v_gdn_o_gain, v_swa_sink, v_t5_bias, v_mla_q_gain, v_mla_kv_gain, v_mla_w_uq, v_mla_w_ukv, v_w_branch, v_w_out, v_mix_pre_gain, v_mix_post_gain, v_mlp_pre_gain, v_mlp_post_gain, v_w_mlp_in, v_w_mlp_out):
    w = dict(zip(WEIGHT_NAMES, (w_in, s5_lam_re, s5_lam_im, s5_log_step, s5_b_re, s5_b_im, s5_c_re, s5_c_im, s5_d, s5_w_glu, s5_b_glu, gdn_conv, gdn_a_log, gdn_dt_bias, gdn_o_gain, swa_sink, t5_bias, mla_q_gain, mla_kv_gain, mla_w_uq, mla_w_ukv, w_branch, w_out, mix_pre_gain, mix_post_gain, mlp_pre_gain, mlp_post_gain, w_mlp_in, w_mlp_out)))
    m = dict(zip(WEIGHT_NAMES, (m_w_in, m_s5_lam_re, m_s5_lam_im, m_s5_log_step, m_s5_b_re, m_s5_b_im, m_s5_c_re, m_s5_c_im, m_s5_d, m_s5_w_glu, m_s5_b_glu, m_gdn_conv, m_gdn_a_log, m_gdn_dt_bias, m_gdn_o_gain, m_swa_sink, m_t5_bias, m_mla_q_gain, m_mla_kv_gain, m_mla_w_uq, m_mla_w_ukv, m_w_branch, m_w_out, m_mix_pre_gain, m_mix_post_gain, m_mlp_pre_gain, m_mlp_post_gain, m_w_mlp_in, m_w_mlp_out)))
    v = dict(zip(WEIGHT_NAMES, (v_w_in, v_s5_lam_re, v_s5_lam_im, v_s5_log_step, v_s5_b_re, v_s5_b_im, v_s5_c_re, v_s5_c_im, v_s5_d, v_s5_w_glu, v_s5_b_glu, v_gdn_conv, v_gdn_a_log, v_gdn_dt_bias, v_gdn_o_gain, v_swa_sink, v_t5_bias, v_mla_q_gain, v_mla_kv_gain, v_mla_w_uq, v_mla_w_ukv, v_w_branch, v_w_out, v_mix_pre_gain, v_mix_post_gain, v_mlp_pre_gain, v_mlp_post_gain, v_w_mlp_in, v_w_mlp_out)))
    sharded = tuple(SHARDED)

    gathered = _two_level_all_gather([w[n].astype(SHARDED[n][1]) for n in sharded], 'weights_all_gather')
    weights = dict(zip(sharded, gathered))
    weights.update({n: w[n] for n in REPLICATED})

    loss, (grads, grad_x) = jax.value_and_grad(_local_loss, argnums=(0, 1))(weights, x[0], loss_target[0])
    loss = lax.psum(loss, ('x', 'y', 'c'))

    core = lax.axis_index('c')
    from_sibling = _exchange([grads[n] for n in sharded], FLIPS_SIBLING, 'swap_half', 'grads_core_swap')
    chip_sums = []
    for n, r in zip(sharded, from_sibling):
        mine = lax.dynamic_slice_in_dim(grads[n], core * r.shape[1], r.shape[1], axis=1)
        chip_sums.append(_pair_add_call(_two_d(mine, 0), _two_d(r, 0), 'grads_core_add').reshape(r.shape))
    per_chip = _exchange(chip_sums, FLIPS_CHIPS, 'scatter', 'grads_chip_scatter')
    halves = [_sum_parts_call(_two_d(parts, 1), 'grads_chip_sum').reshape(parts.shape[1:]) for parts in per_chip]
    both_halves = _exchange(halves, FLIPS_SIBLING, 'gather', 'grads_core_gather')
    out = {}
    for n, g in zip(sharded, both_halves):
        res = _adamw_call(_two_d(g, 0)[None], _two_d(w[n], 0), _two_d(m[n], 0), _two_d(v[n], 0), 'adamw_sharded')
        out[n] = tuple(r.reshape(w[n].shape) for r in res)

    small = [grads[n] for n in REPLICATED]
    all_parts = _exchange([_pack(small)], FLIPS_ALL, 'gather', 'grads_all_gather')[0]
    res = _adamw_call(all_parts, _pack([w[n] for n in REPLICATED]), _pack([m[n] for n in REPLICATED]),
                      _pack([v[n] for n in REPLICATED]), 'adamw_replicated')
    unpacked = [_unpack(r, small) for r in res]
    for i, n in enumerate(REPLICATED):
        out[n] = tuple(u[i] for u in unpacked)

    return (loss, grad_x[None]) + tuple(out[n][k] for k in range(4) for n in WEIGHT_NAMES)
```

```python
import functools
import math

import numpy as np
import jax
import jax.numpy as jnp
from jax import lax
from jax.experimental import pallas as pl
from jax.experimental.pallas import tpu as pltpu

F32 = jnp.float32
BF16 = jnp.bfloat16

VMEM_LIMIT_BYTES = 56 * 1024 * 1024
SUBLANES = 8
LANES = 128

NORM_EPS = 1e-6
DEPTH = 4
N_BRANCHES = 4
S5_GROUP = 16
S5_STATE = 64
GDN_HEAD_DIM = 128
GDN_CONV = 4
GDN_CHUNK = 64
SWA_HEAD_DIM = 64
SWA_KV_GROUP = 4
WINDOW = 128
SWA_BLOCK = 128
T5_BUCKETS = 32
T5_MAX_DISTANCE = 128
MLA_NOPE = 128
MLA_ROPE = 64
MLA_V = 128
ROPE_THETA = 10000.0

ADAM_LR = 0.001
ADAM_B1 = 0.9
ADAM_B2 = 0.999
ADAM_EPS = 1e-08
ADAM_WD = 0.01
ADAM_STEP = 10

_HIGHEST = lax.Precision.HIGHEST


def _params(sem, vmem=VMEM_LIMIT_BYTES):
    return pltpu.CompilerParams(dimension_semantics=sem, vmem_limit_bytes=vmem)


def _tile(dim, target, align):
    if dim <= target:
        return dim
    t = (target // align) * align
    while t >= align:
        if dim % t == 0:
            return t
        t -= align
    return dim


_DOT_DIMS = {"nn": ((1,), (0,)), "nt": ((1,), (1,)), "tn": ((0,), (0,))}


MM_OPERAND_TILE_BYTES = 12 * 1024 * 1024


def _mm(a, b, form="nn", out_dtype=F32, tm=1024, tn=1024, tk=None, relu2_a=False, relu2_grad_of=None, name="mm"):
    if form == "nn":
        (M, K), (K2, N) = a.shape, b.shape
    elif form == "nt":
        (M, K), (N, K2) = a.shape, b.shape
    else:
        (K, M), (K2, N) = a.shape, b.shape
    assert K == K2, (a.shape, b.shape, form)
    tm = _tile(M, tm, LANES if form == "tn" else 16)
    tn = _tile(N, tn, LANES)
    if tk is None:
        tk = MM_OPERAND_TILE_BYTES // (tm * a.dtype.itemsize + tn * b.dtype.itemsize)
    tk = _tile(K, max(LANES, tk // LANES * LANES), LANES)
    nk = K // tk
    dims = (_DOT_DIMS[form], ((), ()))
    n_in = 2 if relu2_grad_of is None else 3

    def dot(a_ref, b_ref):
        av = a_ref[...]
        if relu2_a:
            av = jnp.square(jnp.maximum(av, 0.0))
        return lax.dot_general(av.astype(BF16), b_ref[...].astype(BF16), dims, preferred_element_type=F32)

    def finish(acc, refs):
        if relu2_grad_of is not None:
            acc = acc * (2.0 * jnp.maximum(refs[2][...], 0.0))
        refs[n_in][...] = acc.astype(refs[n_in].dtype)

    def body_single(*refs):
        finish(dot(refs[0], refs[1]), refs)

    def body_acc(*refs):
        acc_ref = refs[n_in + 1]
        k = pl.program_id(2)

        @pl.when(k == 0)
        def _():
            acc_ref[...] = jnp.zeros_like(acc_ref)

        acc_ref[...] += dot(refs[0], refs[1])

        @pl.when(k == nk - 1)
        def _():
            finish(acc_ref[...], refs)

    body = body_single if nk == 1 else body_acc
    extra = [] if relu2_grad_of is None else [relu2_grad_of]
    extra_specs = [pl.BlockSpec((tm, tn), lambda i, j, k: (i, j))] * len(extra)

    if form == "nn":
        a_spec = pl.BlockSpec((tm, tk), lambda i, j, k: (i, k))
        b_spec = pl.BlockSpec((tk, tn), lambda i, j, k: (k, j))
    elif form == "nt":
        a_spec = pl.BlockSpec((tm, tk), lambda i, j, k: (i, k))
        b_spec = pl.BlockSpec((tn, tk), lambda i, j, k: (j, k))
    else:
        a_spec = pl.BlockSpec((tk, tm), lambda i, j, k: (k, i))
        b_spec = pl.BlockSpec((tk, tn), lambda i, j, k: (k, j))
    return pl.pallas_call(
        body,
        out_shape=jax.ShapeDtypeStruct((M, N), out_dtype),
        grid=(M // tm, N // tn, nk),
        in_specs=[a_spec, b_spec] + extra_specs,
        out_specs=pl.BlockSpec((tm, tn), lambda i, j, k: (i, j)),
        scratch_shapes=[] if nk == 1 else [pltpu.VMEM((tm, tn), F32)],
        compiler_params=_params(("parallel", "parallel", "arbitrary")),
        name=name,
    )(a, b, *extra)


@jax.custom_vjp
def relu2_linear(x, w):
    return _mm(x, w, "nn", relu2_a=True, name="mlp_out_fwd")


def _relu2_linear_fwd(x, w):
    return _mm(x, w, "nn", relu2_a=True, name="mlp_out_fwd"), (x, w)


def _relu2_linear_bwd(res, dy):
    x, w = res
    dx = _mm(dy, w, "nt", relu2_grad_of=x, name="mlp_out_dx")
    dw = _mm(x, dy, "tn", out_dtype=w.dtype, relu2_a=True, name="mlp_out_dw")
    return dx, dw


relu2_linear.defvjp(_relu2_linear_fwd, _relu2_linear_bwd)


@functools.partial(jax.custom_vjp, nondiff_argnums=(2,))
def linear(x, w, name):
    return _mm(x, w, "nn", name=name + "_fwd")


def _linear_fwd(x, w, name):
    return _mm(x, w, "nn", name=name + "_fwd"), (x, w)


def _linear_bwd(name, res, dy):
    x, w = res
    dx = _mm(dy, w, "nt", out_dtype=x.dtype, name=name + "_dx")
    dw = _mm(x, dy, "tn", out_dtype=w.dtype, name=name + "_dw")
    return dx, dw


linear.defvjp(_linear_fwd, _linear_bwd)


def _rms_rows(L, D):
    return _tile(L, max(SUBLANES, (1 << 20) // D), SUBLANES)


def _rms_fwd_call(x, gain, name, residual=None):
    L, D = x.shape
    tm = _rms_rows(L, D)
    extra = [] if residual is None else [residual]

    def body(x_ref, g_ref, *rest):
        o_ref = rest[-1]
        xv = x_ref[...]
        r = lax.rsqrt(jnp.mean(xv * xv, axis=-1, keepdims=True) + NORM_EPS)
        y = xv * r * g_ref[...]
        o_ref[...] = y if residual is None else rest[0][...] + y

    row_spec = pl.BlockSpec((tm, D), lambda i: (i, 0))
    return pl.pallas_call(
        body,
        out_shape=jax.ShapeDtypeStruct((L, D), F32),
        grid=(L // tm,),
        in_specs=[row_spec, pl.BlockSpec((1, D), lambda i: (0, 0))] + [row_spec] * len(extra),
        out_specs=row_spec,
        compiler_params=_params(("parallel",)),
        name=name,
    )(x, gain.reshape(1, D), *extra)


def _rms_bwd_call(x, gain, dy, name):
    L, D = x.shape
    tm = _rms_rows(L, D)

    def body(x_ref, g_ref, dy_ref, dx_ref, dg_ref):
        xv = x_ref[...]
        dyv = dy_ref[...]
        r = lax.rsqrt(jnp.mean(xv * xv, axis=-1, keepdims=True) + NORM_EPS)
        xh = xv * r
        dyg = dyv * g_ref[...]
        dx_ref[...] = r * (dyg - xh * jnp.mean(dyg * xh, axis=-1, keepdims=True))

        @pl.when(pl.program_id(0) == 0)
        def _():
            dg_ref[...] = jnp.zeros_like(dg_ref)

        dg_ref[...] += jnp.sum(dyv * xh, axis=0, keepdims=True)

    return pl.pallas_call(
        body,
        out_shape=(jax.ShapeDtypeStruct((L, D), F32), jax.ShapeDtypeStruct((1, D), F32)),
        grid=(L // tm,),
        in_specs=[pl.BlockSpec((tm, D), lambda i: (i, 0)), pl.BlockSpec((1, D), lambda i: (0, 0)),
                  pl.BlockSpec((tm, D), lambda i: (i, 0))],
        out_specs=(pl.BlockSpec((tm, D), lambda i: (i, 0)), pl.BlockSpec((1, D), lambda i: (0, 0))),
        compiler_params=_params(("arbitrary",)),
        name=name,
    )(x, gain.reshape(1, D), dy)


@functools.partial(jax.custom_vjp, nondiff_argnums=(2,))
def rms_norm(x, gain, name):
    return _rms_fwd_call(x, gain, name + "_fwd")


def _rms_norm_fwd(x, gain, name):
    return _rms_fwd_call(x, gain, name + "_fwd"), (x, gain)


def _rms_norm_bwd(name, res, dy):
    x, gain = res
    dx, dg = _rms_bwd_call(x, gain, dy, name + "_bwd")
    return dx, dg.reshape(gain.shape)


rms_norm.defvjp(_rms_norm_fwd, _rms_norm_bwd)


@functools.partial(jax.custom_vjp, nondiff_argnums=(3,))
def add_rms_norm(skip, x, gain, name):
    return _rms_fwd_call(x, gain, name + "_fwd", residual=skip)


def _add_rms_norm_fwd(skip, x, gain, name):
    return _rms_fwd_call(x, gain, name + "_fwd", residual=skip), (x, gain)


def _add_rms_norm_bwd(name, res, dy):
    x, gain = res
    dx, dg = _rms_bwd_call(x, gain, dy, name + "_bwd")
    return dy, dx, dg.reshape(gain.shape)


add_rms_norm.defvjp(_add_rms_norm_fwd, _add_rms_norm_bwd)


SCAN_COLS = 512
SCAN_ROWS = 256
SCAN_UNROLL = 8


def _to_scan_cols(t, nd):
    P = t.shape[-1]
    pc = _tile(P, SCAN_COLS, LANES)
    lead = t.shape[:-3]
    t = t.reshape(lead + (nd, 2, P // pc, pc))
    t = jnp.swapaxes(t, -3, -2)
    return t.reshape(lead + (nd * 2 * P,))


def _scan_call(b, a, nd, reverse_dir0, name):
    L, NC = b.shape
    P = NC // (2 * nd)
    pc = _tile(P, SCAN_COLS, LANES)
    ncb = P // pc
    T = _tile(L, SCAN_ROWS, SUBLANES)
    nt = L // T
    rev0 = 1 if reverse_dir0 else 0
    G = SUBLANES

    def rev_of(d):
        return d + rev0 - 2 * d * rev0

    a3 = a.reshape(nd * ncb, 2, pc)
    cmul = lambda x, y: (x[0] * y[0] - x[1] * y[1], x[0] * y[1] + x[1] * y[0])
    pows = [(a3[:, 0], a3[:, 1])]
    for _ in range(G - 1):
        pows.append(cmul(pows[-1], pows[0]))
    cols = lambda t: jnp.stack(t, axis=1).reshape(nd, NC // nd)
    steps = jnp.stack([cols(pows[0]), cols(pows[1]), cols(pows[3])], axis=0).reshape(3, NC)
    with_time = jnp.stack([cols(p) for p in pows], axis=0)
    carry_pows = jnp.concatenate([with_time[::-1, d] if rev_of(d) else with_time[:, d] for d in range(nd)], axis=1)

    def body(b_ref, step_ref, pow_ref, o_ref, carry_ref):
        rv = rev_of(pl.program_id(0))

        @pl.when(pl.program_id(2) == 0)
        def _():
            carry_ref[...] = jnp.zeros_like(carry_ref)

        row = lax.broadcasted_iota(jnp.int32, (G, pc), 0)
        mults = [(step_ref[j:j + 1, :pc], step_ref[j:j + 1, pc:]) for j in range(3)]
        pr, pi = pow_ref[:, :pc], pow_ref[:, pc:]

        def run(backwards):
            def shifted(x, k):
                if backwards:
                    return jnp.where(row < G - k, pltpu.roll(x, G - k, 0), 0.0)
                return jnp.where(row >= k, pltpu.roll(x, k, 0), 0.0)

            def group(g, carry):
                cr, ci = carry
                r0 = pl.multiple_of((T // G - 1 - g if backwards else g) * G, G)
                x = b_ref[pl.ds(r0, G), :]
                xr, xi = x[:, :pc], x[:, pc:]
                for j, (mr, mi) in enumerate(mults):
                    sr, si = shifted(xr, 1 << j), shifted(xi, 1 << j)
                    xr, xi = xr + (mr * sr - mi * si), xi + (mr * si + mi * sr)
                o_ref[pl.ds(r0, G), :] = jnp.concatenate([xr + (pr * cr - pi * ci), xi + (pr * ci + pi * cr)], axis=1)
                last = o_ref[pl.ds(r0 + (0 if backwards else G - 1), 1), :]
                return last[:, :pc], last[:, pc:]

            cr, ci = lax.fori_loop(0, T // G, group, (carry_ref[:, :pc], carry_ref[:, pc:]), unroll=SCAN_UNROLL)
            carry_ref[...] = jnp.concatenate([cr, ci], axis=1)

        pl.when(rv == 0)(functools.partial(run, False))
        pl.when(rv == 1)(functools.partial(run, True))

    def t_idx(d, i):
        return i + rev_of(d) * (nt - 1 - 2 * i)

    spec = pl.BlockSpec((T, 2 * pc), lambda d, c, i: (t_idx(d, i), d * ncb + c))
    return pl.pallas_call(
        body,
        out_shape=jax.ShapeDtypeStruct(b.shape, F32),
        grid=(nd, ncb, nt),
        in_specs=[spec, pl.BlockSpec((3, 2 * pc), lambda d, c, i: (0, d * ncb + c)),
                  pl.BlockSpec((G, 2 * pc), lambda d, c, i: (0, d * ncb + c))],
        out_specs=spec,
        scratch_shapes=[pltpu.VMEM((1, 2 * pc), F32)],
        compiler_params=_params(("arbitrary", "arbitrary", "arbitrary")),
        name=name,
    )(b, steps, carry_pows)


def _scan_da_call(g, sp, nd, name):
    L, NC = g.shape
    P = NC // (2 * nd)
    pc = _tile(P, SCAN_COLS, LANES)
    T = _tile(L, SCAN_ROWS * 2, SUBLANES)

    def body(g_ref, s_ref, o_ref):
        @pl.when(pl.program_id(1) == 0)
        def _():
            o_ref[...] = jnp.zeros_like(o_ref)

        gr, gi = g_ref[:, :pc], g_ref[:, pc:]
        sr, si = s_ref[:, :pc], s_ref[:, pc:]
        dar = jnp.sum(gr * sr + gi * si, axis=0, keepdims=True)
        dai = jnp.sum(gi * sr - gr * si, axis=0, keepdims=True)
        o_ref[...] += jnp.concatenate([dar, dai], axis=1)

    spec = pl.BlockSpec((T, 2 * pc), lambda c, i: (i, c))
    return pl.pallas_call(
        body,
        out_shape=jax.ShapeDtypeStruct((1, NC), F32),
        grid=(NC // (2 * pc), L // T),
        in_specs=[spec, spec],
        out_specs=pl.BlockSpec((1, 2 * pc), lambda c, i: (0, c)),
        compiler_params=_params(("arbitrary", "arbitrary")),
        name=name,
    )(g, sp)


def _conj_cols(a, nd):
    P = a.shape[-1] // (2 * nd)
    pc = _tile(P, SCAN_COLS, LANES)
    sign = jnp.tile(jnp.concatenate([jnp.ones((pc,), F32), -jnp.ones((pc,), F32)]), a.shape[-1] // (2 * pc))
    return a * sign


@functools.partial(jax.custom_vjp, nondiff_argnums=(2,))
def s5_scan(b, a, nd):
    return _scan_call(b, a, nd, False, "s5_scan_fwd")


def _s5_scan_fwd(b, a, nd):
    s = _scan_call(b, a, nd, False, "s5_scan_fwd")
    return s, (s, a)


def _s5_scan_bwd(nd, res, ds):
    s, a = res
    g = _scan_call(ds, _conj_cols(a, nd), nd, True, "s5_scan_adj")
    L, NC = s.shape
    half = NC // nd
    zero = jnp.zeros((1, half), F32)
    prev = [jnp.concatenate([zero, s[:-1, :half]], axis=0)]
    if nd == 2:
        prev.append(jnp.concatenate([s[1:, half:], zero], axis=0))
    sp = jnp.concatenate(prev, axis=1)
    da = _scan_da_call(g, sp, nd, "s5_scan_da")
    return g, da


s5_scan.defvjp(_s5_scan_fwd, _s5_scan_bwd)


LOG2E = math.log2(math.e)
ATTN_TQ = 512
ATTN_TK = 4096
ATTN_BWD_TQ = 512
ATTN_BWD_TK = 4096
ATTN_SUB = 2


def _attn_fwd_call(q, k, v, scale):
    H, L, DQ = q.shape
    DV = v.shape[-1]
    tq = _tile(L, ATTN_TQ, LANES)
    tk = _tile(L, ATTN_TK, LANES)
    nk = L // tk

    def body(q_ref, k_ref, v_ref, o_ref, lse_ref, m_s, l_s, acc_s):
        j = pl.program_id(2)

        @pl.when(j == 0)
        def _():
            m_s[...] = jnp.full_like(m_s, -jnp.inf)
            l_s[...] = jnp.zeros_like(l_s)
            acc_s[...] = jnp.zeros_like(acc_s)

        s = lax.dot_general(q_ref[0].astype(BF16), k_ref[0].astype(BF16), (((1,), (1,)), ((), ())),
                            preferred_element_type=F32) * (scale * LOG2E)
        m_old = m_s[...]
        m_new = jnp.maximum(m_old, jnp.max(s, axis=1, keepdims=True))
        alpha = jnp.exp2(m_old - m_new)
        p = jnp.exp2(s - m_new)
        l_s[...] = alpha * l_s[...] + jnp.sum(p, axis=1, keepdims=True)
        acc_s[...] = alpha * acc_s[...] + jnp.dot(p.astype(BF16), v_ref[0].astype(BF16), preferred_element_type=F32)
        m_s[...] = m_new

        @pl.when(j == nk - 1)
        def _():
            o_ref[0] = acc_s[...] / l_s[...]
            lse_ref[0] = m_s[...] + jnp.log(l_s[...]) * LOG2E

    return pl.pallas_call(
        body,
        out_shape=(jax.ShapeDtypeStruct((H, L, DV), F32), jax.ShapeDtypeStruct((H, L, 1), F32)),
        grid=(H, L // tq, nk),
        in_specs=[pl.BlockSpec((1, tq, DQ), lambda h, i, j: (h, i, 0)),
                  pl.BlockSpec((1, tk, DQ), lambda h, i, j: (h, j, 0)),
                  pl.BlockSpec((1, tk, DV), lambda h, i, j: (h, j, 0))],
        out_specs=(pl.BlockSpec((1, tq, DV), lambda h, i, j: (h, i, 0)),
                   pl.BlockSpec((1, tq, 1), lambda h, i, j: (h, i, 0))),
        scratch_shapes=[pltpu.VMEM((tq, 1), F32), pltpu.VMEM((tq, 1), F32), pltpu.VMEM((tq, DV), F32)],
        compiler_params=_params(("parallel", "parallel", "arbitrary")),
        name="mla_attn_fwd",
    )(q, k, v)


def _attn_bwd_call(q, k, v, do, lse, delta, scale):
    H, L, DQ = q.shape
    DV = v.shape[-1]
    tq = _tile(L, ATTN_BWD_TQ, LANES)
    tk = _tile(L, ATTN_BWD_TK, LANES)
    nq = L // tq

    def body(q_ref, k_ref, v_ref, do_ref, lse_ref, dl_ref, dq_ref, dk_ref, dv_ref, dk_s, dv_s):
        j = pl.program_id(1)
        i = pl.program_id(2)

        @pl.when(jnp.logical_and(j == 0, i == 0))
        def _():
            dq_ref[...] = jnp.zeros_like(dq_ref)

        @pl.when(i == 0)
        def _():
            dk_s[...] = jnp.zeros_like(dk_s)
            dv_s[...] = jnp.zeros_like(dv_s)

        kb = k_ref[0].astype(BF16)
        vb = v_ref[0].astype(BF16)
        w = tq // ATTN_SUB
        R = range(ATTN_SUB)
        sub = [pl.ds(a * w, w) for a in R]
        nt, tn = (((1,), (1,)), ((), ())), (((0,), (0,)), ((), ()))
        qb = [q_ref[0, sub[a], :].astype(BF16) for a in R]
        dob = [do_ref[0, sub[a], :].astype(BF16) for a in R]
        s = [lax.dot_general(qb[a], kb, nt, preferred_element_type=F32) * (scale * LOG2E) for a in R]
        dp = [lax.dot_general(dob[a], vb, nt, preferred_element_type=F32) for a in R]
        p = [jnp.exp2(s[a] - lse_ref[0, sub[a], :]) for a in R]
        ds = [(p[a] * (dp[a] - dl_ref[0, sub[a], :]) * scale).astype(BF16) for a in R]
        dv = dv_s[...]
        dk = dk_s[...]
        for a in R:
            dv = dv + lax.dot_general(p[a].astype(BF16), dob[a], tn, preferred_element_type=F32)
            dk = dk + lax.dot_general(ds[a], qb[a], tn, preferred_element_type=F32)
        dv_s[...] = dv
        dk_s[...] = dk
        for a in R:
            rows = pl.ds(pl.multiple_of(i * tq + a * w, w), w)
            dq_ref[0, rows, :] += jnp.dot(ds[a], kb, preferred_element_type=F32)

        @pl.when(i == nq - 1)
        def _():
            dk_ref[0] = dk_s[...]
            dv_ref[0] = dv_s[...]

    return pl.pallas_call(
        body,
        out_shape=(jax.ShapeDtypeStruct((H, L, DQ), F32), jax.ShapeDtypeStruct((H, L, DQ), F32),
                   jax.ShapeDtypeStruct((H, L, DV), F32)),
        grid=(H, L // tk, nq),
        in_specs=[pl.BlockSpec((1, tq, DQ), lambda h, j, i: (h, i, 0)),
                  pl.BlockSpec((1, tk, DQ), lambda h, j, i: (h, j, 0)),
                  pl.BlockSpec((1, tk, DV), lambda h, j, i: (h, j, 0)),
                  pl.BlockSpec((1, tq, DV), lambda h, j, i: (h, i, 0)),
                  pl.BlockSpec((1, tq, 1), lambda h, j, i: (h, i, 0)),
                  pl.BlockSpec((1, tq, 1), lambda h, j, i: (h, i, 0))],
        out_specs=(pl.BlockSpec((1, L, DQ), lambda h, j, i: (h, 0, 0)),
                   pl.BlockSpec((1, tk, DQ), lambda h, j, i: (h, j, 0)),
                   pl.BlockSpec((1, tk, DV), lambda h, j, i: (h, j, 0))),
        scratch_shapes=[pltpu.VMEM((tk, DQ), F32), pltpu.VMEM((tk, DV), F32)],
        compiler_params=_params(("arbitrary", "arbitrary", "arbitrary")),
        name="mla_attn_bwd",
    )(q, k, v, do, lse, delta)


@functools.partial(jax.custom_vjp, nondiff_argnums=(3,))
def full_attention(q, k, v, scale):
    return _full_attention_fwd(q, k, v, scale)[0]


def _full_attention_fwd(q, k, v, scale):
    q, k, v = q.astype(BF16), k.astype(BF16), v.astype(BF16)
    o, lse = _attn_fwd_call(q, k, v, scale)
    return o, (q, k, v, o, lse)


def _full_attention_bwd(scale, res, do):
    q, k, v, o, lse = res
    delta = jnp.sum(do * o, axis=-1, keepdims=True)
    return _attn_bwd_call(q, k, v, do.astype(BF16), lse, delta, scale)


full_attention.defvjp(_full_attention_fwd, _full_attention_bwd)


_NN = ((1,), (0,))
_NT = ((1,), (1,))
_TN = ((0,), (0,))
_DOT_VJP = {_NN: (("g", "b", _NT), ("a", "g", _TN)),
            _NT: (("g", "b", _NN), ("g", "a", _TN)),
            _TN: (("b", "g", _NT), ("a", "g", _NN))}


def _dot1(a, b, dims):
    return lax.dot_general(a.astype(BF16), b.astype(BF16), (dims, ((), ())), preferred_element_type=F32)


def _split_bf16(t):
    hi = t.astype(BF16)
    return hi, (t - hi.astype(F32)).astype(BF16)


def _dot3_raw(a, b, dims):
    ah, al = _split_bf16(a)
    bh, bl = _split_bf16(b)
    d = lambda p, q: lax.dot_general(p, q, (dims, ((), ())), preferred_element_type=F32)
    return d(ah, bh) + (d(ah, bl) + d(al, bh))


def _with_f32_cotangents(raw):
    @functools.partial(jax.custom_vjp, nondiff_argnums=(2,))
    def dot(a, b, dims):
        return raw(a, b, dims)

    def fwd(a, b, dims):
        return raw(a, b, dims), (a, b)

    def bwd(dims, res, g):
        a, b = res
        ops = {"a": a, "b": b, "g": g}
        (p, q, dp), (r, s, dr) = _DOT_VJP[dims]
        return raw(ops[p], ops[q], dp), raw(ops[r], ops[s], dr)

    dot.defvjp(fwd, bwd)
    return dot


def _halves(t, axis):
    n = t.shape[axis] // 2
    return lax.slice_in_dim(t, 0, n, axis=axis), lax.slice_in_dim(t, n, 2 * n, axis=axis)


@functools.partial(jax.custom_vjp, nondiff_argnums=(2,))
def _stack(a, b, axis):
    return jnp.concatenate([a, b], axis=axis)


_stack.defvjp(lambda a, b, axis: (jnp.concatenate([a, b], axis=axis), None),
              lambda axis, _, g: _halves(g, axis))


@functools.partial(jax.custom_vjp, nondiff_argnums=(1,))
def _unstack(t, axis):
    return _halves(t, axis)


_unstack.defvjp(lambda t, axis: (_halves(t, axis), None),
                lambda axis, _, g: (jnp.concatenate(list(g), axis=axis),))


_bdot = _with_f32_cotangents(_dot1)
_dot3 = _with_f32_cotangents(_dot3_raw)


def _swa_heads(q, k, v, bias, sink, valid):
    R = range(len(q))
    kv = [h // SWA_KV_GROUP for h in R]
    s = [_bdot(q[h], k[kv[h]], _NT) * (SWA_HEAD_DIM ** -0.5) + bias[h] for h in R]
    s = [jnp.where(valid, x, -1e30) for x in s]
    m = [lax.stop_gradient(jnp.maximum(jnp.max(s[h], axis=1, keepdims=True), sink[h])) for h in R]
    p = [jnp.exp(s[h] - m[h]) for h in R]
    den = [jnp.sum(p[h], axis=1, keepdims=True) + jnp.exp(sink[h] - m[h]) for h in R]
    return [_bdot(p[h] / den[h], v[kv[h]], _NN) for h in R]


def _swa_valid(n, L):
    qi = lax.broadcasted_iota(jnp.int32, (SWA_BLOCK, 3 * SWA_BLOCK), 0)
    sj = lax.broadcasted_iota(jnp.int32, (SWA_BLOCK, 3 * SWA_BLOCK), 1)
    rel = sj - SWA_BLOCK - qi
    kpos = n * SWA_BLOCK + sj - SWA_BLOCK
    return (jnp.abs(rel) <= WINDOW) & (kpos >= 0) & (kpos < L)


def _swa_specs(HQ, HKV, L):
    B = SWA_BLOCK
    q_spec = pl.BlockSpec((HQ, B, SWA_HEAD_DIM), lambda n: (0, n, 0))
    kv_spec = pl.BlockSpec((HKV, L + 2 * B, SWA_HEAD_DIM), lambda n: (0, 0, 0))
    bias_spec = pl.BlockSpec((HQ, B, 3 * B), lambda n: (0, 0, 0))
    sink_spec = pl.BlockSpec((HQ, B, 1), lambda n: (0, 0, 0))
    return q_spec, kv_spec, bias_spec, sink_spec


def _swa_fwd_call(q, kpad, vpad, bias, sink):
    HQ, L, _ = q.shape
    HKV = kpad.shape[0]
    B = SWA_BLOCK

    def body(q_ref, k_ref, v_ref, b_ref, s_ref, o_ref):
        n = pl.program_id(0)
        rows = pl.ds(pl.multiple_of(n * B, B), 3 * B)
        out = _swa_heads([q_ref[h] for h in range(HQ)], [k_ref[g, rows, :] for g in range(HKV)],
                         [v_ref[g, rows, :] for g in range(HKV)], [b_ref[h] for h in range(HQ)],
                         [s_ref[h] for h in range(HQ)], _swa_valid(n, L))
        for h in range(HQ):
            o_ref[h] = out[h]

    q_spec, kv_spec, bias_spec, sink_spec = _swa_specs(HQ, HKV, L)
    return pl.pallas_call(
        body,
        out_shape=jax.ShapeDtypeStruct(q.shape, F32),
        grid=(L // B,),
        in_specs=[q_spec, kv_spec, kv_spec, bias_spec, sink_spec],
        out_specs=q_spec,
        compiler_params=_params(("parallel",)),
        name="swa_fwd",
    )(q, kpad, vpad, bias, sink)


def _swa_bwd_call(q, kpad, vpad, bias, sink, do):
    HQ, L, _ = q.shape
    HKV = kpad.shape[0]
    B = SWA_BLOCK

    def body(q_ref, k_ref, v_ref, b_ref, s_ref, do_ref, dq_ref, dk_ref, dv_ref, db_ref, ds_ref):
        n = pl.program_id(0)

        @pl.when(n == 0)
        def _():
            dk_ref[...] = jnp.zeros_like(dk_ref)
            dv_ref[...] = jnp.zeros_like(dv_ref)
            db_ref[...] = jnp.zeros_like(db_ref)
            ds_ref[...] = jnp.zeros_like(ds_ref)

        rows = pl.ds(pl.multiple_of(n * B, B), 3 * B)
        _, vjp = jax.vjp(functools.partial(_swa_heads, valid=_swa_valid(n, L)),
                         [q_ref[h] for h in range(HQ)], [k_ref[g, rows, :] for g in range(HKV)],
                         [v_ref[g, rows, :] for g in range(HKV)], [b_ref[h] for h in range(HQ)],
                         [s_ref[h] for h in range(HQ)])
        dq, dk, dv, db, dsk = vjp([do_ref[h] for h in range(HQ)])
        for h in range(HQ):
            dq_ref[h] = dq[h]
            db_ref[h] += db[h]
            ds_ref[h] += dsk[h]
        for g in range(HKV):
            dk_ref[g, rows, :] += dk[g]
            dv_ref[g, rows, :] += dv[g]

    q_spec, kv_spec, bias_spec, sink_spec = _swa_specs(HQ, HKV, L)
    return pl.pallas_call(
        body,
        out_shape=(jax.ShapeDtypeStruct(q.shape, F32), jax.ShapeDtypeStruct(kpad.shape, F32),
                   jax.ShapeDtypeStruct(vpad.shape, F32), jax.ShapeDtypeStruct(bias.shape, F32),
                   jax.ShapeDtypeStruct(sink.shape, F32)),
        grid=(L // B,),
        in_specs=[q_spec, kv_spec, kv_spec, bias_spec, sink_spec, q_spec],
        out_specs=(q_spec, kv_spec, kv_spec, bias_spec, sink_spec),
        compiler_params=_params(("arbitrary",)),
        name="swa_bwd",
    )(q, kpad, vpad, bias, sink, do)


@jax.custom_vjp
def window_attention(q, kpad, vpad, bias, sink):
    return _swa_fwd_call(q, kpad, vpad, bias, sink)


def _window_attention_fwd(q, kpad, vpad, bias, sink):
    return _swa_fwd_call(q, kpad, vpad, bias, sink), (q, kpad, vpad, bias, sink)


def _window_attention_bwd(res, do):
    return _swa_bwd_call(*res, do)


window_attention.defvjp(_window_attention_fwd, _window_attention_bwd)


def _unit_tri_inverses(a):
    C = a[0].shape[0]
    ii = lax.broadcasted_iota(jnp.int32, (C, C), 0)
    jj = lax.broadcasted_iota(jnp.int32, (C, C), 1)
    pw = [-x for x in a]
    t = [jnp.where(ii == jj, 1.0, 0.0)] * len(a)
    for _ in range(int(math.log2(C))):
        both = [_dot3_raw(jnp.concatenate([ts, ps], axis=0), ps, _NN) for ts, ps in zip(t, pw)]
        t = [ts + b[:C] for ts, b in zip(t, both)]
        pw = [b[C:] for b in both]
    return t


@jax.custom_vjp
def _known_inverse(a, t):
    return t


def _known_inverse_fwd(a, t):
    return t, t


def _known_inverse_bwd(t, dt):
    da = -_dot3_raw(_dot3_raw(t, dt, _TN), t, _NT)
    return da, jnp.zeros_like(t)


_known_inverse.defvjp(_known_inverse_fwd, _known_inverse_bwd)


def _gdn_chunks(S, q, k, v, gc, gr, gl, beta, t_known, backwards):
    R = range(len(q))
    C = q[0].shape[0]
    ii = lax.broadcasted_iota(jnp.int32, (C, C), 0)
    jj = lax.broadcasted_iota(jnp.int32, (C, C), 1)
    causal = [(ii <= jj) if backwards[s] else (ii >= jj) for s in R]
    strict = [(ii < jj) if backwards[s] else (ii > jj) for s in R]
    decay = [jnp.where(causal[s], jnp.exp(jnp.where(causal[s], gc[s] - gr[s], 0.0)), 0.0) for s in R]
    kb = [k[s] * beta[s] for s in R]
    kk_qk = [_unstack(_bdot(_stack(kb[s], q[s], 0), k[s], _NT), 0) for s in R]
    a = [jnp.where(strict[s], kk_qk[s][0] * decay[s], 0.0) for s in R]
    t = _unit_tri_inverses(a) if t_known is None else [_known_inverse(a[s], t_known[s]) for s in R]
    u_w = [_unstack(_dot3(t[s], _stack(v[s] * beta[s], kb[s] * jnp.exp(gc[s]), 1), _NN), 1) for s in R]
    ws_qs = [_unstack(_bdot(_stack(u_w[s][1], q[s] * jnp.exp(gc[s]), 0), S[s], _NN), 0) for s in R]
    v_new = [u_w[s][0] - ws_qs[s][0] for s in R]
    o = [ws_qs[s][1] + _bdot(kk_qk[s][1] * decay[s], v_new[s], _NN) for s in R]
    s_new = [S[s] * jnp.exp(gl[s]) + _bdot(k[s] * jnp.exp(gl[s] - gc[s]), v_new[s], _TN) for s in R]
    return s_new, o, t


def _gdn_specs(H, L, C, Dh, flip):
    NC = L // C
    idx = (lambda c: NC - 1 - c) if flip else (lambda c: c)
    seq = pl.BlockSpec((C, H * Dh), lambda c: (idx(c), 0))

    def scalar(rows, cols, group):
        return pl.BlockSpec((H, 1, rows, cols), lambda c: (group, idx(c), 0, 0))

    return seq, scalar


def _gdn_fwd_call(q, k, v, gc, gr, gl, beta):
    L, HD = q.shape
    C = gc.shape[2]
    NC = L // C
    H = gc.shape[0] // 2
    Dh = HD // H

    def body(qf, kf, vf, qb, kb, vb, gcf, grf, glf, bf, gcb, grb, glb, bb,
             of_ref, ob_ref, stf_ref, stb_ref, tf_ref, tb_ref, s_scr):
        @pl.when(pl.program_id(0) == 0)
        def _():
            s_scr[...] = jnp.zeros_like(s_scr)

        groups = ((qf, kf, vf, gcf, grf, glf, bf), (qb, kb, vb, gcb, grb, glb, bb))
        outs = ((of_ref, stf_ref, tf_ref), (ob_ref, stb_ref, tb_ref))
        seqs = [(d, h) for d in range(2) for h in range(H)]
        cols = [slice(h * Dh, (h + 1) * Dh) for h in range(H)]
        s0 = [s_scr[d * H + h] for d, h in seqs]
        for (d, h), s in zip(seqs, s0):
            outs[d][1][h, 0] = s
        seq_in = lambda j: [groups[d][j][:, cols[h]] for d, h in seqs]
        scal_in = lambda j: [groups[d][j][h, 0] for d, h in seqs]
        s1, o, t = _gdn_chunks(s0, seq_in(0), seq_in(1), seq_in(2), scal_in(3), scal_in(4), scal_in(5), scal_in(6),
                               None, [d == 1 for d, _ in seqs])
        for i, (d, h) in enumerate(seqs):
            s_scr[d * H + h] = s1[i]
            outs[d][0][:, cols[h]] = o[i]
            outs[d][2][h, 0] = t[i]

    seq_f, sc_f = _gdn_specs(H, L, C, Dh, False)
    seq_b, sc_b = _gdn_specs(H, L, C, Dh, True)
    sds = jax.ShapeDtypeStruct
    return pl.pallas_call(
        body,
        out_shape=(sds((L, HD), F32), sds((L, HD), F32), sds((H, NC, Dh, Dh), F32), sds((H, NC, Dh, Dh), F32),
                   sds((H, NC, C, C), F32), sds((H, NC, C, C), F32)),
        grid=(NC,),
        in_specs=[seq_f, seq_f, seq_f, seq_b, seq_b, seq_b,
                  sc_f(C, 1, 0), sc_f(1, C, 0), sc_f(1, 1, 0), sc_f(C, 1, 0),
                  sc_b(C, 1, 1), sc_b(1, C, 1), sc_b(1, 1, 1), sc_b(C, 1, 1)],
        out_specs=(seq_f, seq_b, sc_f(Dh, Dh, 0), sc_b(Dh, Dh, 0), sc_f(C, C, 0), sc_b(C, C, 0)),
        scratch_shapes=[pltpu.VMEM((2 * H, Dh, Dh), F32)],
        compiler_params=_params(("arbitrary",)),
        name="gdn_fwd",
    )(q, k, v, q, k, v, gc, gr, gl, beta, gc, gr, gl, beta)


def _gdn_bwd_call(q, k, v, gc, gr, gl, beta, st_f, st_b, t_f, t_b, do_f, do_b):
    L, HD = q.shape
    C = gc.shape[2]
    NC = L // C
    H = gc.shape[0] // 2
    Dh = HD // H

    def body(qf, kf, vf, qb, kb, vb, gcf, grf, glf, bf, gcb, grb, glb, bb, stf, stb, tf, tb, dof, dob,
             dqf, dkf, dvf, dqb, dkb, dvb, dgcf, dgrf, dglf, dbf, dgcb, dgrb, dglb, dbb, ds_scr):
        @pl.when(pl.program_id(0) == 0)
        def _():
            ds_scr[...] = jnp.zeros_like(ds_scr)

        groups = ((qf, kf, vf, gcf, grf, glf, bf, stf, tf, dof), (qb, kb, vb, gcb, grb, glb, bb, stb, tb, dob))
        outs = ((dqf, dkf, dvf, dgcf, dgrf, dglf, dbf), (dqb, dkb, dvb, dgcb, dgrb, dglb, dbb))
        seqs = [(d, h) for d in range(2) for h in range(H)]
        cols = [slice(h * Dh, (h + 1) * Dh) for h in range(H)]
        seq_in = lambda j: [groups[d][j][:, cols[h]] for d, h in seqs]
        scal_in = lambda j: [groups[d][j][h, 0] for d, h in seqs]
        t_known = scal_in(8)
        backwards = [d == 1 for d, _ in seqs]

        def chunks(*args):
            return _gdn_chunks(*args, t_known, backwards)[:2]

        _, vjp = jax.vjp(chunks, scal_in(7), seq_in(0), seq_in(1), seq_in(2), scal_in(3), scal_in(4), scal_in(5),
                         scal_in(6))
        grads = vjp(([ds_scr[d * H + h] for d, h in seqs], seq_in(9)))
        for i, (d, h) in enumerate(seqs):
            ds_scr[d * H + h] = grads[0][i]
            for j in range(3):
                outs[d][j][:, cols[h]] = grads[1 + j][i]
            for j in range(3, 7):
                outs[d][j][h, 0] = grads[1 + j][i]

    seq_f, sc_f = _gdn_specs(H, L, C, Dh, True)
    seq_b, sc_b = _gdn_specs(H, L, C, Dh, False)
    sds = jax.ShapeDtypeStruct
    seq_out = sds((L, HD), F32)
    half = lambda t: sds((H,) + t.shape[1:], F32)
    scal_f = [sc_f(C, 1, 0), sc_f(1, C, 0), sc_f(1, 1, 0), sc_f(C, 1, 0)]
    scal_b = [sc_b(C, 1, 1), sc_b(1, C, 1), sc_b(1, 1, 1), sc_b(C, 1, 1)]
    scal_b_out = [sc_b(C, 1, 0), sc_b(1, C, 0), sc_b(1, 1, 0), sc_b(C, 1, 0)]
    return pl.pallas_call(
        body,
        out_shape=(seq_out,) * 6 + (half(gc), half(gr), half(gl), half(beta)) * 2,
        grid=(NC,),
        in_specs=[seq_f, seq_f, seq_f, seq_b, seq_b, seq_b] + scal_f + scal_b
                 + [sc_f(Dh, Dh, 0), sc_b(Dh, Dh, 0), sc_f(C, C, 0), sc_b(C, C, 0), seq_f, seq_b],
        out_specs=[seq_f, seq_f, seq_f, seq_b, seq_b, seq_b] + scal_f + scal_b_out,
        scratch_shapes=[pltpu.VMEM((2 * H, Dh, Dh), F32)],
        compiler_params=_params(("arbitrary",)),
        name="gdn_bwd",
    )(q, k, v, q, k, v, gc, gr, gl, beta, gc, gr, gl, beta, st_f, st_b, t_f, t_b, do_f, do_b)


@jax.custom_vjp
def gated_delta_rule(q, k, v, gc, gr, gl, beta):
    return _gdn_fwd_call(q, k, v, gc, gr, gl, beta)[:2]


def _gated_delta_rule_fwd(q, k, v, gc, gr, gl, beta):
    o_f, o_b, st_f, st_b, t_f, t_b = _gdn_fwd_call(q, k, v, gc, gr, gl, beta)
    return (o_f, o_b), (q, k, v, gc, gr, gl, beta, st_f, st_b, t_f, t_b)


def _gated_delta_rule_bwd(res, do):
    (dqf, dkf, dvf, dqb, dkb, dvb, dgcf, dgrf, dglf, dbf, dgcb, dgrb, dglb, dbb) = _gdn_bwd_call(*res, *do)
    cat = lambda a, b: jnp.concatenate([a, b], axis=0)
    return dqf + dqb, dkf + dkb, dvf + dvb, cat(dgcf, dgcb), cat(dgrf, dgrb), cat(dglf, dglb), cat(dbf, dbb)


gated_delta_rule.defvjp(_gated_delta_rule_fwd, _gated_delta_rule_bwd)


GATE_ROWS = 128


def _gate_specs(L, D, nb):
    tm = _tile(L, GATE_ROWS, SUBLANES)
    logit_specs = [pl.BlockSpec((tm, D), functools.partial(lambda i, b: (i, b), b=b)) for b in range(nb)]
    row_spec = pl.BlockSpec((tm, D), lambda i: (i, 0))
    return tm, logit_specs, row_spec


def _gate_fwd_call(logits, branches):
    nb = len(branches)
    L, D = branches[0].shape
    tm, logit_specs, row_spec = _gate_specs(L, D, nb)

    def body(*refs):
        o_ref = refs[2 * nb]
        acc = jax.nn.sigmoid(refs[0][...]) * refs[nb][...]
        for b in range(1, nb):
            acc = acc + jax.nn.sigmoid(refs[b][...]) * refs[nb + b][...]
        o_ref[...] = acc

    return pl.pallas_call(
        body, out_shape=jax.ShapeDtypeStruct((L, D), F32), grid=(L // tm,),
        in_specs=logit_specs + [row_spec] * nb, out_specs=row_spec,
        compiler_params=_params(("parallel",)), name="gate_merge_fwd",
    )(*([logits] * nb), *branches)


def _gate_bwd_call(logits, branches, dm):
    nb = len(branches)
    L, D = branches[0].shape
    tm, logit_specs, row_spec = _gate_specs(L, D, nb)

    def body(*refs):
        dm_v = refs[2 * nb][...]
        dl_ref = refs[2 * nb + 1]
        for b in range(nb):
            sig = jax.nn.sigmoid(refs[b][...])
            refs[2 * nb + 2 + b][...] = dm_v * sig
            dl_ref[:, b * D:(b + 1) * D] = dm_v * refs[nb + b][...] * (sig * (1.0 - sig))

    return pl.pallas_call(
        body,
        out_shape=[jax.ShapeDtypeStruct(logits.shape, F32)] + [jax.ShapeDtypeStruct((L, D), F32)] * nb,
        grid=(L // tm,),
        in_specs=logit_specs + [row_spec] * (nb + 1),
        out_specs=[pl.BlockSpec((tm, nb * D), lambda i: (i, 0))] + [row_spec] * nb,
        compiler_params=_params(("parallel",)), name="gate_merge_bwd",
    )(*([logits] * nb), *branches, dm)


@jax.custom_vjp
def gate_merge(logits, branches):
    return _gate_fwd_call(logits, branches)


def _gate_merge_fwd(logits, branches):
    return _gate_fwd_call(logits, branches), (logits, branches)


def _gate_merge_bwd(res, dm):
    logits, branches = res
    out = _gate_bwd_call(logits, branches, dm)
    return out[0], tuple(out[1:])


gate_merge.defvjp(_gate_merge_fwd, _gate_merge_bwd)


def _loss_fwd_call(y, t):
    L, D = y.shape
    tm = _rms_rows(L, D)

    def body(y_ref, t_ref, o_ref):
        @pl.when(pl.program_id(0) == 0)
        def _():
            o_ref[...] = jnp.zeros_like(o_ref)

        e = y_ref[...] - t_ref[...]
        part = jnp.sum(jnp.sum(e * e, axis=1, keepdims=True), axis=0, keepdims=True)
        o_ref[...] += part * (0.5 / D)

    out = pl.pallas_call(
        body,
        out_shape=jax.ShapeDtypeStruct((SUBLANES, LANES), F32),
        grid=(L // tm,),
        in_specs=[pl.BlockSpec((tm, D), lambda i: (i, 0)), pl.BlockSpec((tm, D), lambda i: (i, 0))],
        out_specs=pl.BlockSpec((SUBLANES, LANES), lambda i: (0, 0)),
        compiler_params=_params(("arbitrary",)),
        name="loss_fwd",
    )(y, t)
    return out[0, 0]


def _loss_bwd_call(y, t, g):
    L, D = y.shape
    tm = _rms_rows(L, D)

    def body(y_ref, t_ref, g_ref, o_ref):
        o_ref[...] = (y_ref[...] - t_ref[...]) * (g_ref[...] * (1.0 / D))

    return pl.pallas_call(
        body,
        out_shape=jax.ShapeDtypeStruct((L, D), F32),
        grid=(L // tm,),
        in_specs=[pl.BlockSpec((tm, D), lambda i: (i, 0)), pl.BlockSpec((tm, D), lambda i: (i, 0)),
                  pl.BlockSpec((1, 1), lambda i: (0, 0))],
        out_specs=pl.BlockSpec((tm, D), lambda i: (i, 0)),
        compiler_params=_params(("parallel",)),
        name="loss_bwd",
    )(y, t, g.reshape(1, 1))


@jax.custom_vjp
def loss_head(y, t):
    return _loss_fwd_call(y, t)


def _loss_head_fwd(y, t):
    return _loss_fwd_call(y, t), (y, t)


def _loss_head_bwd(res, g):
    y, t = res
    return _loss_bwd_call(y, t, g), jnp.zeros_like(t)


loss_head.defvjp(_loss_head_fwd, _loss_head_bwd)


def _rows_for(C, nbuf):
    return max(16, ((24 << 20) // (nbuf * 4 * C)) // 16 * 16)


def _pair_add_call(a, b, name):
    R, C = a.shape
    tm = _tile(R, _rows_for(C, 6), 16)

    def body(a_ref, b_ref, o_ref):
        o_ref[...] = (a_ref[...].astype(F32) + b_ref[...].astype(F32)).astype(o_ref.dtype)

    spec = pl.BlockSpec((tm, C), lambda i: (i, 0))
    return pl.pallas_call(
        body, out_shape=jax.ShapeDtypeStruct(a.shape, a.dtype), grid=(R // tm,),
        in_specs=[spec, spec], out_specs=spec, compiler_params=_params(("parallel",)), name=name,
    )(a, b)


def _adamw_call(parts, w, m, v, name):
    P, R, C = parts.shape
    tm = _tile(R, _rows_for(C, 2 * (P + 7)), 16)

    def body(p_ref, w_ref, m_ref, v_ref, g_ref, d_ref, mo_ref, vo_ref):
        g = p_ref[0].astype(F32)
        for s in range(1, P):
            g = g + p_ref[s].astype(F32)
        m2 = ADAM_B1 * m_ref[...] + (1.0 - ADAM_B1) * g
        v2 = ADAM_B2 * v_ref[...] + (1.0 - ADAM_B2) * jnp.square(g)
        m_hat = m2 / (1.0 - ADAM_B1 ** ADAM_STEP)
        v_hat = v2 / (1.0 - ADAM_B2 ** ADAM_STEP)
        g_ref[...] = g
        d_ref[...] = -ADAM_LR * (m_hat / (jnp.sqrt(v_hat) + ADAM_EPS) + ADAM_WD * w_ref[...])
        mo_ref[...] = m2
        vo_ref[...] = v2

    spec = pl.BlockSpec((tm, C), lambda i: (i, 0))
    out = jax.ShapeDtypeStruct((R, C), F32)
    return pl.pallas_call(
        body, out_shape=(out, out, out, out), grid=(R // tm,),
        in_specs=[pl.BlockSpec((P, tm, C), lambda i: (0, i, 0)), spec, spec, spec],
        out_specs=(spec, spec, spec, spec), compiler_params=_params(("parallel",)), name=name,
    )(parts, w, m, v)


_MESH = pl.DeviceIdType.MESH
FLIPS_CHIPS = ((1, 0, 0), (0, 1, 0), (1, 1, 0))
FLIPS_ALL = ((0, 0, 1), (1, 0, 0), (0, 1, 0), (1, 1, 0), (1, 0, 1), (0, 1, 1), (1, 1, 1))
FLIPS_SIBLING = ((0, 0, 1),)


def _exchange(arrays, flips, mode, name):
    n = len(arrays)
    nf = len(flips)
    if flips == FLIPS_SIBLING:
        n_slots, slot = 2, (lambda x, y, c: c)
    elif any(f[2] for f in flips):
        n_slots, slot = 8, (lambda x, y, c: 4 * x + 2 * y + c)
    else:
        n_slots, slot = 4, (lambda x, y, c: 2 * x + y)

    def body(*refs):
        ins, outs = refs[:n], refs[n:2 * n]
        send_sems, recv_sems = refs[2 * n:]
        x, y, c = lax.axis_index("x"), lax.axis_index("y"), lax.axis_index("c")
        me = slot(x, y, c)
        peers = [(x + f[0] - 2 * x * f[0], y + f[1] - 2 * y * f[1], c + f[2] - 2 * c * f[2]) for f in flips]

        def copy(i, k, sending):
            px, py, pc = peers[k]
            there = slot(px, py, pc)
            if mode == "swap":
                src, dst = ins[i], outs[i]
            elif mode == "swap_half":
                h = arrays[i].shape[1] // 2
                src, dst = ins[i].at[:, pl.ds(pc * h, h)], outs[i]
            elif mode == "gather":
                src, dst = ins[i], outs[i].at[me if sending else there]
            else:
                src, dst = ins[i].at[there if sending else me], outs[i].at[me if sending else there]
            return pltpu.make_async_remote_copy(src_ref=src, dst_ref=dst, send_sem=send_sems.at[i, k],
                                                recv_sem=recv_sems.at[i, k], device_id=(px, py, pc),
                                                device_id_type=_MESH)

        sends = [copy(i, k, True) for i in range(n) for k in range(nf)]
        for cp in sends:
            cp.start()
        for i in range(n):
            for k in range(nf):
                copy(i, k, False).wait_recv()
        for cp in sends:
            cp.wait_send()

    def with_own_slot(outs):
        if mode not in ("gather", "scatter"):
            return outs
        me = slot(lax.axis_index("x"), lax.axis_index("y"), lax.axis_index("c"))
        own = [a if mode == "gather" else lax.dynamic_index_in_dim(a, me, 0, keepdims=False) for a in arrays]
        return [lax.dynamic_update_index_in_dim(o, a, me, 0) for o, a in zip(outs, own)]

    if mode == "gather":
        out_shape = [jax.ShapeDtypeStruct((n_slots,) + a.shape, a.dtype) for a in arrays]
    elif mode == "swap_half":
        out_shape = [jax.ShapeDtypeStruct((a.shape[0], a.shape[1] // 2) + a.shape[2:], a.dtype) for a in arrays]
    else:
        out_shape = [jax.ShapeDtypeStruct(a.shape, a.dtype) for a in arrays]
    any_spec = pl.BlockSpec(memory_space=pl.ANY)
    return with_own_slot(pl.pallas_call(
        body,
        out_shape=out_shape,
        in_specs=[any_spec] * n,
        out_specs=[any_spec] * n,
        scratch_shapes=[pltpu.SemaphoreType.DMA((n, nf)), pltpu.SemaphoreType.DMA((n, nf))],
        compiler_params=pltpu.CompilerParams(has_side_effects=True),
        name=name,
    )(*arrays))


def _two_level_all_gather(arrays, name):
    n = len(arrays)
    nf = len(FLIPS_CHIPS)

    def body(*refs):
        ins, outs = refs[:n], refs[n:2 * n]
        send_sems, recv_sems, pass_send_sems, pass_recv_sems = refs[2 * n:]
        x, y, c = lax.axis_index("x"), lax.axis_index("y"), lax.axis_index("c")
        me = 2 * x + y
        chips = [(x + f[0] - 2 * x * f[0], y + f[1] - 2 * y * f[1]) for f in FLIPS_CHIPS]

        def rows(i, core):
            h = arrays[i].shape[0] // 2
            return pl.ds(core * h, h)

        def fetch(i, k, sending):
            px, py = chips[k]
            dst = outs[i].at[me if sending else 2 * px + py, rows(i, c)]
            return pltpu.make_async_remote_copy(src_ref=ins[i].at[rows(i, c)], dst_ref=dst,
                                                send_sem=send_sems.at[i, k], recv_sem=recv_sems.at[i, k],
                                                device_id=(px, py, c), device_id_type=_MESH)

        def hand_on(i, k, sending):
            px, py = chips[k]
            part = outs[i].at[2 * px + py, rows(i, c if sending else 1 - c)]
            return pltpu.make_async_remote_copy(src_ref=part, dst_ref=part, send_sem=pass_send_sems.at[i, k],
                                                recv_sem=pass_recv_sems.at[i, k], device_id=(x, y, 1 - c),
                                                device_id_type=_MESH)

        pairs = [(i, k) for i in range(n) for k in range(nf)]
        for i, k in pairs:
            fetch(i, k, True).start()
        for i, k in pairs:
            fetch(i, k, False).wait_recv()
            hand_on(i, k, True).start()
        for i, k in pairs:
            hand_on(i, k, False).wait_recv()
        for i, k in pairs:
            fetch(i, k, True).wait_send()
            hand_on(i, k, True).wait_send()

    any_spec = pl.BlockSpec(memory_space=pl.ANY)
    sems = pltpu.SemaphoreType.DMA((n, nf))
    gathered = pl.pallas_call(
        body,
        out_shape=[jax.ShapeDtypeStruct((4,) + a.shape, a.dtype) for a in arrays],
        in_specs=[any_spec] * n,
        out_specs=[any_spec] * n,
        scratch_shapes=[sems, sems, sems, sems],
        compiler_params=pltpu.CompilerParams(has_side_effects=True),
        name=name,
    )(*arrays)
    me = 2 * lax.axis_index("x") + lax.axis_index("y")
    return [lax.dynamic_update_index_in_dim(g, a, me, 0) for g, a in zip(gathered, arrays)]


def _sum_parts_call(parts, name):
    P, R, C = parts.shape
    tm = _tile(R, _rows_for(C, 2 * (P + 2)), 16)

    def body(p_ref, o_ref):
        g = p_ref[0].astype(F32)
        for s in range(1, P):
            g = g + p_ref[s].astype(F32)
        o_ref[...] = g

    return pl.pallas_call(
        body, out_shape=jax.ShapeDtypeStruct((R, C), F32), grid=(R // tm,),
        in_specs=[pl.BlockSpec((P, tm, C), lambda i: (0, i, 0))], out_specs=pl.BlockSpec((tm, C), lambda i: (i, 0)),
        compiler_params=_params(("parallel",)), name=name,
    )(parts)


WEIGHT_NAMES = ('w_in', 's5_lam_re', 's5_lam_im', 's5_log_step', 's5_b_re', 's5_b_im', 's5_c_re', 's5_c_im', 's5_d',
                's5_w_glu', 's5_b_glu', 'gdn_conv', 'gdn_a_log', 'gdn_dt_bias', 'gdn_o_gain', 'swa_sink', 't5_bias',
                'mla_q_gain', 'mla_kv_gain', 'mla_w_uq', 'mla_w_ukv', 'w_branch', 'w_out', 'mix_pre_gain',
                'mix_post_gain', 'mlp_pre_gain', 'mlp_post_gain', 'w_mlp_in', 'w_mlp_out')
SHARDED = {'w_in': (2, BF16), 's5_w_glu': (1, BF16), 'gdn_conv': (2, F32), 'mla_w_uq': (2, BF16),
           'mla_w_ukv': (2, BF16), 'w_branch': (3, BF16), 'w_out': (1, BF16), 'w_mlp_in': (2, BF16),
           'w_mlp_out': (1, BF16)}
REPLICATED = tuple(n for n in WEIGHT_NAMES if n not in SHARDED)
PACK_COLS = 1024

_IN_A = (('s5_u', 512), ('gdn_qkv', 1536), ('gdn_z', 512), ('gdn_beta', 8), ('gdn_decay', 8))
_IN_B = (('swa_q', 512), ('swa_kv', 256), ('mla_cq', 384), ('mla_ckv', 512), ('mla_kr', 64))
_IN_A_W = sum(w for _, w in _IN_A)
_IN_B_W = sum(w for _, w in _IN_B)
_IN_A_PAD = -_IN_A_W % LANES
_IN_B_PAD = -(_IN_A_W + _IN_A_PAD + _IN_B_W) % 512


def _assemble(g, axis):
    t = jnp.moveaxis(g, 0, axis)
    shape = t.shape[:axis] + (t.shape[axis] * t.shape[axis + 1],) + t.shape[axis + 2:]
    return t.reshape(shape)


def _s5_mixer(u, lam_re, lam_im, log_step, b_re, b_im, c_re, c_im, d_skip, w_glu, b_glu):
    nd, G, P = lam_re.shape
    Hg = b_re.shape[-1]
    lam_re = jnp.minimum(lam_re, -1e-4)
    dt = jnp.exp(log_step)[..., None]
    mag = jnp.exp(lam_re * dt)
    abar_r = mag * jnp.cos(lam_im * dt)
    abar_i = mag * jnp.sin(lam_im * dt)
    den = lam_re * lam_re + lam_im * lam_im
    xr = abar_r - 1.0
    xi = abar_i
    coef_r = (xr * lam_re + xi * lam_im) / den
    coef_i = (xi * lam_re - xr * lam_im) / den
    bbar_r = coef_r[..., None] * b_re - coef_i[..., None] * b_im
    bbar_i = coef_r[..., None] * b_im + coef_i[..., None] * b_re
    eye = jnp.eye(G, dtype=F32)

    def dense_b(bb):
        return jnp.einsum('dgph,gk->ghdkp', bb, eye).reshape(G * Hg, nd, G * P)

    def dense_c(cc):
        return jnp.einsum('dghp,gk->khdgp', cc, eye).reshape(G * Hg, nd, G * P)

    b_cat = _to_scan_cols(jnp.stack([dense_b(bbar_r), dense_b(bbar_i)], axis=2), nd)
    c_cat = _to_scan_cols(jnp.stack([dense_c(c_re), -dense_c(c_im)], axis=2), nd).T
    a_row = _to_scan_cols(jnp.stack([abar_r.reshape(nd, G * P), abar_i.reshape(nd, G * P)], axis=1), nd)[None]
    s = s5_scan(linear(u, b_cat, 's5_in'), a_row, nd)
    y = linear(s, c_cat, 's5_out') + d_skip * u
    y = jax.nn.gelu(y)
    return y * jax.nn.sigmoid(linear(y, w_glu, 's5_glu') + b_glu)


def _gdn_mixer(qkv, z, beta_logits, decay_logits, conv_w, a_log, dt_bias, o_gain):
    L = qkv.shape[0]
    Dh = GDN_HEAD_DIM
    H = z.shape[1] // Dh
    C = GDN_CHUNK
    NC = L // C
    xp = jnp.pad(qkv, ((GDN_CONV // 2, GDN_CONV - 1 - GDN_CONV // 2), (0, 0)))
    conv = xp[0:L] * conv_w[0]
    for j in range(1, GDN_CONV):
        conv = conv + xp[j:j + L] * conv_w[j]
    q, k, v = jnp.split(jax.nn.silu(conv), 3, axis=-1)

    def l2n(t):
        return t * lax.rsqrt(jnp.sum(t * t, axis=-1, keepdims=True) + 1e-6)

    q = (l2n(q.reshape(L, H, Dh)) * (Dh ** -0.5)).reshape(L, H * Dh)
    k = l2n(k.reshape(L, H, Dh)).reshape(L, H * Dh)
    beta = jax.nn.sigmoid(beta_logits).reshape(L, 2, H)
    g = -jnp.exp(a_log) * jax.nn.softplus(decay_logits.reshape(L, 2, H) + dt_bias)

    def per_chunk(t, d):
        return t[:, d].T.reshape(H, NC, C)

    g_with = jnp.cumsum(per_chunk(g, 0), axis=-1)
    g_against = jnp.cumsum(per_chunk(g, 1)[..., ::-1], axis=-1)[..., ::-1]
    gs = jnp.concatenate([g_with, g_against], axis=0)
    totals = jnp.concatenate([g_with[..., -1], g_against[..., 0]], axis=0)
    betas = jnp.concatenate([per_chunk(beta, 0), per_chunk(beta, 1)], axis=0)
    o_with, o_against = gated_delta_rule(q, k, v, gs[..., None], gs[:, :, None, :], totals[..., None, None],
                                         betas[..., None])
    o = o_with + o_against
    o = rms_norm(o.reshape(L * H, Dh), o_gain, 'gdn_onorm').reshape(L, H, Dh)
    o = o * jax.nn.silu(z.reshape(L, H, Dh))
    return o.reshape(L, H * Dh)


def _t5_bucket(rel):
    nb = T5_BUCKETS // 2
    max_exact = nb // 2
    ret = jnp.where(rel > 0, nb, 0)
    n = jnp.abs(rel)
    nf = jnp.maximum(n, 1).astype(F32)
    large = max_exact + (jnp.log(nf / max_exact) / math.log(T5_MAX_DISTANCE / max_exact)
                         * (nb - max_exact)).astype(jnp.int32)
    large = jnp.minimum(large, nb - 1)
    return ret + jnp.where(n < max_exact, n, large)


def _swa_mixer(q, kv, sink, t5_bias):
    L = q.shape[0]
    B = SWA_BLOCK
    HQ = q.shape[1] // SWA_HEAD_DIM
    HKV = HQ // SWA_KV_GROUP
    qh = q.reshape(L, HQ, SWA_HEAD_DIM).transpose(1, 0, 2)
    k, v = jnp.split(kv, 2, axis=-1)

    def heads_padded(t):
        return jnp.pad(t.reshape(L, HKV, SWA_HEAD_DIM).transpose(1, 0, 2), ((0, 0), (B, B), (0, 0)))

    qi = jnp.arange(B)[:, None]
    sj = jnp.arange(3 * B)[None, :]
    one_hot = (_t5_bucket(sj - B - qi)[..., None] == jnp.arange(T5_BUCKETS)).astype(F32)
    bias = jnp.einsum('qsb,bh->hqs', one_hot, t5_bias, precision=_HIGHEST)
    sink_rows = jnp.broadcast_to(sink[:, None, None], (HQ, B, 1))
    o = window_attention(qh, heads_padded(k), heads_padded(v), bias, sink_rows)
    return o.transpose(1, 0, 2).reshape(L, HQ * SWA_HEAD_DIM)


def _apply_rope(x, cos, sin):
    x1, x2 = jnp.split(x, 2, axis=-1)
    return jnp.concatenate([x1 * cos - x2 * sin, x2 * cos + x1 * sin], axis=-1)


def _mla_mixer(c_q, c_kv, k_rope, q_gain, kv_gain, w_uq, w_ukv):
    L = c_q.shape[0]
    H = w_uq.shape[1] // (MLA_NOPE + MLA_ROPE)
    q = linear(rms_norm(c_q, q_gain, 'mla_qnorm'), w_uq, 'mla_uq').reshape(L, H, MLA_NOPE + MLA_ROPE)
    kv = linear(rms_norm(c_kv, kv_gain, 'mla_kvnorm'), w_ukv, 'mla_ukv').reshape(L, H, MLA_NOPE + MLA_V)
    q_nope, q_pe = q[..., :MLA_NOPE], q[..., MLA_NOPE:]
    k_nope, v = kv[..., :MLA_NOPE], kv[..., MLA_NOPE:]
    pos = jnp.arange(L, dtype=F32)
    inv_freq = ROPE_THETA ** (-jnp.arange(0, MLA_ROPE, 2, dtype=F32) / MLA_ROPE)
    ang = pos[:, None] * inv_freq[None, :]
    cos, sin = jnp.cos(ang)[:, None, :], jnp.sin(ang)[:, None, :]
    q_pe = _apply_rope(q_pe, cos, sin)
    k_pe = _apply_rope(k_rope[:, None, :], cos, sin)
    qf = jnp.concatenate([q_nope, q_pe], axis=-1).transpose(1, 0, 2)
    kf = jnp.concatenate([k_nope, jnp.broadcast_to(k_pe, (L, H, MLA_ROPE))], axis=-1).transpose(1, 0, 2)
    o = full_attention(qf, kf, v.transpose(1, 0, 2), (MLA_NOPE + MLA_ROPE) ** -0.5)
    return o.transpose(1, 0, 2).reshape(L, H * MLA_V)


_IN_LAYOUT = _IN_A + ((None, _IN_A_PAD),) + _IN_B + ((None, _IN_B_PAD),)


@jax.custom_vjp
def split_proj(proj):
    out, start = [], 0
    for name, width in _IN_LAYOUT:
        if name is not None:
            out.append(proj[:, start:start + width])
        start += width
    return tuple(out)


def _split_proj_bwd(_, cts):
    it = iter(cts)
    rows = cts[0].shape[0]
    parts = [jnp.zeros((rows, width), F32) if name is None else next(it) for name, width in _IN_LAYOUT]
    return (jnp.concatenate(parts, axis=1),)


split_proj.defvjp(lambda proj: (split_proj(proj), None), _split_proj_bwd)


def _local_loss(weights, x, target):
    p = {n: (_assemble(weights[n], SHARDED[n][0]) if n in SHARDED else weights[n]) for n in WEIGHT_NAMES}
    L, D = x.shape
    depth = p['w_in'].shape[0]
    for l in range(depth):
        w_in = p['w_in'][l]
        zeros = lambda n: jnp.zeros((D, n), w_in.dtype)
        w_r = jnp.concatenate([w_in[:, :_IN_A_W], zeros(_IN_A_PAD), w_in[:, _IN_A_W:_IN_A_W + _IN_B_W],
                               zeros(_IN_B_PAD)], axis=1)
        w_g = w_in[:, _IN_A_W + _IN_B_W:]
        h = rms_norm(x, p['mix_pre_gain'][l], 'mix_pre')
        proj = linear(h, w_r, 'in_proj')
        gate_logits = linear(h, w_g, 'in_gate')
        seg = dict(zip([name for name, _ in _IN_A + _IN_B], split_proj(proj)))
        y_a = _s5_mixer(seg['s5_u'], p['s5_lam_re'][l], p['s5_lam_im'][l], p['s5_log_step'][l], p['s5_b_re'][l],
                        p['s5_b_im'][l], p['s5_c_re'][l], p['s5_c_im'][l], p['s5_d'][l], p['s5_w_glu'][l],
                        p['s5_b_glu'][l])
        y_b = _gdn_mixer(seg['gdn_qkv'], seg['gdn_z'], seg['gdn_beta'], seg['gdn_decay'], p['gdn_conv'][l],
                         p['gdn_a_log'][l], p['gdn_dt_bias'][l], p['gdn_o_gain'][l])
        y_c = _swa_mixer(seg['swa_q'], seg['swa_kv'], p['swa_sink'][l], p['t5_bias'])
        y_d = _mla_mixer(seg['mla_cq'], seg['mla_ckv'], seg['mla_kr'], p['mla_q_gain'][l], p['mla_kv_gain'][l],
                         p['mla_w_uq'][l], p['mla_w_ukv'][l])
        branches = tuple(linear(y, p['w_branch'][l, b], 'branch') for b, y in enumerate((y_a, y_b, y_c, y_d)))
        merged = gate_merge(gate_logits, branches)
        x = add_rms_norm(x, linear(merged, p['w_out'][l], 'mix_out'), p['mix_post_gain'][l], 'mix_post')
        h = rms_norm(x, p['mlp_pre_gain'][l], 'mlp_pre')
        f = relu2_linear(linear(h, p['w_mlp_in'][l], 'mlp_in'), p['w_mlp_out'][l])
        x = add_rms_norm(x, f, p['mlp_post_gain'][l], 'mlp_post')
    return loss_head(x, target)


def _two_d(t, lead):
    return t.reshape(t.shape[:lead] + (-1, t.shape[-1]))


def _pack(arrays):
    flat = jnp.concatenate([a.reshape(-1) for a in arrays])
    pad = -flat.shape[0] % (16 * PACK_COLS)
    return jnp.pad(flat, (0, pad)).reshape(-1, PACK_COLS)


def _unpack(packed, like):
    flat = packed.reshape(-1)
    out, pos = [], 0
    for a in like:
        out.append(flat[pos:pos + a.size].reshape(a.shape))
        pos += a.size
    return out


def kernel(x, w_in, s5_lam_re, s5_lam_im, s5_log_step, s5_b_re, s5_b_im, s5_c_re, s5_c_im, s5_d, s5_w_glu, s5_b_glu, gdn_conv, gdn_a_log, gdn_dt_bias, gdn_o_gain, swa_sink, t5_bias, mla_q_gain, mla_kv_gain, mla_w_uq, mla_w_ukv, w_branch, w_out, mix_pre_gain, mix_post_gain, mlp_pre_gain, mlp_post_gain, w_mlp_in, w_mlp_out, loss_target, m_w_in, m_s5_lam_re, m_s5_lam_im, m_s5_log_step, m_s5_b_re, m_s5_b_im, m_s5_c_re, m_s5_c_im, m_s5_d, m_s5_w_glu, m_s5_b_glu, m_gdn_conv, m_gdn_a_log, m_gdn_dt_bias, m_gdn_o_gain, m_swa_sink, m_t5_bias, m_mla_q_gain, m_mla_kv_gain, m_mla_w_uq, m_mla_w_ukv, m_w_branch, m_w_out, m_mix_pre_gain, m_mix_post_gain, m_mlp_pre_gain, m_mlp_post_gain, m_w_mlp_in, m_w_mlp_out, v_w_in, v_s5_lam_re, v_s5_lam_im, v_s5_log_step, v_s5_b_re, v_s5_b_im, v_s5_c_re, v_s5_c_im, v_s5_d, v_s5_w_glu, v_s5_b_glu, v_gdn_conv, v_gdn_a_log, v_gdn_dt_bias, v_gdn_o_gain, v_swa_sink, v_t5_bias, v_mla_q_gain, v_mla_kv_gain, v_mla_w_uq, v_mla_w_ukv, v_w_branch, v_w_out, v_mix_pre_gain, v_mix_post_gain, v_mlp_pre_gain, v_mlp_post_gain, v_w_mlp_in, v_w_mlp_out):
    w = dict(zip(WEIGHT_NAMES, (w_in, s5_lam_re, s5_lam_im, s5_log_step, s5_b_re, s5_b_im, s5_c_re, s5_c_im, s5_d, s5_w_glu, s5_b_glu, gdn_conv, gdn_a_log, gdn_dt_bias, gdn_o_gain, swa_sink, t5_bias, mla_q_gain, mla_kv_gain, mla_w_uq, mla_w_ukv, w_branch, w_out, mix_pre_gain, mix_post_gain, mlp_pre_gain, mlp_post_gain, w_mlp_in, w_mlp_out)))
    m = dict(zip(WEIGHT_NAMES, (m_w_in, m_s5_lam_re, m_s5_lam_im, m_s5_log_step, m_s5_b_re, m_s5_b_im, m_s5_c_re, m_s5_c_im, m_s5_d, m_s5_w_glu, m_s5_b_glu, m_gdn_conv, m_gdn_a_log, m_gdn_dt_bias, m_gdn_o_gain, m_swa_sink, m_t5_bias, m_mla_q_gain, m_mla_kv_gain, m_mla_w_uq, m_mla_w_ukv, m_w_branch, m_w_out, m_mix_pre_gain, m_mix_post_gain, m_mlp_pre_gain, m_mlp_post_gain, m_w_mlp_in, m_w_mlp_out)))
    v = dict(zip(WEIGHT_NAMES, (v_w_in, v_s5_lam_re, v_s5_lam_im, v_s5_log_step, v_s5_b_re, v_s5_b_im, v_s5_c_re, v_s5_c_im, v_s5_d, v_s5_w_glu, v_s5_b_glu, v_gdn_conv, v_gdn_a_log, v_gdn_dt_bias, v_gdn_o_gain, v_swa_sink, v_t5_bias, v_mla_q_gain, v_mla_kv_gain, v_mla_w_uq, v_mla_w_ukv, v_w_branch, v_w_out, v_mix_pre_gain, v_mix_post_gain, v_mlp_pre_gain, v_mlp_post_gain, v_w_mlp_in, v_w_mlp_out)))
    sharded = tuple(SHARDED)

    gathered = _two_level_all_gather([w[n].astype(SHARDED[n][1]) for n in sharded], 'weights_all_gather')
    weights = dict(zip(sharded, gathered))
    weights.update({n: w[n] for n in REPLICATED})

    loss, (grads, grad_x) = jax.value_and_grad(_local_loss, argnums=(0, 1))(weights, x[0], loss_target[0])
    loss = lax.psum(loss, ('x', 'y', 'c'))

    core = lax.axis_index('c')
    from_sibling = _exchange([grads[n] for n in sharded], FLIPS_SIBLING, 'swap_half', 'grads_core_swap')
    chip_sums = []
    for n, r in zip(sharded, from_sibling):
        mine = lax.dynamic_slice_in_dim(grads[n], core * r.shape[1], r.shape[1], axis=1)
        chip_sums.append(_pair_add_call(_two_d(mine, 0), _two_d(r, 0), 'grads_core_add').reshape(r.shape))
    per_chip = _exchange(chip_sums, FLIPS_CHIPS, 'scatter', 'grads_chip_scatter')
    halves = [_sum_parts_call(_two_d(parts, 1), 'grads_chip_sum').reshape(parts.shape[1:]) for parts in per_chip]
    both_halves = _exchange(halves, FLIPS_SIBLING, 'gather', 'grads_core_gather')
    out = {}
    for n, g in zip(sharded, both_halves):
        res = _adamw_call(_two_d(g, 0)[None], _two_d(w[n], 0), _two_d(m[n], 0), _two_d(v[n], 0), 'adamw_sharded')
        out[n] = tuple(r.reshape(w[n].shape) for r in res)

    small = [grads[n] for n in REPLICATED]
    all_parts = _exchange([_pack(small)], FLIPS_ALL, 'gather', 'grads_all_gather')[0]
    res = _adamw_call(all_parts, _pack([w[n] for n in REPLICATED]), _pack([m[n] for n in REPLICATED]),
                      _pack([v[n] for n in REPLICATED]), 'adamw_replicated')
    unpacked = [_unpack(r, small) for r in res]
    for i, n in enumerate(REPLICATED):
        out[n] = tuple(u[i] for u in unpacked)

    return (loss, grad_x[None]) + tuple(out[n][k] for k in range(4) for n in WEIGHT_NAMES)
```
